```python
import math
import jax
import jax.numpy as jnp
from jax import lax
import numpy as np

D_MODEL = 2048
BATCH = 8
SEQ = 2048
DEPTH = 2

GRID_W = 64
CTX_LEN = 256
N_EVEN = (DEPTH + 1) // 2
N_ODD = DEPTH // 2
N_MOD = 6
D_FF = 4 * D_MODEL
NORM_EPS = 1e-6
ROPE_BASE = 10000.0
NEG_INF = -1e30

MLA_HEADS = 8
MLA_Q_LORA = 512
MLA_KV_LORA = 512
MLA_NOPE = 128
MLA_ROPE = 64
MLA_V = 128
MLA_Q_BLOCK = 128
MLA_SCALE = 1.0 / math.sqrt(MLA_NOPE + MLA_ROPE)

GMLP_GROUPS = 8
GMLP_GROUP_DIM = 128
GMLP_CHUNK = 128

EVEN_SPLITS = (MLA_Q_LORA, MLA_Q_LORA + MLA_KV_LORA, MLA_Q_LORA + MLA_KV_LORA + MLA_ROPE)
EVEN_IN = EVEN_SPLITS[-1] + 2 * GMLP_GROUPS * GMLP_GROUP_DIM
EVEN_MIX = MLA_HEADS * MLA_V + GMLP_GROUPS * GMLP_GROUP_DIM

SWA_HEADS = 32
SWA_KV_HEADS = 4
SWA_GROUP = SWA_HEADS // SWA_KV_HEADS
SWA_HEAD_DIM = 64
SWA_WINDOW = 128
SWA_BLOCK = 128
SWA_SPAN = SWA_BLOCK + 2 * SWA_WINDOW
SWA_SCALE = 1.0 / math.sqrt(SWA_HEAD_DIM)
ODD_IN = (SWA_HEADS + 2 * SWA_KV_HEADS) * SWA_HEAD_DIM
ODD_MIX = SWA_HEADS * SWA_HEAD_DIM

kernel_name = 'hybrid_mla_gmlp_swa_dit_block'


def rms_norm(x, g):
    xf = x.astype(jnp.float32)
    y = xf * lax.rsqrt(jnp.mean(jnp.square(xf), axis=-1, keepdims=True) + NORM_EPS)
    return (y * g.astype(jnp.float32)).astype(x.dtype)


def group_layer_norm(x):
    xf = x.astype(jnp.float32)
    mu = jnp.mean(xf, axis=-1, keepdims=True)
    xc = xf - mu
    var = jnp.mean(jnp.square(xc), axis=-1, keepdims=True)
    return (xc * lax.rsqrt(var + NORM_EPS)).astype(x.dtype)


def axial_rope(n_tokens, rot_dim):
    rows = n_tokens // GRID_W
    t = jnp.arange(rows * GRID_W)
    row = (t // GRID_W).astype(jnp.float32)
    col = (t % GRID_W).astype(jnp.float32)
    n_freq = rot_dim // 4
    inv_freq = ROPE_BASE ** (-jnp.arange(n_freq, dtype=jnp.float32) / n_freq)
    ang = jnp.concatenate([row[:, None] * inv_freq, col[:, None] * inv_freq], axis=-1)
    return jnp.cos(ang), jnp.sin(ang)


def apply_rope(x, cos, sin):
    half = x.shape[-1] // 2
    x1 = x[..., :half].astype(jnp.float32)
    x2 = x[..., half:].astype(jnp.float32)
    return jnp.concatenate([x1 * cos - x2 * sin, x1 * sin + x2 * cos], axis=-1).astype(x.dtype)


def modulate(h, shift, scale):
    return h * (1 + scale) + shift


def squared_relu_mlp(h, w1, w2):
    return jnp.square(jax.nn.relu(h @ w1)) @ w2


def mla_attend(q_nope, q_rope, k_nope, k_rope, v):
    s = jnp.einsum('bqhd,bkhd->bhqk', q_nope, k_nope) + jnp.einsum('bqhr,bkr->bhqk', q_rope, k_rope)
    p = jax.nn.softmax(s.astype(jnp.float32) * MLA_SCALE, axis=-1)
    return jnp.einsum('bhqk,bkhd->bqhd', p.astype(v.dtype), v)


def chunk_gmlp(gm, w_sp, b_sp):
    b, n = gm.shape[:2]
    z = jax.nn.gelu(gm)
    u, v = jnp.split(z, 2, axis=-1)
    v = group_layer_norm(v.reshape(b, n // GMLP_CHUNK, GMLP_CHUNK, GMLP_GROUPS, GMLP_GROUP_DIM))
    mixed = jnp.einsum('gpq,bnqgc->bnpgc', w_sp, v) + b_sp.T[:, :, None]
    return u * mixed.reshape(b, n, GMLP_GROUPS * GMLP_GROUP_DIM)


def mla_gmlp_mixer(h_lat, h_ctx, w_in, q_norm_g, w_uq, kv_norm_g, w_ukv, w_sp, b_sp, w_out,
                   cos, sin, with_ctx_out):
    def project(h):
        b, n = h.shape[:2]
        z = h @ w_in
        cq, ckv, k_rope, gm = jnp.split(z, EVEN_SPLITS, axis=-1)
        q = (rms_norm(cq, q_norm_g) @ w_uq).reshape(b, n, MLA_HEADS, MLA_NOPE + MLA_ROPE)
        kv = (rms_norm(ckv, kv_norm_g) @ w_ukv).reshape(b, n, MLA_HEADS, MLA_NOPE + MLA_V)
        return q[..., :MLA_NOPE], q[..., MLA_NOPE:], kv[..., :MLA_NOPE], k_rope, kv[..., MLA_NOPE:], gm

    qn_l, qr_l, kn_l, kr_l, v_l, gm_l = project(h_lat)
    qn_c, qr_c, kn_c, kr_c, v_c, gm_c = project(h_ctx)
    qr_l = apply_rope(qr_l, cos[:, None, :], sin[:, None, :])
    kr_l = apply_rope(kr_l, cos, sin)
    kn_all = jnp.concatenate([kn_l, kn_c], axis=1)
    kr_all = jnp.concatenate([kr_l, kr_c], axis=1)
    v_all = jnp.concatenate([v_l, v_c], axis=1)
    b, n = h_lat.shape[:2]

    def q_block(i):
        start = i * MLA_Q_BLOCK
        qn = lax.dynamic_slice_in_dim(qn_l, start, MLA_Q_BLOCK, axis=1)
        qr = lax.dynamic_slice_in_dim(qr_l, start, MLA_Q_BLOCK, axis=1)
        return mla_attend(qn, qr, kn_all, kr_all, v_all)

    att_l = lax.map(q_block, jnp.arange(n // MLA_Q_BLOCK))
    att_l = jnp.moveaxis(att_l, 0, 1).reshape(b, n, MLA_HEADS * MLA_V)
    y_lat = jnp.concatenate([att_l, chunk_gmlp(gm_l, w_sp, b_sp)], axis=-1) @ w_out
    if not with_ctx_out:
        return y_lat, None
    att_c = mla_attend(qn_c, qr_c, kn_c, kr_c, v_c).reshape(b, h_ctx.shape[1], MLA_HEADS * MLA_V)
    y_ctx = jnp.concatenate([att_c, chunk_gmlp(gm_c, w_sp, b_sp)], axis=-1) @ w_out
    return y_lat, y_ctx


def sink_attend(q, k, v, sink, mask):
    s = jnp.einsum('bqkgd,bjkd->bkgqj', q, k).astype(jnp.float32) * SWA_SCALE
    if mask is not None:
        s = jnp.where(mask, s, NEG_INF)
    sk = jnp.broadcast_to(sink.astype(jnp.float32)[None, :, :, None, None], s.shape[:-1] + (1,))
    p = jax.nn.softmax(jnp.concatenate([s, sk], axis=-1), axis=-1)[..., :-1]
    return jnp.einsum('bkgqj,bjkd->bqkgd', p.astype(v.dtype), v)


def window_gqa_mixer(h_lat, h_ctx, w_in, sinks, w_out, cos, sin, with_ctx_out):
    def project(h):
        b, n = h.shape[:2]
        z = h @ w_in
        q, k, v = jnp.split(z, (SWA_HEADS * SWA_HEAD_DIM, (SWA_HEADS + SWA_KV_HEADS) * SWA_HEAD_DIM), axis=-1)
        return (q.reshape(b, n, SWA_KV_HEADS, SWA_GROUP, SWA_HEAD_DIM),
                k.reshape(b, n, SWA_KV_HEADS, SWA_HEAD_DIM),
                v.reshape(b, n, SWA_KV_HEADS, SWA_HEAD_DIM))

    q_l, k_l, v_l = project(h_lat)
    q_c, k_c, v_c = project(h_ctx)
    q_l = apply_rope(q_l, cos[:, None, None, :], sin[:, None, None, :])
    k_l = apply_rope(k_l, cos[:, None, :], sin[:, None, :])
    sink = sinks.reshape(SWA_KV_HEADS, SWA_GROUP)
    b, n = h_lat.shape[:2]
    n_ctx = h_ctx.shape[1]
    pad = ((0, 0), (SWA_WINDOW, SWA_WINDOW), (0, 0), (0, 0))
    k_pad = jnp.pad(k_l, pad)
    v_pad = jnp.pad(v_l, pad)
    qi = jnp.arange(SWA_BLOCK)[:, None]
    kj = jnp.arange(SWA_SPAN)[None, :]
    in_window = jnp.abs(kj - SWA_WINDOW - qi) <= SWA_WINDOW
    ctx_mask = jnp.ones((SWA_BLOCK, n_ctx), dtype=bool)

    def band_block(i):
        start = i * SWA_BLOCK
        qb = lax.dynamic_slice_in_dim(q_l, start, SWA_BLOCK, axis=1)
        kb = lax.dynamic_slice_in_dim(k_pad, start, SWA_SPAN, axis=1)
        vb = lax.dynamic_slice_in_dim(v_pad, start, SWA_SPAN, axis=1)
        pos = start - SWA_WINDOW + kj
        band = in_window & (pos >= 0) & (pos < n)
        mask = jnp.concatenate([band, ctx_mask], axis=1)
        return sink_attend(qb, jnp.concatenate([kb, k_c], axis=1), jnp.concatenate([vb, v_c], axis=1), sink, mask)

    out_l = lax.map(band_block, jnp.arange(n // SWA_BLOCK))
    out_l = jnp.moveaxis(out_l, 0, 1).reshape(b, n, ODD_MIX)
    y_lat = out_l @ w_out
    if not with_ctx_out:
        return y_lat, None
    out_c = sink_attend(q_c, k_c, v_c, sink, None).reshape(b, n_ctx, ODD_MIX)
    return y_lat, out_c @ w_out


def setup_inputs(seed: int = 0) -> dict:
    key = jax.random.key(seed)
    keys = iter(jax.random.split(key, 32))
    f32 = jnp.float32

    def normal(shape, scale):
        return jax.random.normal(next(keys), shape, f32) * scale

    def dense(shape, fan_in, scale=1.0):
        return normal(shape, scale * fan_in ** -0.5)

    def gain(shape):
        return 1.0 + normal(shape, 0.02)

    inputs = {}
    inputs['x'] = normal((BATCH, SEQ, D_MODEL), 1.0)
    inputs['c'] = normal((BATCH, D_MODEL), 1.0)
    inputs['ctx'] = normal((BATCH, CTX_LEN, D_MODEL), 1.0)
    inputs['c_ctx'] = normal((D_MODEL,), 1.0)
    inputs['norm1_g'] = gain((DEPTH, D_MODEL))
    inputs['w_mod'] = dense((DEPTH, D_MODEL, N_MOD * D_MODEL), D_MODEL, 0.5)
    inputs['b_mod'] = normal((DEPTH, N_MOD * D_MODEL), 0.02)
    inputs['norm2_g'] = gain((DEPTH, D_MODEL))
    inputs['w_ff1'] = dense((DEPTH, D_MODEL, D_FF), D_MODEL)
    inputs['w_ff2'] = dense((DEPTH, D_FF, D_MODEL), D_FF)
    inputs['even_w_in'] = dense((N_EVEN, D_MODEL, EVEN_IN), D_MODEL)
    inputs['mla_q_norm_g'] = gain((N_EVEN, MLA_Q_LORA))
    inputs['mla_w_uq'] = dense((N_EVEN, MLA_Q_LORA, MLA_HEADS * (MLA_NOPE + MLA_ROPE)), MLA_Q_LORA)
    inputs['mla_kv_norm_g'] = gain((N_EVEN, MLA_KV_LORA))
    inputs['mla_w_ukv'] = dense((N_EVEN, MLA_KV_LORA, MLA_HEADS * (MLA_NOPE + MLA_V)), MLA_KV_LORA)
    inputs['gmlp_w_sp'] = dense((N_EVEN, GMLP_GROUPS, GMLP_CHUNK, GMLP_CHUNK), GMLP_CHUNK, 0.5)
    inputs['gmlp_b_sp'] = gain((N_EVEN, GMLP_GROUPS, GMLP_CHUNK))
    inputs['even_w_out'] = dense((N_EVEN, EVEN_MIX, D_MODEL), EVEN_MIX)
    inputs['odd_w_in'] = dense((N_ODD, D_MODEL, ODD_IN), D_MODEL)
    inputs['swa_sinks'] = normal((N_ODD, SWA_HEADS), 0.5)
    inputs['odd_w_out'] = dense((N_ODD, ODD_MIX, D_MODEL), ODD_MIX)
    inputs['final_norm_g'] = gain((D_MODEL,))
    return inputs


def reference(x, c, ctx, c_ctx, norm1_g, w_mod, b_mod, norm2_g, w_ff1, w_ff2,
              even_w_in, mla_q_norm_g, mla_w_uq, mla_kv_norm_g, mla_w_ukv, gmlp_w_sp, gmlp_b_sp,
              even_w_out, odd_w_in, swa_sinks, odd_w_out, final_norm_g):
    n = x.shape[1]
    cos_m, sin_m = axial_rope(n, MLA_ROPE)
    cos_s, sin_s = axial_rope(n, SWA_HEAD_DIM)
    silu_c = jax.nn.silu(c)
    silu_cc = jax.nn.silu(c_ctx)
    xl, xc = x, ctx
    for layer in range(DEPTH):
        need_ctx = layer < DEPTH - 1
        j = layer // 2
        mod_l = jnp.split(silu_c @ w_mod[layer] + b_mod[layer], N_MOD, axis=-1)
        sh1, sc1, g1, sh2, sc2, g2 = [m[:, None, :] for m in mod_l]
        csh1, csc1, cg1, csh2, csc2, cg2 = jnp.split(silu_cc @ w_mod[layer] + b_mod[layer], N_MOD, axis=-1)
        hl = modulate(rms_norm(xl, norm1_g[layer]), sh1, sc1)
        hc = modulate(rms_norm(xc, norm1_g[layer]), csh1, csc1)
        if layer % 2 == 0:
            yl, yc = mla_gmlp_mixer(hl, hc, even_w_in[j], mla_q_norm_g[j], mla_w_uq[j], mla_kv_norm_g[j],
                                    mla_w_ukv[j], gmlp_w_sp[j], gmlp_b_sp[j], even_w_out[j],
                                    cos_m, sin_m, need_ctx)
        else:
            yl, yc = window_gqa_mixer(hl, hc, odd_w_in[j], swa_sinks[j], odd_w_out[j],
                                      cos_s, sin_s, need_ctx)
        xl = xl + g1 * yl
        hl = modulate(rms_norm(xl, norm2_g[layer]), sh2, sc2)
        xl = xl + g2 * squared_relu_mlp(hl, w_ff1[layer], w_ff2[layer])
        if need_ctx:
            xc = xc + cg1 * yc
            hc = modulate(rms_norm(xc, norm2_g[layer]), csh2, csc2)
            xc = xc + cg2 * squared_relu_mlp(hc, w_ff1[layer], w_ff2[layer])
    return rms_norm(xl, final_norm_g)
```

```python
import functools
import math

import jax
import jax.numpy as jnp
from jax import lax
from jax.experimental import pallas as pl
from jax.experimental.pallas import tpu as pltpu

F32 = jnp.float32
BF16 = jnp.bfloat16

LANES = 128
MIB = 1 << 20

GRID_W = 64
N_MOD = 6
NORM_EPS = 1e-6
ROPE_BASE = 10000.0
NEG_INF = -1e30

MLA_HEADS = 8
MLA_LORA = 512
MLA_NOPE = 128
MLA_ROPE = 64
MLA_V = 128
MLA_QK_PAD = 2 * LANES
MLA_SCALE = 1.0 / math.sqrt(MLA_NOPE + MLA_ROPE)

GMLP_GROUPS = 8
GMLP_DIM = 128
GMLP_CHUNK = 128
GMLP_WIDTH = GMLP_GROUPS * GMLP_DIM

SWA_HEADS = 32
SWA_KV_HEADS = 4
SWA_GROUP = SWA_HEADS // SWA_KV_HEADS
SWA_HEAD_DIM = 64
SWA_WINDOW = 128
SWA_BLOCK = 128
SWA_SPAN = SWA_BLOCK + 2 * SWA_WINDOW
SWA_SCALE = 1.0 / math.sqrt(SWA_HEAD_DIM)
SWA_Q_COLS = SWA_HEADS * SWA_HEAD_DIM
SWA_KV_COLS = SWA_KV_HEADS * SWA_HEAD_DIM

MOD_ROWS = 16
ROW_CHUNK = 256


def _params(semantics, vmem_mib):
    return pltpu.CompilerParams(dimension_semantics=semantics, vmem_limit_bytes=vmem_mib * MIB)


def _rms(x, g):
    return x * lax.rsqrt(jnp.mean(x * x, axis=-1, keepdims=True) + NORM_EPS) * g


def _for_row_chunks(n_rows, fn, rows=ROW_CHUNK):
    rows = min(rows, n_rows)

    def body(r, carry):
        fn(pl.ds(pl.multiple_of(r * rows, rows), rows))
        return carry

    lax.fori_loop(0, n_rows // rows, body, 0)


def _dot(a, b):
    return jnp.dot(a, b, preferred_element_type=F32)


def _dot_nt(a, b):
    return lax.dot_general(a, b, (((1,), (1,)), ((), ())), preferred_element_type=F32)


def _mod_kernel(c_ref, w_ref, b_ref, o_ref):
    c = c_ref[...]
    s = (c * jax.nn.sigmoid(c)).astype(BF16)
    o_ref[...] = _dot(s, w_ref[...].astype(BF16)) + b_ref[...]


def _modulation(cvec, w_mod, b_mod, *, tn=1024):
    depth, d, n = w_mod.shape
    return pl.pallas_call(
        _mod_kernel,
        out_shape=jax.ShapeDtypeStruct((depth, MOD_ROWS, n), F32),
        grid=(depth, n // tn),
        in_specs=[
            pl.BlockSpec((MOD_ROWS, d), lambda l, j: (0, 0)),
            pl.BlockSpec((None, d, tn), lambda l, j: (l, 0, j)),
            pl.BlockSpec((None, 1, tn), lambda l, j: (l, 0, j)),
        ],
        out_specs=pl.BlockSpec((None, MOD_ROWS, tn), lambda l, j: (l, 0, j)),
        compiler_params=_params(("parallel", "parallel"), 40),
        name="modulation",
    )(cvec, w_mod, b_mod.reshape(depth, 1, n))


def _proj_kernel(*refs, modulated, rope):
    it = iter(refs)
    x_ref, g_ref = next(it), next(it)
    sh_ref = sc_ref = None
    if modulated:
        sh_ref, sc_ref = next(it), next(it)
    w_ref = next(it)
    n_tabs = {None: 0, "fold": 2, "swap": 3}[rope[0] if rope else None]
    tabs = [next(it) for _ in range(n_tabs)]
    o_ref, h_ref = next(it), next(it)
    j = pl.program_id(1)

    @pl.when(j == 0)
    def _():
        def rows(sl):
            y = _rms(x_ref[sl, :], g_ref[...])
            if modulated:
                y = y * (1.0 + sc_ref[...]) + sh_ref[...]
            h_ref[sl, :] = y.astype(BF16)

        _for_row_chunks(x_ref.shape[0], rows)

    z = _dot(h_ref[...], w_ref[...])
    groups_per_tile = z.shape[1] // LANES

    if rope is None:
        o_ref[...] = z.astype(o_ref.dtype)
    elif rope[0] == "fold":
        _, tile, groups = rope
        o_ref[...] = z.astype(o_ref.dtype)

        def rotate():
            c_t, s_t = tabs[0][...], tabs[1][...]
            for gi in groups:
                sl = slice(gi * LANES, (gi + 1) * LANES)
                t = z[:, sl]
                o_ref[:, sl] = (t * c_t + pltpu.roll(t, LANES // 2, 1) * s_t).astype(o_ref.dtype)

        if tile is None:
            rotate()
        else:
            pl.when(j == tile)(rotate)
    else:
        _, ncols = rope
        tn = z.shape[1]
        full_tiles, rem_groups = ncols // tn, (ncols % tn) // LANES
        quarter = SWA_HEAD_DIM // 2

        def emit(n_rot):
            c_t, a_t, b_t = tabs[0][...], tabs[1][...], tabs[2][...]
            for gi in range(groups_per_tile):
                sl = slice(gi * LANES, (gi + 1) * LANES)
                t = z[:, sl]
                if gi < n_rot:
                    t = (t * c_t + pltpu.roll(t, LANES - quarter, 1) * a_t
                         + pltpu.roll(t, quarter, 1) * b_t)
                o_ref[:, sl] = t.astype(o_ref.dtype)

        pl.when(j < full_tiles)(lambda: emit(groups_per_tile))
        pl.when(j == full_tiles)(lambda: emit(rem_groups))
        pl.when(j > full_tiles)(lambda: emit(0))


def _proj(x, xcol, k, g, w, *, tm, tn, out_dtype, mod=None, rope=None, tabs=(), pos_blocks=1,
          vmem_mib=48, name="proj"):
    t = x.shape[0]
    n = w.shape[1]
    assert t % tm == 0 and n % tn == 0 and w.shape[0] == k
    in_specs = [
        pl.BlockSpec((tm, k), lambda i, j: (i, xcol)),
        pl.BlockSpec((1, k), lambda i, j: (0, 0)),
    ]
    args = [x, g.reshape(1, k)]
    if mod is not None:
        mods, row_of, k_shift, k_scale = mod
        in_specs += [
            pl.BlockSpec((None, 1, k), lambda i, j: (row_of(i), 0, k_shift)),
            pl.BlockSpec((None, 1, k), lambda i, j: (row_of(i), 0, k_scale)),
        ]
        args += [mods, mods]
    in_specs.append(pl.BlockSpec((k, tn), lambda i, j: (0, j)))
    args.append(w)
    for tab in tabs:
        in_specs.append(pl.BlockSpec((tm, LANES), lambda i, j: (i % pos_blocks, 0)))
        args.append(tab)
    return pl.pallas_call(
        functools.partial(_proj_kernel, modulated=mod is not None, rope=rope),
        out_shape=jax.ShapeDtypeStruct((t, n), out_dtype),
        grid=(t // tm, n // tn),
        in_specs=in_specs,
        out_specs=pl.BlockSpec((tm, tn), lambda i, j: (i, j)),
        scratch_shapes=[pltpu.VMEM((tm, k), BF16)],
        compiler_params=_params(("parallel", "arbitrary"), vmem_mib),
        name=name,
    )(*args)


def _mla_kernel(*refs, n_lat, n_ctx):
    if n_lat:
        q_ref, knl_ref, vl_ref, krl_ref, knc_ref, vc_ref, krc_ref, o_ref, k_scr, v_scr = refs
    else:
        q_ref, knc_ref, vc_ref, krc_ref, o_ref, k_scr, v_scr = refs

    @pl.when(pl.program_id(2) == 0)
    def _():
        if n_lat:
            k_scr[0:n_lat, 0:LANES] = knl_ref[...]
            k_scr[0:n_lat, LANES:] = krl_ref[...].astype(BF16)
            v_scr[0:n_lat, :] = vl_ref[...]
        k_scr[n_lat:, 0:LANES] = knc_ref[...]
        k_scr[n_lat:, LANES:] = krc_ref[...].astype(BF16)
        v_scr[n_lat:, :] = vc_ref[...]

    s = _dot_nt(q_ref[...], k_scr[...])
    m = jnp.max(s, axis=-1, keepdims=True)
    p = jnp.exp((s - m) * MLA_SCALE)
    denom = jnp.sum(p, axis=-1, keepdims=True)
    o = _dot(p.astype(BF16), v_scr[...])
    o_ref[...] = (o / denom).astype(o_ref.dtype)


def _mla_attention(q, kv_c, z_c, kr_col, *, batch, n_q, n_ctx, kv_l=None, z_l=None, tq=256):
    n_lat = 0 if kv_l is None else kv_l.shape[0] // batch
    nq_blocks = n_q // tq
    in_specs = [pl.BlockSpec((tq, MLA_QK_PAD), lambda b, h, i: (b * nq_blocks + i, h))]
    args = [q]
    if n_lat:
        in_specs += [
            pl.BlockSpec((n_lat, LANES), lambda b, h, i: (b, 2 * h)),
            pl.BlockSpec((n_lat, LANES), lambda b, h, i: (b, 2 * h + 1)),
            pl.BlockSpec((n_lat, LANES), lambda b, h, i: (b, kr_col)),
        ]
        args += [kv_l, kv_l, z_l]
    in_specs += [
        pl.BlockSpec((n_ctx, LANES), lambda b, h, i: (b, 2 * h)),
        pl.BlockSpec((n_ctx, LANES), lambda b, h, i: (b, 2 * h + 1)),
        pl.BlockSpec((n_ctx, LANES), lambda b, h, i: (b, kr_col)),
    ]
    args += [kv_c, kv_c, z_c]
    n_keys = n_lat + n_ctx
    return pl.pallas_call(
        functools.partial(_mla_kernel, n_lat=n_lat, n_ctx=n_ctx),
        out_shape=jax.ShapeDtypeStruct((batch * n_q, MLA_HEADS * MLA_V), BF16),
        grid=(batch, MLA_HEADS, nq_blocks),
        in_specs=in_specs,
        out_specs=pl.BlockSpec((tq, MLA_V), lambda b, h, i: (b * nq_blocks + i, h)),
        scratch_shapes=[pltpu.VMEM((n_keys, MLA_QK_PAD), BF16), pltpu.VMEM((n_keys, MLA_V), BF16)],
        compiler_params=_params(("parallel", "parallel", "arbitrary"), 48),
        name="mla_attention",
    )(*args)


def _gmlp_kernel(u_ref, v_ref, w_ref, b_ref, o_ref):
    for n in range(u_ref.shape[0] // GMLP_CHUNK):
        rows = slice(n * GMLP_CHUNK, (n + 1) * GMLP_CHUNK)
        for g in range(GMLP_GROUPS):
            cols = slice(g * GMLP_DIM, (g + 1) * GMLP_DIM)
            v = jax.nn.gelu(v_ref[rows, cols])
            vc = v - jnp.mean(v, axis=-1, keepdims=True)
            var = jnp.mean(vc * vc, axis=-1, keepdims=True)
            vn = (vc * lax.rsqrt(var + NORM_EPS)).astype(BF16)
            mixed = _dot(w_ref[g], vn) + b_ref[g]
            o_ref[rows, cols] = (jax.nn.gelu(u_ref[rows, cols]) * mixed).astype(o_ref.dtype)


def _gmlp(z, u_col, v_col, w_sp, b_sp, *, tm=256):
    t = z.shape[0]
    return pl.pallas_call(
        _gmlp_kernel,
        out_shape=jax.ShapeDtypeStruct((t, GMLP_WIDTH), BF16),
        grid=(t // tm,),
        in_specs=[
            pl.BlockSpec((tm, GMLP_WIDTH), lambda i: (i, u_col)),
            pl.BlockSpec((tm, GMLP_WIDTH), lambda i: (i, v_col)),
            pl.BlockSpec((GMLP_GROUPS, GMLP_CHUNK, GMLP_CHUNK), lambda i: (0, 0, 0)),
            pl.BlockSpec((GMLP_GROUPS, GMLP_CHUNK, 1), lambda i: (0, 0, 0)),
        ],
        out_specs=pl.BlockSpec((tm, GMLP_WIDTH), lambda i: (i, 0)),
        compiler_params=_params(("parallel",), 32),
        name="gmlp",
    )(z, z, w_sp, b_sp)


def _swa_kernel(sink_ref, q_ref, kl_ref, vl_ref, kc_ref, vc_ref, o_ref, *, tq, n_lat):
    start = pl.program_id(1) * tq
    span0 = pl.multiple_of(jnp.clip(start - SWA_WINDOW, 0, n_lat - SWA_SPAN), SWA_WINDOW)
    k_span = kl_ref[pl.ds(span0, SWA_SPAN), :]
    v_span = vl_ref[pl.ds(span0, SWA_SPAN), :]
    k_ctx, v_ctx = kc_ref[...], vc_ref[...]

    q_pos = start + lax.broadcasted_iota(jnp.int32, (tq, SWA_SPAN), 0)
    k_pos = span0 + lax.broadcasted_iota(jnp.int32, (tq, SWA_SPAN), 1)
    bias = jnp.where(jnp.abs(k_pos - q_pos) <= SWA_WINDOW, 0.0, NEG_INF).astype(F32)
    bias = jnp.concatenate([bias] * SWA_GROUP, axis=0)

    for g in range(SWA_KV_HEADS):
        heads = range(g * SWA_GROUP, (g + 1) * SWA_GROUP)
        kv_cols = slice(g * SWA_HEAD_DIM, (g + 1) * SWA_HEAD_DIM)
        q = jnp.concatenate(
            [q_ref[:, h * SWA_HEAD_DIM:(h + 1) * SWA_HEAD_DIM] for h in heads], axis=0)
        sink = jnp.concatenate([jnp.full((tq, 1), sink_ref[h], F32) for h in heads], axis=0)
        s_l = _dot_nt(q, k_span[:, kv_cols]) + bias
        s_c = _dot_nt(q, k_ctx[:, kv_cols])
        m = jnp.maximum(jnp.maximum(jnp.max(s_l, axis=-1, keepdims=True),
                                    jnp.max(s_c, axis=-1, keepdims=True)), sink)
        p_l = jnp.exp(s_l - m)
        p_c = jnp.exp(s_c - m)
        denom = (jnp.sum(p_l, axis=-1, keepdims=True) + jnp.sum(p_c, axis=-1, keepdims=True)
                 + jnp.exp(sink - m))
        o = _dot(p_l.astype(BF16), v_span[:, kv_cols]) + _dot(p_c.astype(BF16), v_ctx[:, kv_cols])
        o = o / denom
        for r, h in enumerate(heads):
            o_ref[:, h * SWA_HEAD_DIM:(h + 1) * SWA_HEAD_DIM] = (
                o[r * tq:(r + 1) * tq].astype(o_ref.dtype))


def _swa_attention(z_l, z_c, sinks, *, batch, n_lat, n_ctx, tq=SWA_BLOCK):
    nq_blocks = n_lat // tq
    k_blk = SWA_Q_COLS // SWA_KV_COLS
    return pl.pallas_call(
        functools.partial(_swa_kernel, tq=tq, n_lat=n_lat),
        out_shape=jax.ShapeDtypeStruct((batch * n_lat, SWA_Q_COLS), BF16),
        grid=(batch, nq_blocks),
        in_specs=[
            pl.BlockSpec(memory_space=pltpu.SMEM),
            pl.BlockSpec((tq, SWA_Q_COLS), lambda b, i: (b * nq_blocks + i, 0)),
            pl.BlockSpec((n_lat, SWA_KV_COLS), lambda b, i: (b, k_blk)),
            pl.BlockSpec((n_lat, SWA_KV_COLS), lambda b, i: (b, k_blk + 1)),
            pl.BlockSpec((n_ctx, SWA_KV_COLS), lambda b, i: (b, 0)),
            pl.BlockSpec((n_ctx, SWA_KV_COLS), lambda b, i: (b, 1)),
        ],
        out_specs=pl.BlockSpec((tq, SWA_Q_COLS), lambda b, i: (b * nq_blocks + i, 0)),
        compiler_params=_params(("parallel", "arbitrary"), 48),
        name="swa_attention",
    )(sinks, z_l, z_l, z_l, z_c, z_c)


def _outproj_kernel(a1_ref, a2_ref, w1_ref, w2_ref, x_ref, gate_ref, o_ref):
    y = _dot(a1_ref[...], w1_ref[...]) + _dot(a2_ref[...], w2_ref[...])
    o_ref[...] = x_ref[...] + gate_ref[...] * y


def _outproj(a1, a1_col, a2, a2_col, w, x, mods, row_of, k_gate, *, tm=512):
    t, d = x.shape
    half = w.shape[0] // 2
    return pl.pallas_call(
        _outproj_kernel,
        out_shape=jax.ShapeDtypeStruct((t, d), F32),
        grid=(t // tm,),
        in_specs=[
            pl.BlockSpec((tm, half), lambda i: (i, a1_col)),
            pl.BlockSpec((tm, half), lambda i: (i, a2_col)),
            pl.BlockSpec((half, d), lambda i: (0, 0)),
            pl.BlockSpec((half, d), lambda i: (1, 0)),
            pl.BlockSpec((tm, d), lambda i: (i, 0)),
            pl.BlockSpec((None, 1, d), lambda i: (row_of(i), 0, k_gate)),
        ],
        out_specs=pl.BlockSpec((tm, d), lambda i: (i, 0)),
        compiler_params=_params(("parallel",), 48),
        name="outproj",
    )(a1, a2, w, w, x, mods)


def _mlp_kernel(*refs, final):
    if final:
        x_ref, g_ref, sh_ref, sc_ref, gate_ref, w1_ref, w2_ref, fg_ref, o_ref, h_ref = refs
    else:
        x_ref, g_ref, sh_ref, sc_ref, gate_ref, w1_ref, w2_ref, o_ref, h_ref = refs
    k = pl.program_id(1)

    @pl.when(k == 0)
    def _():
        def rows(sl):
            y = _rms(x_ref[sl, :], g_ref[...]) * (1.0 + sc_ref[...]) + sh_ref[...]
            h_ref[sl, :] = y.astype(BF16)

        _for_row_chunks(x_ref.shape[0], rows)

    a = jnp.square(jnp.maximum(_dot(h_ref[...], w1_ref[...]), 0.0)).astype(BF16)
    y = _dot(a, w2_ref[...])

    @pl.when(k == 0)
    def _():
        o_ref[...] = y

    @pl.when(k > 0)
    def _():
        o_ref[...] += y

    @pl.when(k == pl.num_programs(1) - 1)
    def _():
        def rows(sl):
            out = x_ref[sl, :] + gate_ref[...] * o_ref[sl, :]
            if final:
                out = _rms(out, fg_ref[...])
            o_ref[sl, :] = out

        _for_row_chunks(x_ref.shape[0], rows)


def _mlp(x, g, mods, row_of, w1, w2, *, final_g=None, tm=512, tf=512):
    t, d = x.shape
    f = w1.shape[1]
    in_specs = [
        pl.BlockSpec((tm, d), lambda i, k: (i, 0)),
        pl.BlockSpec((1, d), lambda i, k: (0, 0)),
        pl.BlockSpec((None, 1, d), lambda i, k: (row_of(i), 0, 3)),
        pl.BlockSpec((None, 1, d), lambda i, k: (row_of(i), 0, 4)),
        pl.BlockSpec((None, 1, d), lambda i, k: (row_of(i), 0, 5)),
        pl.BlockSpec((d, tf), lambda i, k: (0, k)),
        pl.BlockSpec((tf, d), lambda i, k: (k, 0)),
    ]
    args = [x, g.reshape(1, d), mods, mods, mods, w1, w2]
    if final_g is not None:
        in_specs.append(pl.BlockSpec((1, d), lambda i, k: (0, 0)))
        args.append(final_g.reshape(1, d))
    return pl.pallas_call(
        functools.partial(_mlp_kernel, final=final_g is not None),
        out_shape=jax.ShapeDtypeStruct((t, d), F32),
        grid=(t // tm, f // tf),
        in_specs=in_specs,
        out_specs=pl.BlockSpec((tm, d), lambda i, k: (i, 0)),
        scratch_shapes=[pltpu.VMEM((tm, d), BF16)],
        compiler_params=_params(("parallel", "arbitrary"), 48),
        name="mlp",
    )(*args)


def _axial_angles(n_tokens, rot_dim):
    t = jnp.arange(n_tokens)
    row = (t // GRID_W).astype(F32)
    col = (t % GRID_W).astype(F32)
    n_freq = rot_dim // 4
    inv_freq = ROPE_BASE ** (-jnp.arange(n_freq, dtype=F32) / n_freq)
    ang = jnp.concatenate([row[:, None] * inv_freq, col[:, None] * inv_freq], axis=-1)
    return jnp.cos(ang), jnp.sin(ang)


def _swap_halves(w):
    half = w.shape[-1] // 2
    return jnp.concatenate([w[..., half:], w[..., :half]], axis=-1)


def kernel(x, c, ctx, c_ctx, norm1_g, w_mod, b_mod, norm2_g, w_ff1, w_ff2, even_w_in, mla_q_norm_g,
           mla_w_uq, mla_kv_norm_g, mla_w_ukv, gmlp_w_sp, gmlp_b_sp, even_w_out, odd_w_in, swa_sinks,
           odd_w_out, final_norm_g):
    batch, n_lat, d = x.shape
    n_ctx = ctx.shape[1]
    assert batch + 1 <= MOD_ROWS
    xl = x.reshape(batch * n_lat, d)
    xc = ctx.reshape(batch * n_ctx, d)

    tm = 1024
    lat_blocks = n_lat // tm

    def lat_row(block_rows):
        per_sample = n_lat // block_rows
        return lambda i: i // per_sample

    ctx_row = lambda i: batch

    cvec = jnp.concatenate([c, c_ctx[None, :], jnp.zeros((MOD_ROWS - batch - 1, d), F32)], axis=0)
    mods = _modulation(cvec, w_mod, b_mod)
    mods = mods.reshape(mods.shape[0], MOD_ROWS, 1, N_MOD * d)

    cos_m, sin_m = _axial_angles(n_lat, MLA_ROPE)
    zeros64 = jnp.zeros((n_lat, LANES // 2), F32)
    fold_cos = jnp.concatenate([cos_m, cos_m, zeros64], axis=-1)
    fold_sin = jnp.concatenate([-sin_m, sin_m, zeros64], axis=-1)
    keep = jnp.concatenate([jnp.ones((tm, LANES // 2), F32), jnp.zeros((tm, LANES // 2), F32)], axis=-1)
    drop = jnp.zeros((tm, LANES), F32)

    w_in = even_w_in[0]
    cq_w, ckv_w = w_in[:, :MLA_LORA], w_in[:, MLA_LORA:2 * MLA_LORA]
    kr_w = w_in[:, 2 * MLA_LORA:2 * MLA_LORA + MLA_ROPE]
    gm_w = w_in[:, 2 * MLA_LORA + MLA_ROPE:]
    w_in0 = jnp.concatenate([cq_w, ckv_w, gm_w, kr_w, _swap_halves(kr_w)], axis=-1).astype(BF16)
    n_in0 = w_in0.shape[1]
    tn0 = 640
    kr_group = (n_in0 - LANES) // LANES
    kr_rope = ("fold", kr_group * LANES // tn0, ((kr_group * LANES % tn0) // LANES,))

    wq = mla_w_uq[0].reshape(MLA_LORA, MLA_HEADS, MLA_NOPE + MLA_ROPE)
    wq_rope = wq[..., MLA_NOPE:]
    w_uq = jnp.concatenate([wq[..., :MLA_NOPE], wq_rope, _swap_halves(wq_rope)], axis=-1)
    w_uq = w_uq.reshape(MLA_LORA, MLA_HEADS * MLA_QK_PAD).astype(BF16)
    q_rope = ("fold", None, tuple(range(1, 2 * MLA_HEADS, 2)))
    w_ukv = mla_w_ukv[0].astype(BF16)
    w_sp = gmlp_w_sp[0].astype(BF16)
    b_sp = gmlp_b_sp[0][:, :, None]
    w_out0 = even_w_out[0].astype(BF16)
    w1_0, w2_0 = w_ff1[0].astype(BF16), w_ff2[0].astype(BF16)

    def layer0_tokens(xs, row_of, tabs, pos_blocks):
        m0 = mods[0]
        z = _proj(xs, 0, d, norm1_g[0], w_in0, tm=tm, tn=tn0, out_dtype=F32,
                  mod=(m0, row_of, 0, 1), rope=kr_rope, tabs=tabs, pos_blocks=pos_blocks,
                  vmem_mib=52, name="even_in_proj")
        q = _proj(z, 0, MLA_LORA, mla_q_norm_g[0], w_uq, tm=tm, tn=w_uq.shape[1], out_dtype=BF16,
                  rope=q_rope, tabs=tabs, pos_blocks=pos_blocks, name="mla_q_proj")
        kv = _proj(z, 1, MLA_LORA, mla_kv_norm_g[0], w_ukv, tm=tm, tn=w_ukv.shape[1], out_dtype=BF16,
                   name="mla_kv_proj")
        gm = _gmlp(z, 1, 2, w_sp, b_sp)
        return z, q, kv, gm

    z_l, q_l, kv_l, gm_l = layer0_tokens(xl, lat_row(tm), (fold_cos, fold_sin), lat_blocks)
    z_c, q_c, kv_c, gm_c = layer0_tokens(xc, ctx_row, (keep, drop), 1)
    att_l = _mla_attention(q_l, kv_c, z_c, kr_group, batch=batch, n_q=n_lat, n_ctx=n_ctx,
                           kv_l=kv_l, z_l=z_l)
    att_c = _mla_attention(q_c, kv_c, z_c, kr_group, batch=batch, n_q=n_ctx, n_ctx=n_ctx)

    tm_o = 512
    xl = _outproj(att_l, 0, gm_l, 0, w_out0, xl, mods[0], lat_row(tm_o), 2, tm=tm_o)
    xc = _outproj(att_c, 0, gm_c, 0, w_out0, xc, mods[0], ctx_row, 2, tm=tm_o)
    tm_f = 512
    xl = _mlp(xl, norm2_g[0], mods[0], lat_row(tm_f), w1_0, w2_0, tm=tm_f)
    xc = _mlp(xc, norm2_g[0], mods[0], ctx_row, w1_0, w2_0, tm=tm_f)

    cos_s, sin_s = _axial_angles(n_lat, SWA_HEAD_DIM)
    zeros32 = jnp.zeros_like(sin_s)
    swap_cos = jnp.concatenate([cos_s] * 4, axis=-1)
    swap_up = jnp.concatenate([-sin_s, zeros32, -sin_s, zeros32], axis=-1)
    swap_down = jnp.concatenate([zeros32, sin_s, zeros32, sin_s], axis=-1)

    w_in = odd_w_in[0]
    w_in1 = jnp.concatenate([w_in[:, :SWA_Q_COLS] * SWA_SCALE, w_in[:, SWA_Q_COLS:]], axis=-1).astype(BF16)
    w_kv1 = w_in[:, SWA_Q_COLS:].astype(BF16)
    m1 = mods[1]
    z1_l = _proj(xl, 0, d, norm1_g[1], w_in1, tm=tm, tn=1280, out_dtype=BF16,
                 mod=(m1, lat_row(tm), 0, 1), rope=("swap", SWA_Q_COLS + SWA_KV_COLS),
                 tabs=(swap_cos, swap_up, swap_down), pos_blocks=lat_blocks, name="odd_in_proj")
    z1_c = _proj(xc, 0, d, norm1_g[1], w_kv1, tm=tm, tn=w_kv1.shape[1], out_dtype=BF16,
                 mod=(m1, ctx_row, 0, 1), name="odd_ctx_kv_proj")
    att = _swa_attention(z1_l, z1_c, swa_sinks[0], batch=batch, n_lat=n_lat, n_ctx=n_ctx)
    xl = _outproj(att, 0, att, 1, odd_w_out[0].astype(BF16), xl, m1, lat_row(tm_o), 2, tm=tm_o)
    out = _mlp(xl, norm2_g[1], m1, lat_row(tm_f), w_ff1[1].astype(BF16), w_ff2[1].astype(BF16),
               final_g=final_norm_g, tm=tm_f)
    return out.reshape(batch, n_lat, d)
```

```python
import functools
import math

import jax
import jax.numpy as jnp
from jax import lax
from jax.experimental import pallas as pl
from jax.experimental.pallas import tpu as pltpu

F32 = jnp.float32
BF16 = jnp.bfloat16

LANES = 128
MIB = 1 << 20

GRID_W = 64
N_MOD = 6
NORM_EPS = 1e-6
ROPE_BASE = 10000.0
NEG_INF = -1e30

MLA_HEADS = 8
MLA_LORA = 512
MLA_NOPE = 128
MLA_ROPE = 64
MLA_V = 128
MLA_QK_PAD = 2 * LANES
MLA_SCALE = 1.0 / math.sqrt(MLA_NOPE + MLA_ROPE)
MLA_Q_SUB = 512

GMLP_GROUPS = 8
GMLP_DIM = 128
GMLP_CHUNK = 128
GMLP_WIDTH = GMLP_GROUPS * GMLP_DIM

SWA_HEADS = 32
SWA_KV_HEADS = 4
SWA_GROUP = SWA_HEADS // SWA_KV_HEADS
SWA_HEAD_DIM = 64
SWA_WINDOW = 128
SWA_BLOCK = 128
SWA_SPAN = SWA_BLOCK + 2 * SWA_WINDOW
SWA_SCALE = 1.0 / math.sqrt(SWA_HEAD_DIM)
SWA_Q_COLS = SWA_HEADS * SWA_HEAD_DIM
SWA_KV_COLS = SWA_KV_HEADS * SWA_HEAD_DIM

MOD_ROWS = 16
ROW_CHUNK = 256
ONES_ROWS = 16
LOG2E = math.log2(math.e)


def _params(semantics, vmem_mib):
    return pltpu.CompilerParams(dimension_semantics=semantics, vmem_limit_bytes=vmem_mib * MIB)


def _rms(x, g):
    return x * lax.rsqrt(jnp.mean(x * x, axis=-1, keepdims=True) + NORM_EPS) * g


def _for_row_chunks(n_rows, fn, rows=ROW_CHUNK):
    rows = min(rows, n_rows)

    def body(r, carry):
        fn(pl.ds(pl.multiple_of(r * rows, rows), rows))
        return carry

    lax.fori_loop(0, n_rows // rows, body, 0)


def _dot(a, b):
    return jnp.dot(a, b, preferred_element_type=F32)


def _dot_nt(a, b):
    return lax.dot_general(a, b, (((1,), (1,)), ((), ())), preferred_element_type=F32)


def _mod_kernel(c_ref, w_ref, b_ref, o_ref):
    c = c_ref[...]
    s = (c * jax.nn.sigmoid(c)).astype(BF16)
    o_ref[...] = _dot(s, w_ref[...].astype(BF16)) + b_ref[...]


def _modulation(cvec, w_mod, b_mod, *, tn=1024):
    depth, d, n = w_mod.shape
    return pl.pallas_call(
        _mod_kernel,
        out_shape=jax.ShapeDtypeStruct((depth, MOD_ROWS, n), F32),
        grid=(depth, n // tn),
        in_specs=[
            pl.BlockSpec((MOD_ROWS, d), lambda l, j: (0, 0)),
            pl.BlockSpec((None, d, tn), lambda l, j: (l, 0, j)),
            pl.BlockSpec((None, 1, tn), lambda l, j: (l, 0, j)),
        ],
        out_specs=pl.BlockSpec((None, MOD_ROWS, tn), lambda l, j: (l, 0, j)),
        compiler_params=_params(("parallel", "parallel"), 40),
        name="modulation",
    )(cvec, w_mod, b_mod.reshape(depth, 1, n))


def _proj_kernel(*refs, modulated, rope):
    it = iter(refs)
    x_ref, g_ref = next(it), next(it)
    sh_ref = sc_ref = None
    if modulated:
        sh_ref, sc_ref = next(it), next(it)
    w_ref = next(it)
    tabs = [next(it) for _ in range(2 if rope else 0)]
    o_ref, h_ref = next(it), next(it)
    j = pl.program_id(1)

    @pl.when(j == 0)
    def _():
        def rows(sl):
            y = _rms(x_ref[sl, :], g_ref[...])
            if modulated:
                y = y * (1.0 + sc_ref[...]) + sh_ref[...]
            h_ref[sl, :] = y.astype(BF16)

        _for_row_chunks(x_ref.shape[0], rows)

    z = _dot(h_ref[...], w_ref[...])
    groups_per_tile = z.shape[1] // LANES

    if rope is None:
        o_ref[...] = z.astype(o_ref.dtype)
        return

    tile, groups = rope

    def rotated(t):
        return t * tabs[0][...] + pltpu.roll(t, LANES // 2, 1) * tabs[1][...]

    if tile is None:
        for gi in range(groups_per_tile):
            sl = slice(gi * LANES, (gi + 1) * LANES)
            t = z[:, sl]
            o_ref[:, sl] = (rotated(t) if gi in groups else t).astype(o_ref.dtype)
    else:
        o_ref[...] = z.astype(o_ref.dtype)

        @pl.when(j == tile)
        def _():
            for gi in groups:
                sl = slice(gi * LANES, (gi + 1) * LANES)
                o_ref[:, sl] = rotated(z[:, sl]).astype(o_ref.dtype)


def _proj(x, xcol, k, g, w, *, tm, tn, out_dtype, mod=None, rope=None, tabs=(), pos_blocks=1,
          vmem_mib=48, name="proj"):
    t = x.shape[0]
    n = w.shape[1]
    assert t % tm == 0 and n % tn == 0 and w.shape[0] == k
    in_specs = [
        pl.BlockSpec((tm, k), lambda i, j: (i, xcol)),
        pl.BlockSpec((1, k), lambda i, j: (0, 0)),
    ]
    args = [x, g.reshape(1, k)]
    if mod is not None:
        mods, row_of, k_shift, k_scale = mod
        in_specs += [
            pl.BlockSpec((None, 1, k), lambda i, j: (row_of(i), 0, k_shift)),
            pl.BlockSpec((None, 1, k), lambda i, j: (row_of(i), 0, k_scale)),
        ]
        args += [mods, mods]
    in_specs.append(pl.BlockSpec((k, tn), lambda i, j: (0, j)))
    args.append(w)
    for tab in tabs:
        in_specs.append(pl.BlockSpec((tm, LANES), lambda i, j: (i % pos_blocks, 0)))
        args.append(tab)
    return pl.pallas_call(
        functools.partial(_proj_kernel, modulated=mod is not None, rope=rope),
        out_shape=jax.ShapeDtypeStruct((t, n), out_dtype),
        grid=(t // tm, n // tn),
        in_specs=in_specs,
        out_specs=pl.BlockSpec((tm, tn), lambda i, j: (i, j)),
        scratch_shapes=[pltpu.VMEM((tm, k), BF16)],
        compiler_params=_params(("parallel", "arbitrary"), vmem_mib),
        name=name,
    )(*args)


def _mla_kernel(*refs, n_lat, n_ctx):
    if n_lat:
        q_ref, knl_ref, vl_ref, krl_ref, knc_ref, vc_ref, krc_ref, o_ref, k_scr, vt_scr = refs
    else:
        q_ref, knc_ref, vc_ref, krc_ref, o_ref, k_scr, vt_scr = refs

    @pl.when(pl.program_id(2) == 0)
    def _():
        if n_lat:
            k_scr[0:n_lat, 0:LANES] = knl_ref[...]
            k_scr[0:n_lat, LANES:] = krl_ref[...].astype(BF16)
            vt_scr[0:MLA_V, 0:n_lat] = vl_ref[...].astype(F32).T.astype(BF16)
        k_scr[n_lat:, 0:LANES] = knc_ref[...]
        k_scr[n_lat:, LANES:] = krc_ref[...].astype(BF16)
        vt_scr[0:MLA_V, n_lat:] = vc_ref[...].astype(F32).T.astype(BF16)
        vt_scr[MLA_V:, :] = jnp.ones((ONES_ROWS, n_lat + n_ctx), BF16)

    n_sub = min(MLA_Q_SUB, q_ref.shape[0])
    subs = [slice(r, r + n_sub) for r in range(0, q_ref.shape[0], n_sub)]
    s_next = _dot_nt(k_scr[...], q_ref[subs[0], :])
    for i, rows in enumerate(subs):
        s = s_next
        if i + 1 < len(subs):
            s_next = _dot_nt(k_scr[...], q_ref[subs[i + 1], :])
        p = jnp.exp2(s - jnp.max(s, axis=0, keepdims=True))
        o = _dot(vt_scr[...], p.astype(BF16))
        o_ref[rows, :] = (o[0:MLA_V] / o[MLA_V:MLA_V + 1]).T.astype(o_ref.dtype)


def _mla_attention(q, kv_c, z_c, kr_col, *, batch, n_q, n_ctx, kv_l=None, z_l=None, tq=256):
    n_lat = 0 if kv_l is None else kv_l.shape[0] // batch
    nq_blocks = n_q // tq
    in_specs = [pl.BlockSpec((tq, MLA_QK_PAD), lambda b, h, i: (b * nq_blocks + i, h))]
    args = [q]
    if n_lat:
        in_specs += [
            pl.BlockSpec((n_lat, LANES), lambda b, h, i: (b, 2 * h)),
            pl.BlockSpec((n_lat, LANES), lambda b, h, i: (b, 2 * h + 1)),
            pl.BlockSpec((n_lat, LANES), lambda b, h, i: (b, kr_col)),
        ]
        args += [kv_l, kv_l, z_l]
    in_specs += [
        pl.BlockSpec((n_ctx, LANES), lambda b, h, i: (b, 2 * h)),
        pl.BlockSpec((n_ctx, LANES), lambda b, h, i: (b, 2 * h + 1)),
        pl.BlockSpec((n_ctx, LANES), lambda b, h, i: (b, kr_col)),
    ]
    args += [kv_c, kv_c, z_c]
    n_keys = n_lat + n_ctx
    return pl.pallas_call(
        functools.partial(_mla_kernel, n_lat=n_lat, n_ctx=n_ctx),
        out_shape=jax.ShapeDtypeStruct((batch * n_q, MLA_HEADS * MLA_V), BF16),
        grid=(batch, MLA_HEADS, nq_blocks),
        in_specs=in_specs,
        out_specs=pl.BlockSpec((tq, MLA_V), lambda b, h, i: (b * nq_blocks + i, h)),
        scratch_shapes=[pltpu.VMEM((n_keys, MLA_QK_PAD), BF16),
                        pltpu.VMEM((MLA_V + ONES_ROWS, n_keys), BF16)],
        compiler_params=_params(("parallel", "parallel", "arbitrary"), 48),
        name="mla_attention",
    )(*args)


def _gmlp_kernel(u_ref, v_ref, w_ref, b_ref, o_ref):
    for n in range(u_ref.shape[0] // GMLP_CHUNK):
        rows = slice(n * GMLP_CHUNK, (n + 1) * GMLP_CHUNK)
        for g in range(GMLP_GROUPS):
            cols = slice(g * GMLP_DIM, (g + 1) * GMLP_DIM)
            v = jax.nn.gelu(v_ref[rows, cols])
            vc = v - jnp.mean(v, axis=-1, keepdims=True)
            var = jnp.mean(vc * vc, axis=-1, keepdims=True)
            vn = (vc * lax.rsqrt(var + NORM_EPS)).astype(BF16)
            mixed = _dot(w_ref[g], vn) + b_ref[g]
            o_ref[rows, cols] = (jax.nn.gelu(u_ref[rows, cols]) * mixed).astype(o_ref.dtype)


def _gmlp(z, u_col, v_col, w_sp, b_sp, *, tm=256):
    t = z.shape[0]
    return pl.pallas_call(
        _gmlp_kernel,
        out_shape=jax.ShapeDtypeStruct((t, GMLP_WIDTH), BF16),
        grid=(t // tm,),
        in_specs=[
            pl.BlockSpec((tm, GMLP_WIDTH), lambda i: (i, u_col)),
            pl.BlockSpec((tm, GMLP_WIDTH), lambda i: (i, v_col)),
            pl.BlockSpec((GMLP_GROUPS, GMLP_CHUNK, GMLP_CHUNK), lambda i: (0, 0, 0)),
            pl.BlockSpec((GMLP_GROUPS, GMLP_CHUNK, 1), lambda i: (0, 0, 0)),
        ],
        out_specs=pl.BlockSpec((tm, GMLP_WIDTH), lambda i: (i, 0)),
        compiler_params=_params(("parallel",), 32),
        name="gmlp",
    )(z, z, w_sp, b_sp)


def _swa_kernel(sink_ref, q_ref, kl_ref, vl_ref, kc_ref, vc_ref, o_ref, *, tq, n_lat):
    start = pl.program_id(1) * tq
    span0 = pl.multiple_of(jnp.clip(start - SWA_WINDOW, 0, n_lat - SWA_SPAN), SWA_WINDOW)
    k_span = kl_ref[pl.ds(span0, SWA_SPAN), :]
    k_ctx = kc_ref[...]
    vt_span = vl_ref[pl.ds(span0, SWA_SPAN), :].astype(F32).T.astype(BF16)
    vt_ctx = vc_ref[...].astype(F32).T.astype(BF16)
    ones_span = jnp.ones((ONES_ROWS, vt_span.shape[1]), BF16)
    ones_ctx = jnp.ones((ONES_ROWS, vt_ctx.shape[1]), BF16)

    k_pos = span0 + lax.broadcasted_iota(jnp.int32, (SWA_SPAN, tq), 0)
    q_pos = start + lax.broadcasted_iota(jnp.int32, (SWA_SPAN, tq), 1)
    bias = jnp.where(jnp.abs(k_pos - q_pos) <= SWA_WINDOW, 0.0, NEG_INF).astype(F32)
    bias = jnp.concatenate([bias] * SWA_GROUP, axis=1)

    lane = lax.broadcasted_iota(jnp.int32, (tq, LANES), 1)
    first = (lane % SWA_HEAD_DIM) < (SWA_HEAD_DIM // 2)
    keep = [jnp.where(first, 1.0, 0.0).astype(BF16), jnp.where(first, 0.0, 1.0).astype(BF16)]

    def scores(g):
        heads = range(g * SWA_GROUP, (g + 1) * SWA_GROUP)
        k_cols = slice(g * LANES, (g + 1) * LANES)
        q = jnp.concatenate(
            [q_ref[:, (h // 2) * LANES:(h // 2 + 1) * LANES] * keep[h % 2] for h in heads], axis=0)
        return _dot_nt(k_span[:, k_cols], q) + bias, _dot_nt(k_ctx[:, k_cols], q)

    s_next = scores(0)
    for g in range(SWA_KV_HEADS):
        heads = range(g * SWA_GROUP, (g + 1) * SWA_GROUP)
        kv_cols = slice(g * SWA_HEAD_DIM, (g + 1) * SWA_HEAD_DIM)
        s_l, s_c = s_next
        if g + 1 < SWA_KV_HEADS:
            s_next = scores(g + 1)
        sink = jnp.concatenate([jnp.full((1, tq), sink_ref[h] * LOG2E, F32) for h in heads], axis=1)
        m = jnp.maximum(jnp.maximum(jnp.max(s_l, axis=0, keepdims=True),
                                    jnp.max(s_c, axis=0, keepdims=True)), sink)
        p_l = jnp.exp2(s_l - m).astype(BF16)
        p_c = jnp.exp2(s_c - m).astype(BF16)
        vt_l = jnp.concatenate([vt_span[kv_cols, :], ones_span], axis=0)
        vt_c = jnp.concatenate([vt_ctx[kv_cols, :], ones_ctx], axis=0)
        o = _dot(vt_l, p_l) + _dot(vt_c, p_c)
        dim = SWA_HEAD_DIM
        o = o[0:dim] / (o[dim:dim + 1] + jnp.exp2(sink - m))
        for pair in range(SWA_GROUP // 2):
            cols = [slice((2 * pair + r) * tq, (2 * pair + r + 1) * tq) for r in range(2)]
            both = jnp.concatenate([o[:, cols[0]], o[:, cols[1]]], axis=0)
            group = g * (SWA_GROUP // 2) + pair
            o_ref[:, group * LANES:(group + 1) * LANES] = both.T.astype(o_ref.dtype)


def _swa_attention(z_l, z_c, sinks, *, batch, n_lat, n_ctx, tq=SWA_BLOCK):
    nq_blocks = n_lat // tq
    k_cols = 2 * SWA_KV_COLS
    return pl.pallas_call(
        functools.partial(_swa_kernel, tq=tq, n_lat=n_lat),
        out_shape=jax.ShapeDtypeStruct((batch * n_lat, SWA_Q_COLS), BF16),
        grid=(batch, nq_blocks),
        in_specs=[
            pl.BlockSpec(memory_space=pltpu.SMEM),
            pl.BlockSpec((tq, SWA_Q_COLS), lambda b, i: (b * nq_blocks + i, 0)),
            pl.BlockSpec((n_lat, k_cols), lambda b, i: (b, SWA_Q_COLS // k_cols)),
            pl.BlockSpec((n_lat, SWA_KV_COLS), lambda b, i: (b, (SWA_Q_COLS + k_cols) // SWA_KV_COLS)),
            pl.BlockSpec((n_ctx, k_cols), lambda b, i: (b, 0)),
            pl.BlockSpec((n_ctx, SWA_KV_COLS), lambda b, i: (b, k_cols // SWA_KV_COLS)),
        ],
        out_specs=pl.BlockSpec((tq, SWA_Q_COLS), lambda b, i: (b * nq_blocks + i, 0)),
        compiler_params=_params(("parallel", "arbitrary"), 48),
        name="swa_attention",
    )(sinks, z_l, z_l, z_l, z_c, z_c)


def _outproj_kernel(a1_ref, a2_ref, w1_ref, w2_ref, x_ref, gate_ref, o_ref):
    y = _dot(a1_ref[...], w1_ref[...]) + _dot(a2_ref[...], w2_ref[...])
    o_ref[...] = x_ref[...] + gate_ref[...] * y


def _outproj(a1, a1_col, a2, a2_col, w, x, mods, row_of, k_gate, *, tm=512):
    t, d = x.shape
    half = w.shape[0] // 2
    return pl.pallas_call(
        _outproj_kernel,
        out_shape=jax.ShapeDtypeStruct((t, d), F32),
        grid=(t // tm,),
        in_specs=[
            pl.BlockSpec((tm, half), lambda i: (i, a1_col)),
            pl.BlockSpec((tm, half), lambda i: (i, a2_col)),
            pl.BlockSpec((half, d), lambda i: (0, 0)),
            pl.BlockSpec((half, d), lambda i: (1, 0)),
            pl.BlockSpec((tm, d), lambda i: (i, 0)),
            pl.BlockSpec((None, 1, d), lambda i: (row_of(i), 0, k_gate)),
        ],
        out_specs=pl.BlockSpec((tm, d), lambda i: (i, 0)),
        compiler_params=_params(("parallel",), 48),
        name="outproj",
    )(a1, a2, w, w, x, mods)


def _mlp_kernel(*refs, final):
    if final:
        x_ref, g_ref, sh_ref, sc_ref, gate_ref, w1_ref, w2_ref, fg_ref, o_ref, h_ref = refs
    else:
        x_ref, g_ref, sh_ref, sc_ref, gate_ref, w1_ref, w2_ref, o_ref, h_ref = refs
    k = pl.program_id(1)

    @pl.when(k == 0)
    def _():
        def rows(sl):
            y = _rms(x_ref[sl, :], g_ref[...]) * (1.0 + sc_ref[...]) + sh_ref[...]
            h_ref[sl, :] = y.astype(BF16)
            o_ref[sl, :] = jnp.zeros((sl.size, o_ref.shape[1]), F32)

        _for_row_chunks(x_ref.shape[0], rows)

    a = jnp.square(jnp.maximum(_dot(h_ref[...], w1_ref[...]), 0.0)).astype(BF16)
    o_ref[...] += _dot(a, w2_ref[...])

    @pl.when(k == pl.num_programs(1) - 1)
    def _():
        def rows(sl):
            out = x_ref[sl, :] + gate_ref[...] * o_ref[sl, :]
            if final:
                out = _rms(out, fg_ref[...])
            o_ref[sl, :] = out

        _for_row_chunks(x_ref.shape[0], rows)


def _mlp(x, g, mods, row_of, w1, w2, *, final_g=None, tm=512, tf=1024):
    t, d = x.shape
    f = w1.shape[1]
    in_specs = [
        pl.BlockSpec((tm, d), lambda i, k: (i, 0)),
        pl.BlockSpec((1, d), lambda i, k: (0, 0)),
        pl.BlockSpec((None, 1, d), lambda i, k: (row_of(i), 0, 3)),
        pl.BlockSpec((None, 1, d), lambda i, k: (row_of(i), 0, 4)),
        pl.BlockSpec((None, 1, d), lambda i, k: (row_of(i), 0, 5)),
        pl.BlockSpec((d, tf), lambda i, k: (0, k)),
        pl.BlockSpec((tf, d), lambda i, k: (k, 0)),
    ]
    args = [x, g.reshape(1, d), mods, mods, mods, w1, w2]
    if final_g is not None:
        in_specs.append(pl.BlockSpec((1, d), lambda i, k: (0, 0)))
        args.append(final_g.reshape(1, d))
    return pl.pallas_call(
        functools.partial(_mlp_kernel, final=final_g is not None),
        out_shape=jax.ShapeDtypeStruct((t, d), F32),
        grid=(t // tm, f // tf),
        in_specs=in_specs,
        out_specs=pl.BlockSpec((tm, d), lambda i, k: (i, 0)),
        scratch_shapes=[pltpu.VMEM((tm, d), BF16)],
        compiler_params=_params(("parallel", "arbitrary"), 48),
        name="mlp",
    )(*args)


def _axial_angles(n_tokens, rot_dim):
    t = jnp.arange(n_tokens)
    row = (t // GRID_W).astype(F32)
    col = (t % GRID_W).astype(F32)
    n_freq = rot_dim // 4
    inv_freq = ROPE_BASE ** (-jnp.arange(n_freq, dtype=F32) / n_freq)
    ang = jnp.concatenate([row[:, None] * inv_freq, col[:, None] * inv_freq], axis=-1)
    return jnp.cos(ang), jnp.sin(ang)


def _swap_halves(w):
    half = w.shape[-1] // 2
    return jnp.concatenate([w[..., half:], w[..., :half]], axis=-1)


def kernel(x, c, ctx, c_ctx, norm1_g, w_mod, b_mod, norm2_g, w_ff1, w_ff2, even_w_in, mla_q_norm_g,
           mla_w_uq, mla_kv_norm_g, mla_w_ukv, gmlp_w_sp, gmlp_b_sp, even_w_out, odd_w_in, swa_sinks,
           odd_w_out, final_norm_g):
    batch, n_lat, d = x.shape
    n_ctx = ctx.shape[1]
    assert batch + 1 <= MOD_ROWS
    xl = x.reshape(batch * n_lat, d)
    xc = ctx.reshape(batch * n_ctx, d)

    tm = 1024
    lat_blocks = n_lat // tm

    def lat_row(block_rows):
        per_sample = n_lat // block_rows
        return lambda i: i // per_sample

    ctx_row = lambda i: batch

    cvec = jnp.concatenate([c, c_ctx[None, :], jnp.zeros((MOD_ROWS - batch - 1, d), F32)], axis=0)
    mods = _modulation(cvec, w_mod, b_mod)
    mods = mods.reshape(mods.shape[0], MOD_ROWS, 1, N_MOD * d)

    cos_m, sin_m = _axial_angles(n_lat, MLA_ROPE)
    zeros64 = jnp.zeros((n_lat, LANES // 2), F32)
    fold_cos = jnp.concatenate([cos_m, cos_m, zeros64], axis=-1)
    fold_sin = jnp.concatenate([-sin_m, sin_m, zeros64], axis=-1)
    keep = jnp.concatenate([jnp.ones((tm, LANES // 2), F32), jnp.zeros((tm, LANES // 2), F32)], axis=-1)
    drop = jnp.zeros((tm, LANES), F32)

    w_in = even_w_in[0]
    cq_w, ckv_w = w_in[:, :MLA_LORA], w_in[:, MLA_LORA:2 * MLA_LORA]
    kr_w = w_in[:, 2 * MLA_LORA:2 * MLA_LORA + MLA_ROPE]
    gm_w = w_in[:, 2 * MLA_LORA + MLA_ROPE:]
    w_in0 = jnp.concatenate([cq_w, ckv_w, gm_w, kr_w, _swap_halves(kr_w)], axis=-1).astype(BF16)
    n_in0 = w_in0.shape[1]
    tn0 = 640
    kr_group = (n_in0 - LANES) // LANES
    kr_rope = (kr_group * LANES // tn0, ((kr_group * LANES % tn0) // LANES,))

    wq = mla_w_uq[0].reshape(MLA_LORA, MLA_HEADS, MLA_NOPE + MLA_ROPE)
    wq_rope = wq[..., MLA_NOPE:]
    w_uq = jnp.concatenate([wq[..., :MLA_NOPE], wq_rope, _swap_halves(wq_rope)], axis=-1)
    w_uq = (w_uq * (MLA_SCALE * LOG2E)).reshape(MLA_LORA, MLA_HEADS * MLA_QK_PAD).astype(BF16)
    q_rope = (None, tuple(range(1, 2 * MLA_HEADS, 2)))
    w_ukv = mla_w_ukv[0].astype(BF16)
    w_sp = gmlp_w_sp[0].astype(BF16)
    b_sp = gmlp_b_sp[0][:, :, None]
    w_out0 = even_w_out[0].astype(BF16)
    w1_0, w2_0 = w_ff1[0].astype(BF16), w_ff2[0].astype(BF16)

    def layer0_tokens(xs, row_of, tabs, pos_blocks):
        m0 = mods[0]
        z = _proj(xs, 0, d, norm1_g[0], w_in0, tm=tm, tn=tn0, out_dtype=F32,
                  mod=(m0, row_of, 0, 1), rope=kr_rope, tabs=tabs, pos_blocks=pos_blocks,
                  vmem_mib=52, name="even_in_proj")
        q = _proj(z, 0, MLA_LORA, mla_q_norm_g[0], w_uq, tm=tm, tn=w_uq.shape[1], out_dtype=BF16,
                  rope=q_rope, tabs=tabs, pos_blocks=pos_blocks, name="mla_q_proj")
        kv = _proj(z, 1, MLA_LORA, mla_kv_norm_g[0], w_ukv, tm=tm, tn=w_ukv.shape[1], out_dtype=BF16,
                   name="mla_kv_proj")
        gm = _gmlp(z, 1, 2, w_sp, b_sp)
        return z, q, kv, gm

    z_l, q_l, kv_l, gm_l = layer0_tokens(xl, lat_row(tm), (fold_cos, fold_sin), lat_blocks)
    z_c, q_c, kv_c, gm_c = layer0_tokens(xc, ctx_row, (keep, drop), 1)
    att_l = _mla_attention(q_l, kv_c, z_c, kr_group, batch=batch, n_q=n_lat, n_ctx=n_ctx,
                           kv_l=kv_l, z_l=z_l, tq=n_lat)
    att_c = _mla_attention(q_c, kv_c, z_c, kr_group, batch=batch, n_q=n_ctx, n_ctx=n_ctx)

    tm_o = 512
    xl = _outproj(att_l, 0, gm_l, 0, w_out0, xl, mods[0], lat_row(tm_o), 2, tm=tm_o)
    xc = _outproj(att_c, 0, gm_c, 0, w_out0, xc, mods[0], ctx_row, 2, tm=tm_o)
    tm_f = 512
    xl = _mlp(xl, norm2_g[0], mods[0], lat_row(tm_f), w1_0, w2_0, tm=tm_f)
    xc = _mlp(xc, norm2_g[0], mods[0], ctx_row, w1_0, w2_0, tm=tm_f)

    cos_s, sin_s = _axial_angles(n_lat, SWA_HEAD_DIM)
    pair_cos = jnp.concatenate([cos_s] * 4, axis=-1)
    pair_sin = jnp.concatenate([-sin_s, -sin_s, sin_s, sin_s], axis=-1)
    half = SWA_HEAD_DIM // 2
    w_in = odd_w_in[0]
    wq = (w_in[:, :SWA_Q_COLS] * (SWA_SCALE * LOG2E)).reshape(d, SWA_HEADS // 2, 2, 2, half)
    wq = wq.transpose(0, 1, 3, 2, 4).reshape(d, SWA_Q_COLS)
    wk = w_in[:, SWA_Q_COLS:SWA_Q_COLS + SWA_KV_COLS].reshape(d, SWA_KV_HEADS, 2, 1, half)
    wk = jnp.broadcast_to(wk, (d, SWA_KV_HEADS, 2, 2, half)).reshape(d, 2 * SWA_KV_COLS)
    wv = w_in[:, SWA_Q_COLS + SWA_KV_COLS:]
    w_in1 = jnp.concatenate([wq, wk, wv], axis=-1).astype(BF16)
    w_kv1 = jnp.concatenate([wk, wv], axis=-1).astype(BF16)
    rot_groups = (SWA_Q_COLS + 2 * SWA_KV_COLS) // LANES
    m1 = mods[1]
    tm_i = 512
    z1_l = _proj(xl, 0, d, norm1_g[1], w_in1, tm=tm_i, tn=w_in1.shape[1], out_dtype=BF16,
                 mod=(m1, lat_row(tm_i), 0, 1), rope=(None, tuple(range(rot_groups))),
                 tabs=(pair_cos, pair_sin), pos_blocks=n_lat // tm_i, vmem_mib=56, name="odd_in_proj")
    z1_c = _proj(xc, 0, d, norm1_g[1], w_kv1, tm=tm, tn=w_kv1.shape[1], out_dtype=BF16,
                 mod=(m1, ctx_row, 0, 1), name="odd_ctx_kv_proj")
    att = _swa_attention(z1_l, z1_c, swa_sinks[0], batch=batch, n_lat=n_lat, n_ctx=n_ctx)
    xl = _outproj(att, 0, att, 1, odd_w_out[0].astype(BF16), xl, m1, lat_row(tm_o), 2, tm=tm_o)
    out = _mlp(xl, norm2_g[1], m1, lat_row(tm_f), w_ff1[1].astype(BF16), w_ff2[1].astype(BF16),
               final_g=final_norm_g, tm=tm_f)
    return out.reshape(batch, n_lat, d)
```

```python
import functools
import math

import jax
import jax.numpy as jnp
from jax import lax
from jax.experimental import pallas as pl
from jax.experimental.pallas import tpu as pltpu

F32 = jnp.float32
BF16 = jnp.bfloat16

LANES = 128
MIB = 1 << 20

GRID_W = 64
N_MOD = 6
NORM_EPS = 1e-6
ROPE_BASE = 10000.0
NEG_INF = -1e30

MLA_HEADS = 8
MLA_LORA = 512
MLA_NOPE = 128
MLA_ROPE = 64
MLA_V = 128
MLA_QK_PAD = 2 * LANES
MLA_SCALE = 1.0 / math.sqrt(MLA_NOPE + MLA_ROPE)
MLA_Q_SUB = 512

GMLP_GROUPS = 8
GMLP_DIM = 128
GMLP_CHUNK = 128
GMLP_WIDTH = GMLP_GROUPS * GMLP_DIM

SWA_HEADS = 32
SWA_KV_HEADS = 4
SWA_GROUP = SWA_HEADS // SWA_KV_HEADS
SWA_HEAD_DIM = 64
SWA_WINDOW = 128
SWA_BLOCK = 128
SWA_SPAN = SWA_BLOCK + 2 * SWA_WINDOW
SWA_SCALE = 1.0 / math.sqrt(SWA_HEAD_DIM)
SWA_Q_COLS = SWA_HEADS * SWA_HEAD_DIM
SWA_KV_COLS = SWA_KV_HEADS * SWA_HEAD_DIM

MOD_ROWS = 16
ROW_CHUNK = 256
ONES_ROWS = 16
LOG2E = math.log2(math.e)


def _params(semantics, vmem_mib):
    return pltpu.CompilerParams(dimension_semantics=semantics, vmem_limit_bytes=vmem_mib * MIB)


def _rms(x, g):
    return x * lax.rsqrt(jnp.mean(x * x, axis=-1, keepdims=True) + NORM_EPS) * g


def _for_row_chunks(n_rows, fn, rows=ROW_CHUNK):
    rows = min(rows, n_rows)

    def body(r, carry):
        fn(pl.ds(pl.multiple_of(r * rows, rows), rows))
        return carry

    lax.fori_loop(0, n_rows // rows, body, 0)


def _dot(a, b):
    return jnp.dot(a, b, preferred_element_type=F32)


def _dot_nt(a, b):
    return lax.dot_general(a, b, (((1,), (1,)), ((), ())), preferred_element_type=F32)


def _mod_kernel(c_ref, w_ref, b_ref, o_ref):
    c = c_ref[...]
    s = (c * jax.nn.sigmoid(c)).astype(BF16)
    o_ref[...] = _dot(s, w_ref[...].astype(BF16)) + b_ref[...]


def _modulation(cvec, w_mod, b_mod, *, tn=1024):
    depth, d, n = w_mod.shape
    return pl.pallas_call(
        _mod_kernel,
        out_shape=jax.ShapeDtypeStruct((depth, MOD_ROWS, n), F32),
        grid=(depth, n // tn),
        in_specs=[
            pl.BlockSpec((MOD_ROWS, d), lambda l, j: (0, 0)),
            pl.BlockSpec((None, d, tn), lambda l, j: (l, 0, j)),
            pl.BlockSpec((None, 1, tn), lambda l, j: (l, 0, j)),
        ],
        out_specs=pl.BlockSpec((None, MOD_ROWS, tn), lambda l, j: (l, 0, j)),
        compiler_params=_params(("parallel", "parallel"), 40),
        name="modulation",
    )(cvec, w_mod, b_mod.reshape(depth, 1, n))


def _proj_kernel(*refs, modulated, rope):
    it = iter(refs)
    x_ref, g_ref = next(it), next(it)
    sh_ref = sc_ref = None
    if modulated:
        sh_ref, sc_ref = next(it), next(it)
    w_ref = next(it)
    tabs = [next(it) for _ in range(2 if rope else 0)]
    o_ref, h_ref = next(it), next(it)
    j = pl.program_id(1)

    @pl.when(j == 0)
    def _():
        def rows(sl):
            y = _rms(x_ref[sl, :], g_ref[...])
            if modulated:
                y = y * (1.0 + sc_ref[...]) + sh_ref[...]
            h_ref[sl, :] = y.astype(BF16)

        _for_row_chunks(x_ref.shape[0], rows)

    z = _dot(h_ref[...], w_ref[...])
    groups_per_tile = z.shape[1] // LANES

    if rope is None:
        o_ref[...] = z.astype(o_ref.dtype)
        return

    tile, groups = rope

    def rotated(t):
        return t * tabs[0][...] + pltpu.roll(t, LANES // 2, 1) * tabs[1][...]

    if tile is None:
        for gi in range(groups_per_tile):
            sl = slice(gi * LANES, (gi + 1) * LANES)
            t = z[:, sl]
            o_ref[:, sl] = (rotated(t) if gi in groups else t).astype(o_ref.dtype)
    else:
        o_ref[...] = z.astype(o_ref.dtype)

        @pl.when(j == tile)
        def _():
            for gi in groups:
                sl = slice(gi * LANES, (gi + 1) * LANES)
                o_ref[:, sl] = rotated(z[:, sl]).astype(o_ref.dtype)


def _proj(x, xcol, k, g, w, *, tm, tn, out_dtype, mod=None, rope=None, tabs=(), pos_blocks=1,
          vmem_mib=48, name="proj"):
    t = x.shape[0]
    n = w.shape[1]
    assert t % tm == 0 and n % tn == 0 and w.shape[0] == k
    in_specs = [
        pl.BlockSpec((tm, k), lambda i, j: (i, xcol)),
        pl.BlockSpec((1, k), lambda i, j: (0, 0)),
    ]
    args = [x, g.reshape(1, k)]
    if mod is not None:
        mods, row_of, k_shift, k_scale = mod
        in_specs += [
            pl.BlockSpec((None, 1, k), lambda i, j: (row_of(i), 0, k_shift)),
            pl.BlockSpec((None, 1, k), lambda i, j: (row_of(i), 0, k_scale)),
        ]
        args += [mods, mods]
    w_mode = {"pipeline_mode": pl.Buffered(1)} if n == tn else {}
    in_specs.append(pl.BlockSpec((k, tn), lambda i, j: (0, j), **w_mode))
    args.append(w)
    for tab in tabs:
        in_specs.append(pl.BlockSpec((tm, LANES), lambda i, j: (i % pos_blocks, 0)))
        args.append(tab)
    return pl.pallas_call(
        functools.partial(_proj_kernel, modulated=mod is not None, rope=rope),
        out_shape=jax.ShapeDtypeStruct((t, n), out_dtype),
        grid=(t // tm, n // tn),
        in_specs=in_specs,
        out_specs=pl.BlockSpec((tm, tn), lambda i, j: (i, j)),
        scratch_shapes=[pltpu.VMEM((tm, k), BF16)],
        compiler_params=_params(("parallel", "arbitrary"), vmem_mib),
        name=name,
    )(*args)


def _mla_kernel(*refs, n_lat, n_ctx):
    if n_lat:
        q_ref, knl_ref, vl_ref, krl_ref, knc_ref, vc_ref, krc_ref, o_ref, k_scr, vt_scr = refs
    else:
        q_ref, knc_ref, vc_ref, krc_ref, o_ref, k_scr, vt_scr = refs

    @pl.when(pl.program_id(2) == 0)
    def _():
        if n_lat:
            k_scr[0:n_lat, 0:LANES] = knl_ref[...]
            k_scr[0:n_lat, LANES:] = krl_ref[...].astype(BF16)
            vt_scr[0:MLA_V, 0:n_lat] = vl_ref[...].astype(F32).T.astype(BF16)
        k_scr[n_lat:, 0:LANES] = knc_ref[...]
        k_scr[n_lat:, LANES:] = krc_ref[...].astype(BF16)
        vt_scr[0:MLA_V, n_lat:] = vc_ref[...].astype(F32).T.astype(BF16)
        vt_scr[MLA_V:, :] = jnp.ones((ONES_ROWS, n_lat + n_ctx), BF16)

    n_sub = min(MLA_Q_SUB, q_ref.shape[0])
    subs = [slice(r, r + n_sub) for r in range(0, q_ref.shape[0], n_sub)]
    s_next = _dot_nt(k_scr[...], q_ref[subs[0], :])
    for i, rows in enumerate(subs):
        s = s_next
        if i + 1 < len(subs):
            s_next = _dot_nt(k_scr[...], q_ref[subs[i + 1], :])
        p = jnp.exp2(s - jnp.max(s, axis=0, keepdims=True))
        o = _dot(vt_scr[...], p.astype(BF16))
        o_ref[rows, :] = (o[0:MLA_V] / o[MLA_V:MLA_V + 1]).T.astype(o_ref.dtype)


def _mla_attention(q, kv_c, z_c, kr_col, *, batch, n_q, n_ctx, kv_l=None, z_l=None, tq=256):
    n_lat = 0 if kv_l is None else kv_l.shape[0] // batch
    nq_blocks = n_q // tq
    in_specs = [pl.BlockSpec((tq, MLA_QK_PAD), lambda b, h, i: (b * nq_blocks + i, h))]
    args = [q]
    if n_lat:
        in_specs += [
            pl.BlockSpec((n_lat, LANES), lambda b, h, i: (b, 2 * h)),
            pl.BlockSpec((n_lat, LANES), lambda b, h, i: (b, 2 * h + 1)),
            pl.BlockSpec((n_lat, LANES), lambda b, h, i: (b, kr_col)),
        ]
        args += [kv_l, kv_l, z_l]
    in_specs += [
        pl.BlockSpec((n_ctx, LANES), lambda b, h, i: (b, 2 * h)),
        pl.BlockSpec((n_ctx, LANES), lambda b, h, i: (b, 2 * h + 1)),
        pl.BlockSpec((n_ctx, LANES), lambda b, h, i: (b, kr_col)),
    ]
    args += [kv_c, kv_c, z_c]
    n_keys = n_lat + n_ctx
    return pl.pallas_call(
        functools.partial(_mla_kernel, n_lat=n_lat, n_ctx=n_ctx),
        out_shape=jax.ShapeDtypeStruct((batch * n_q, MLA_HEADS * MLA_V), BF16),
        grid=(batch, MLA_HEADS, nq_blocks),
        in_specs=in_specs,
        out_specs=pl.BlockSpec((tq, MLA_V), lambda b, h, i: (b * nq_blocks + i, h)),
        scratch_shapes=[pltpu.VMEM((n_keys, MLA_QK_PAD), BF16),
                        pltpu.VMEM((MLA_V + ONES_ROWS, n_keys), BF16)],
        compiler_params=_params(("parallel", "parallel", "arbitrary"), 48),
        name="mla_attention",
    )(*args)


def _gelu(x):
    c = math.sqrt(2.0 / math.pi)
    return x * (0.5 + 0.5 * jnp.tanh(x * (c + (c * 0.044715) * (x * x))))


def _gmlp_kernel(u_ref, v_ref, w_ref, b_ref, o_ref):
    for n in range(u_ref.shape[0] // GMLP_CHUNK):
        rows = slice(n * GMLP_CHUNK, (n + 1) * GMLP_CHUNK)
        for g in range(GMLP_GROUPS):
            cols = slice(g * GMLP_DIM, (g + 1) * GMLP_DIM)
            v = _gelu(v_ref[rows, cols])
            vc = v - jnp.mean(v, axis=-1, keepdims=True)
            var = jnp.mean(vc * vc, axis=-1, keepdims=True)
            vn = (vc * lax.rsqrt(var + NORM_EPS)).astype(BF16)
            mixed = _dot(w_ref[g], vn) + b_ref[g]
            o_ref[rows, cols] = (_gelu(u_ref[rows, cols]) * mixed).astype(o_ref.dtype)


def _gmlp(z, u_col, v_col, w_sp, b_sp, *, tm=256):
    t = z.shape[0]
    return pl.pallas_call(
        _gmlp_kernel,
        out_shape=jax.ShapeDtypeStruct((t, GMLP_WIDTH), BF16),
        grid=(t // tm,),
        in_specs=[
            pl.BlockSpec((tm, GMLP_WIDTH), lambda i: (i, u_col)),
            pl.BlockSpec((tm, GMLP_WIDTH), lambda i: (i, v_col)),
            pl.BlockSpec((GMLP_GROUPS, GMLP_CHUNK, GMLP_CHUNK), lambda i: (0, 0, 0)),
            pl.BlockSpec((GMLP_GROUPS, GMLP_CHUNK, 1), lambda i: (0, 0, 0)),
        ],
        out_specs=pl.BlockSpec((tm, GMLP_WIDTH), lambda i: (i, 0)),
        compiler_params=_params(("parallel",), 32),
        name="gmlp",
    )(z, z, w_sp, b_sp)


def _swa_kernel(sink_ref, q_ref, kl_ref, vl_ref, kc_ref, vc_ref, o_ref, *, tq, n_lat):
    start = pl.program_id(1) * tq
    span0 = pl.multiple_of(jnp.clip(start - SWA_WINDOW, 0, n_lat - SWA_SPAN), SWA_WINDOW)
    k_span = kl_ref[pl.ds(span0, SWA_SPAN), :]
    k_ctx = kc_ref[...]
    vt_span = vl_ref[pl.ds(span0, SWA_SPAN), :].astype(F32).T.astype(BF16)
    vt_ctx = vc_ref[...].astype(F32).T.astype(BF16)
    ones_span = jnp.ones((ONES_ROWS, vt_span.shape[1]), BF16)
    ones_ctx = jnp.ones((ONES_ROWS, vt_ctx.shape[1]), BF16)

    k_pos = span0 + lax.broadcasted_iota(jnp.int32, (SWA_SPAN, tq), 0)
    q_pos = start + lax.broadcasted_iota(jnp.int32, (SWA_SPAN, tq), 1)
    bias = jnp.where(jnp.abs(k_pos - q_pos) <= SWA_WINDOW, 0.0, NEG_INF).astype(F32)
    bias = jnp.concatenate([bias] * SWA_GROUP, axis=1)

    lane = lax.broadcasted_iota(jnp.int32, (tq, LANES), 1)
    first = (lane % SWA_HEAD_DIM) < (SWA_HEAD_DIM // 2)
    keep = [jnp.where(first, 1.0, 0.0).astype(BF16), jnp.where(first, 0.0, 1.0).astype(BF16)]

    def scores(g):
        heads = range(g * SWA_GROUP, (g + 1) * SWA_GROUP)
        k_cols = slice(g * LANES, (g + 1) * LANES)
        q = jnp.concatenate(
            [q_ref[:, (h // 2) * LANES:(h // 2 + 1) * LANES] * keep[h % 2] for h in heads], axis=0)
        return _dot_nt(k_span[:, k_cols], q) + bias, _dot_nt(k_ctx[:, k_cols], q)

    s_next = scores(0)
    for g in range(SWA_KV_HEADS):
        heads = range(g * SWA_GROUP, (g + 1) * SWA_GROUP)
        kv_cols = slice(g * SWA_HEAD_DIM, (g + 1) * SWA_HEAD_DIM)
        s_l, s_c = s_next
        if g + 1 < SWA_KV_HEADS:
            s_next = scores(g + 1)
        sink = jnp.concatenate([jnp.full((1, tq), sink_ref[h] * LOG2E, F32) for h in heads], axis=1)
        m = jnp.maximum(jnp.maximum(jnp.max(s_l, axis=0, keepdims=True),
                                    jnp.max(s_c, axis=0, keepdims=True)), sink)
        p_l = jnp.exp2(s_l - m).astype(BF16)
        p_c = jnp.exp2(s_c - m).astype(BF16)
        vt_l = jnp.concatenate([vt_span[kv_cols, :], ones_span], axis=0)
        vt_c = jnp.concatenate([vt_ctx[kv_cols, :], ones_ctx], axis=0)
        o = _dot(vt_l, p_l) + _dot(vt_c, p_c)
        dim = SWA_HEAD_DIM
        o = o[0:dim] / (o[dim:dim + 1] + jnp.exp2(sink - m))
        for pair in range(SWA_GROUP // 2):
            cols = [slice((2 * pair + r) * tq, (2 * pair + r + 1) * tq) for r in range(2)]
            both = jnp.concatenate([o[:, cols[0]], o[:, cols[1]]], axis=0)
            group = g * (SWA_GROUP // 2) + pair
            o_ref[:, group * LANES:(group + 1) * LANES] = both.T.astype(o_ref.dtype)


def _swa_attention(z_l, z_c, sinks, *, batch, n_lat, n_ctx, tq=SWA_BLOCK):
    nq_blocks = n_lat // tq
    k_cols = 2 * SWA_KV_COLS
    return pl.pallas_call(
        functools.partial(_swa_kernel, tq=tq, n_lat=n_lat),
        out_shape=jax.ShapeDtypeStruct((batch * n_lat, SWA_Q_COLS), BF16),
        grid=(batch, nq_blocks),
        in_specs=[
            pl.BlockSpec(memory_space=pltpu.SMEM),
            pl.BlockSpec((tq, SWA_Q_COLS), lambda b, i: (b * nq_blocks + i, 0)),
            pl.BlockSpec((n_lat, k_cols), lambda b, i: (b, SWA_Q_COLS // k_cols)),
            pl.BlockSpec((n_lat, SWA_KV_COLS), lambda b, i: (b, (SWA_Q_COLS + k_cols) // SWA_KV_COLS)),
            pl.BlockSpec((n_ctx, k_cols), lambda b, i: (b, 0)),
            pl.BlockSpec((n_ctx, SWA_KV_COLS), lambda b, i: (b, k_cols // SWA_KV_COLS)),
        ],
        out_specs=pl.BlockSpec((tq, SWA_Q_COLS), lambda b, i: (b * nq_blocks + i, 0)),
        compiler_params=_params(("parallel", "arbitrary"), 48),
        name="swa_attention",
    )(sinks, z_l, z_l, z_l, z_c, z_c)


def _outproj_kernel(a1_ref, a2_ref, w1_ref, w2_ref, x_ref, gate_ref, o_ref):
    y = _dot(a1_ref[...], w1_ref[...]) + _dot(a2_ref[...], w2_ref[...])
    o_ref[...] = x_ref[...] + gate_ref[...] * y


def _outproj(a1, a1_col, a2, a2_col, w, x, mods, row_of, k_gate, *, tm=512):
    t, d = x.shape
    half = w.shape[0] // 2
    return pl.pallas_call(
        _outproj_kernel,
        out_shape=jax.ShapeDtypeStruct((t, d), F32),
        grid=(t // tm,),
        in_specs=[
            pl.BlockSpec((tm, half), lambda i: (i, a1_col)),
            pl.BlockSpec((tm, half), lambda i: (i, a2_col)),
            pl.BlockSpec((half, d), lambda i: (0, 0)),
            pl.BlockSpec((half, d), lambda i: (1, 0)),
            pl.BlockSpec((tm, d), lambda i: (i, 0)),
            pl.BlockSpec((None, 1, d), lambda i: (row_of(i), 0, k_gate)),
        ],
        out_specs=pl.BlockSpec((tm, d), lambda i: (i, 0)),
        compiler_params=_params(("parallel",), 48),
        name="outproj",
    )(a1, a2, w, w, x, mods)


def _mlp_kernel(*refs, final):
    if final:
        x_ref, g_ref, sh_ref, sc_ref, gate_ref, w1_ref, w2_ref, fg_ref, o_ref, h_ref = refs
    else:
        x_ref, g_ref, sh_ref, sc_ref, gate_ref, w1_ref, w2_ref, o_ref, h_ref = refs
    k = pl.program_id(1)

    @pl.when(k == 0)
    def _():
        def rows(sl):
            y = _rms(x_ref[sl, :], g_ref[...]) * (1.0 + sc_ref[...]) + sh_ref[...]
            h_ref[sl, :] = y.astype(BF16)
            o_ref[sl, :] = jnp.zeros((sl.size, o_ref.shape[1]), F32)

        _for_row_chunks(x_ref.shape[0], rows)

    a = jnp.square(jnp.maximum(_dot(h_ref[...], w1_ref[...]), 0.0)).astype(BF16)
    o_ref[...] += _dot(a, w2_ref[...])

    @pl.when(k == pl.num_programs(1) - 1)
    def _():
        def rows(sl):
            out = x_ref[sl, :] + gate_ref[...] * o_ref[sl, :]
            if final:
                out = _rms(out, fg_ref[...])
            o_ref[sl, :] = out

        _for_row_chunks(x_ref.shape[0], rows)


def _mlp(x, g, mods, row_of, w1, w2, layer, *, final_g=None, tm=1024, tf=1024, vmem_mib=58):
    t, d = x.shape
    f = w1.shape[2]
    in_specs = [
        pl.BlockSpec((tm, d), lambda i, k: (i, 0), pipeline_mode=pl.Buffered(1)),
        pl.BlockSpec((1, d), lambda i, k: (0, 0)),
        pl.BlockSpec((None, 1, d), lambda i, k: (row_of(i), 0, 3)),
        pl.BlockSpec((None, 1, d), lambda i, k: (row_of(i), 0, 4)),
        pl.BlockSpec((None, 1, d), lambda i, k: (row_of(i), 0, 5)),
        pl.BlockSpec((None, d, tf), lambda i, k: (layer, 0, k)),
        pl.BlockSpec((None, tf, d), lambda i, k: (layer, k, 0)),
    ]
    args = [x, g.reshape(1, d), mods, mods, mods, w1, w2]
    if final_g is not None:
        in_specs.append(pl.BlockSpec((1, d), lambda i, k: (0, 0)))
        args.append(final_g.reshape(1, d))
    return pl.pallas_call(
        functools.partial(_mlp_kernel, final=final_g is not None),
        out_shape=jax.ShapeDtypeStruct((t, d), F32),
        grid=(t // tm, f // tf),
        in_specs=in_specs,
        out_specs=pl.BlockSpec((tm, d), lambda i, k: (i, 0)),
        scratch_shapes=[pltpu.VMEM((tm, d), BF16)],
        compiler_params=_params(("parallel", "arbitrary"), vmem_mib),
        name="mlp",
    )(*args)


def _axial_angles(n_tokens, rot_dim):
    t = jnp.arange(n_tokens)
    row = (t // GRID_W).astype(F32)
    col = (t % GRID_W).astype(F32)
    n_freq = rot_dim // 4
    inv_freq = ROPE_BASE ** (-jnp.arange(n_freq, dtype=F32) / n_freq)
    ang = jnp.concatenate([row[:, None] * inv_freq, col[:, None] * inv_freq], axis=-1)
    return jnp.cos(ang), jnp.sin(ang)


def _swap_halves(w):
    half = w.shape[-1] // 2
    return jnp.concatenate([w[..., half:], w[..., :half]], axis=-1)


def kernel(x, c, ctx, c_ctx, norm1_g, w_mod, b_mod, norm2_g, w_ff1, w_ff2, even_w_in, mla_q_norm_g,
           mla_w_uq, mla_kv_norm_g, mla_w_ukv, gmlp_w_sp, gmlp_b_sp, even_w_out, odd_w_in, swa_sinks,
           odd_w_out, final_norm_g):
    batch, n_lat, d = x.shape
    n_ctx = ctx.shape[1]
    assert batch + 1 <= MOD_ROWS
    xl = x.reshape(batch * n_lat, d)
    xc = ctx.reshape(batch * n_ctx, d)

    tm = 1024

    def lat_row(block_rows):
        per_sample = n_lat // block_rows
        return lambda i: i // per_sample

    ctx_row = lambda i: batch

    cvec = jnp.concatenate([c, c_ctx[None, :], jnp.zeros((MOD_ROWS - batch - 1, d), F32)], axis=0)
    mods = _modulation(cvec, w_mod, b_mod)
    mods = mods.reshape(mods.shape[0], MOD_ROWS, 1, N_MOD * d)

    cos_m, sin_m = _axial_angles(n_lat, MLA_ROPE)
    zeros64 = jnp.zeros((n_lat, LANES // 2), F32)
    fold_cos = jnp.concatenate([cos_m, cos_m, zeros64], axis=-1)
    fold_sin = jnp.concatenate([-sin_m, sin_m, zeros64], axis=-1)
    keep = jnp.concatenate([jnp.ones((tm, LANES // 2), F32), jnp.zeros((tm, LANES // 2), F32)], axis=-1)
    drop = jnp.zeros((tm, LANES), F32)

    w_in = even_w_in[0]
    cq_w, ckv_w = w_in[:, :MLA_LORA], w_in[:, MLA_LORA:2 * MLA_LORA]
    kr_w = w_in[:, 2 * MLA_LORA:2 * MLA_LORA + MLA_ROPE]
    gm_w = w_in[:, 2 * MLA_LORA + MLA_ROPE:]
    w_in0 = jnp.concatenate([cq_w, ckv_w, gm_w, kr_w, _swap_halves(kr_w)], axis=-1).astype(BF16)
    n_in0 = w_in0.shape[1]
    kr_group = (n_in0 - LANES) // LANES
    kr_rope = (None, (kr_group,))

    wq = mla_w_uq[0].reshape(MLA_LORA, MLA_HEADS, MLA_NOPE + MLA_ROPE)
    wq_rope = wq[..., MLA_NOPE:]
    w_uq = jnp.concatenate([wq[..., :MLA_NOPE], wq_rope, _swap_halves(wq_rope)], axis=-1)
    w_uq = (w_uq * (MLA_SCALE * LOG2E)).reshape(MLA_LORA, MLA_HEADS * MLA_QK_PAD).astype(BF16)
    q_rope = (None, tuple(range(1, 2 * MLA_HEADS, 2)))
    w_ukv = mla_w_ukv[0].astype(BF16)
    w_sp = gmlp_w_sp[0].astype(BF16)
    b_sp = gmlp_b_sp[0][:, :, None]
    w_out0 = even_w_out[0].astype(BF16)
    w1_all, w2_all = w_ff1.astype(BF16), w_ff2.astype(BF16)
    tm_i = 512

    def layer0_tokens(xs, tabs, latent):
        row_of = (lambda rows: lat_row(rows)) if latent else (lambda rows: ctx_row)
        pos_blocks = (lambda rows: n_lat // rows) if latent else (lambda rows: 1)
        m0 = mods[0]
        z = _proj(xs, 0, d, norm1_g[0], w_in0, tm=tm_i, tn=n_in0, out_dtype=F32,
                  mod=(m0, row_of(tm_i), 0, 1), rope=kr_rope, tabs=tabs, pos_blocks=pos_blocks(tm_i),
                  vmem_mib=56, name="even_in_proj")
        q = _proj(z, 0, MLA_LORA, mla_q_norm_g[0], w_uq, tm=tm, tn=w_uq.shape[1], out_dtype=BF16,
                  rope=q_rope, tabs=tabs, pos_blocks=pos_blocks(tm), name="mla_q_proj")
        kv = _proj(z, 1, MLA_LORA, mla_kv_norm_g[0], w_ukv, tm=tm, tn=w_ukv.shape[1], out_dtype=BF16,
                   name="mla_kv_proj")
        gm = _gmlp(z, 1, 2, w_sp, b_sp)
        return z, q, kv, gm

    z_l, q_l, kv_l, gm_l = layer0_tokens(xl, (fold_cos, fold_sin), True)
    z_c, q_c, kv_c, gm_c = layer0_tokens(xc, (keep, drop), False)
    att_l = _mla_attention(q_l, kv_c, z_c, kr_group, batch=batch, n_q=n_lat, n_ctx=n_ctx,
                           kv_l=kv_l, z_l=z_l, tq=n_lat)
    att_c = _mla_attention(q_c, kv_c, z_c, kr_group, batch=batch, n_q=n_ctx, n_ctx=n_ctx)

    tm_o = 512
    xl = _outproj(att_l, 0, gm_l, 0, w_out0, xl, mods[0], lat_row(tm_o), 2, tm=tm_o)
    xc = _outproj(att_c, 0, gm_c, 0, w_out0, xc, mods[0], ctx_row, 2, tm=tm_o)
    tm_f = 1024
    xl = _mlp(xl, norm2_g[0], mods[0], lat_row(tm_f), w1_all, w2_all, 0, tm=tm_f)
    xc = _mlp(xc, norm2_g[0], mods[0], ctx_row, w1_all, w2_all, 0, tm=tm_f)

    cos_s, sin_s = _axial_angles(n_lat, SWA_HEAD_DIM)
    pair_cos = jnp.concatenate([cos_s] * 4, axis=-1)
    pair_sin = jnp.concatenate([-sin_s, -sin_s, sin_s, sin_s], axis=-1)
    half = SWA_HEAD_DIM // 2
    w_in = odd_w_in[0]
    wq = (w_in[:, :SWA_Q_COLS] * (SWA_SCALE * LOG2E)).reshape(d, SWA_HEADS // 2, 2, 2, half)
    wq = wq.transpose(0, 1, 3, 2, 4).reshape(d, SWA_Q_COLS)
    wk = w_in[:, SWA_Q_COLS:SWA_Q_COLS + SWA_KV_COLS].reshape(d, SWA_KV_HEADS, 2, 1, half)
    wk = jnp.broadcast_to(wk, (d, SWA_KV_HEADS, 2, 2, half)).reshape(d, 2 * SWA_KV_COLS)
    wv = w_in[:, SWA_Q_COLS + SWA_KV_COLS:]
    w_in1 = jnp.concatenate([wq, wk, wv], axis=-1).astype(BF16)
    w_kv1 = jnp.concatenate([wk, wv], axis=-1).astype(BF16)
    rot_groups = (SWA_Q_COLS + 2 * SWA_KV_COLS) // LANES
    m1 = mods[1]
    z1_l = _proj(xl, 0, d, norm1_g[1], w_in1, tm=tm_i, tn=w_in1.shape[1], out_dtype=BF16,
                 mod=(m1, lat_row(tm_i), 0, 1), rope=(None, tuple(range(rot_groups))),
                 tabs=(pair_cos, pair_sin), pos_blocks=n_lat // tm_i, vmem_mib=56, name="odd_in_proj")
    z1_c = _proj(xc, 0, d, norm1_g[1], w_kv1, tm=tm, tn=w_kv1.shape[1], out_dtype=BF16,
                 mod=(m1, ctx_row, 0, 1), name="odd_ctx_kv_proj")
    att = _swa_attention(z1_l, z1_c, swa_sinks[0], batch=batch, n_lat=n_lat, n_ctx=n_ctx)
    xl = _outproj(att, 0, att, 1, odd_w_out[0].astype(BF16), xl, m1, lat_row(tm_o), 2, tm=tm_o)
    out = _mlp(xl, norm2_g[1], m1, lat_row(tm_f), w1_all, w2_all, 1, final_g=final_norm_g, tm=tm_f)
    return out.reshape(batch, n_lat, d)
```

```python
import functools
import math

import jax
import jax.numpy as jnp
from jax import lax
from jax.experimental import pallas as pl
from jax.experimental.pallas import tpu as pltpu

F32 = jnp.float32
BF16 = jnp.bfloat16

LANES = 128
MIB = 1 << 20

GRID_W = 64
N_MOD = 6
NORM_EPS = 1e-6
ROPE_BASE = 10000.0
NEG_INF = -1e30

MLA_HEADS = 8
MLA_LORA = 512
MLA_NOPE = 128
MLA_ROPE = 64
MLA_V = 128
MLA_QK_PAD = 2 * LANES
MLA_SCALE = 1.0 / math.sqrt(MLA_NOPE + MLA_ROPE)
MLA_Q_SUB = 512

GMLP_GROUPS = 8
GMLP_DIM = 128
GMLP_CHUNK = 128
GMLP_WIDTH = GMLP_GROUPS * GMLP_DIM

SWA_HEADS = 32
SWA_KV_HEADS = 4
SWA_GROUP = SWA_HEADS // SWA_KV_HEADS
SWA_HEAD_DIM = 64
SWA_WINDOW = 128
SWA_BLOCK = 128
SWA_SPAN = SWA_BLOCK + 2 * SWA_WINDOW
SWA_SCALE = 1.0 / math.sqrt(SWA_HEAD_DIM)
SWA_Q_COLS = SWA_HEADS * SWA_HEAD_DIM
SWA_KV_COLS = SWA_KV_HEADS * SWA_HEAD_DIM

MOD_ROWS = 16
ROW_CHUNK = 128
ONES_ROWS = 16
LOG2E = math.log2(math.e)


def _params(semantics, vmem_mib):
    return pltpu.CompilerParams(dimension_semantics=semantics, vmem_limit_bytes=vmem_mib * MIB)


def _rms(x, g):
    return x * lax.rsqrt(jnp.mean(x * x, axis=-1, keepdims=True) + NORM_EPS) * g


def _rms_rows(x_ref, sl, g):
    x = x_ref[sl, :]
    r = lax.rsqrt(jnp.mean(x * x, axis=-1, keepdims=True) + NORM_EPS)
    return x_ref[sl, :] * r * g


def _for_row_chunks(n_rows, fn, rows=ROW_CHUNK):
    rows = min(rows, n_rows)

    def body(r, carry):
        fn(pl.ds(pl.multiple_of(r * rows, rows), rows))
        return carry

    lax.fori_loop(0, n_rows // rows, body, 0)


def _dot(a, b):
    return jnp.dot(a, b, preferred_element_type=F32)


def _dot_nt(a, b):
    return lax.dot_general(a, b, (((1,), (1,)), ((), ())), preferred_element_type=F32)


def _mod_kernel(c_ref, w_ref, b_ref, o_ref):
    c = c_ref[...]
    s = (c * jax.nn.sigmoid(c)).astype(BF16)
    o_ref[...] = _dot(s, w_ref[...].astype(BF16)) + b_ref[...]


def _modulation(cvec, w_mod, b_mod, *, tn=1024):
    depth, d, n = w_mod.shape
    return pl.pallas_call(
        _mod_kernel,
        out_shape=jax.ShapeDtypeStruct((depth, MOD_ROWS, n), F32),
        grid=(depth, n // tn),
        in_specs=[
            pl.BlockSpec((MOD_ROWS, d), lambda l, j: (0, 0)),
            pl.BlockSpec((None, d, tn), lambda l, j: (l, 0, j)),
            pl.BlockSpec((None, 1, tn), lambda l, j: (l, 0, j)),
        ],
        out_specs=pl.BlockSpec((None, MOD_ROWS, tn), lambda l, j: (l, 0, j)),
        compiler_params=_params(("parallel", "parallel"), 40),
        name="modulation",
    )(cvec, w_mod, b_mod.reshape(depth, 1, n))


def _proj_kernel(*refs, modulated, rope):
    it = iter(refs)
    x_ref, g_ref = next(it), next(it)
    sh_ref = sc_ref = None
    if modulated:
        sh_ref, sc_ref = next(it), next(it)
    w_ref = next(it)
    tabs = [next(it) for _ in range(2 if rope else 0)]
    o_ref, h_ref = next(it), next(it)
    j = pl.program_id(1)

    @pl.when(j == 0)
    def _():
        def rows(sl):
            y = _rms_rows(x_ref, sl, g_ref[...])
            if modulated:
                y = y * (1.0 + sc_ref[...]) + sh_ref[...]
            h_ref[sl, :] = y.astype(BF16)

        _for_row_chunks(x_ref.shape[0], rows)

    z = _dot(h_ref[...], w_ref[...])
    groups_per_tile = z.shape[1] // LANES

    if rope is None:
        o_ref[...] = z.astype(o_ref.dtype)
        return

    tile, groups = rope

    def rotated(t):
        return t * tabs[0][...] + pltpu.roll(t, LANES // 2, 1) * tabs[1][...]

    if tile is None:
        for gi in range(groups_per_tile):
            sl = slice(gi * LANES, (gi + 1) * LANES)
            t = z[:, sl]
            o_ref[:, sl] = (rotated(t) if gi in groups else t).astype(o_ref.dtype)
    else:
        o_ref[...] = z.astype(o_ref.dtype)

        @pl.when(j == tile)
        def _():
            for gi in groups:
                sl = slice(gi * LANES, (gi + 1) * LANES)
                o_ref[:, sl] = rotated(z[:, sl]).astype(o_ref.dtype)


def _proj(x, xcol, k, g, w, *, tm, tn, out_dtype, mod=None, rope=None, tabs=(), pos_blocks=1,
          vmem_mib=48, name="proj"):
    t = x.shape[0]
    n = w.shape[1]
    assert t % tm == 0 and n % tn == 0 and w.shape[0] == k
    in_specs = [
        pl.BlockSpec((tm, k), lambda i, j: (i, xcol)),
        pl.BlockSpec((1, k), lambda i, j: (0, 0)),
    ]
    args = [x, g.reshape(1, k)]
    if mod is not None:
        mods, row_of, k_shift, k_scale = mod
        in_specs += [
            pl.BlockSpec((None, 1, k), lambda i, j: (row_of(i), 0, k_shift)),
            pl.BlockSpec((None, 1, k), lambda i, j: (row_of(i), 0, k_scale)),
        ]
        args += [mods, mods]
    w_mode = {"pipeline_mode": pl.Buffered(1)} if n == tn else {}
    in_specs.append(pl.BlockSpec((k, tn), lambda i, j: (0, j), **w_mode))
    args.append(w)
    for tab in tabs:
        in_specs.append(pl.BlockSpec((tm, LANES), lambda i, j: (i % pos_blocks, 0)))
        args.append(tab)
    return pl.pallas_call(
        functools.partial(_proj_kernel, modulated=mod is not None, rope=rope),
        out_shape=jax.ShapeDtypeStruct((t, n), out_dtype),
        grid=(t // tm, n // tn),
        in_specs=in_specs,
        out_specs=pl.BlockSpec((tm, tn), lambda i, j: (i, j)),
        scratch_shapes=[pltpu.VMEM((tm, k), BF16)],
        compiler_params=_params(("parallel", "arbitrary"), vmem_mib),
        name=name,
    )(*args)


def _mla_kernel(*refs, n_lat, n_ctx):
    if n_lat:
        q_ref, knl_ref, vl_ref, krl_ref, knc_ref, vc_ref, krc_ref, o_ref, k_scr, vt_scr = refs
    else:
        q_ref, knc_ref, vc_ref, krc_ref, o_ref, k_scr, vt_scr = refs

    @pl.when(pl.program_id(2) == 0)
    def _():
        if n_lat:
            k_scr[0:n_lat, 0:LANES] = knl_ref[...]
            k_scr[0:n_lat, LANES:] = krl_ref[...].astype(BF16)
            vt_scr[0:MLA_V, 0:n_lat] = vl_ref[...].astype(F32).T.astype(BF16)
        k_scr[n_lat:, 0:LANES] = knc_ref[...]
        k_scr[n_lat:, LANES:] = krc_ref[...].astype(BF16)
        vt_scr[0:MLA_V, n_lat:] = vc_ref[...].astype(F32).T.astype(BF16)
        vt_scr[MLA_V:, :] = jnp.ones((ONES_ROWS, n_lat + n_ctx), BF16)

    n_sub = min(MLA_Q_SUB, q_ref.shape[0])
    subs = [slice(r, r + n_sub) for r in range(0, q_ref.shape[0], n_sub)]
    def scores(i):
        return _dot_nt(k_scr[...], q_ref[subs[i], :])

    def weights(s):
        return jnp.exp2(s - jnp.max(s, axis=0, keepdims=True)).astype(BF16)

    n = len(subs)
    s = [scores(i) if i < 2 else None for i in range(n)]
    p = weights(s[0])
    for i, rows in enumerate(subs):
        if i + 2 < n:
            s[i + 2] = scores(i + 2)
        o = _dot(vt_scr[...], p)
        if i + 1 < n:
            p = weights(s[i + 1])
        o_ref[rows, :] = (o[0:MLA_V] / o[MLA_V:MLA_V + 1]).T.astype(o_ref.dtype)


def _mla_attention(q, kv_c, z_c, kr_col, *, batch, n_q, n_ctx, kv_l=None, z_l=None, tq=256):
    n_lat = 0 if kv_l is None else kv_l.shape[0] // batch
    nq_blocks = n_q // tq
    in_specs = [pl.BlockSpec((tq, MLA_QK_PAD), lambda b, h, i: (b * nq_blocks + i, h))]
    args = [q]
    if n_lat:
        in_specs += [
            pl.BlockSpec((n_lat, LANES), lambda b, h, i: (b, 2 * h)),
            pl.BlockSpec((n_lat, LANES), lambda b, h, i: (b, 2 * h + 1)),
            pl.BlockSpec((n_lat, LANES), lambda b, h, i: (b, kr_col)),
        ]
        args += [kv_l, kv_l, z_l]
    in_specs += [
        pl.BlockSpec((n_ctx, LANES), lambda b, h, i: (b, 2 * h)),
        pl.BlockSpec((n_ctx, LANES), lambda b, h, i: (b, 2 * h + 1)),
        pl.BlockSpec((n_ctx, LANES), lambda b, h, i: (b, kr_col)),
    ]
    args += [kv_c, kv_c, z_c]
    n_keys = n_lat + n_ctx
    return pl.pallas_call(
        functools.partial(_mla_kernel, n_lat=n_lat, n_ctx=n_ctx),
        out_shape=jax.ShapeDtypeStruct((batch * n_q, MLA_HEADS * MLA_V), BF16),
        grid=(batch, MLA_HEADS, nq_blocks),
        in_specs=in_specs,
        out_specs=pl.BlockSpec((tq, MLA_V), lambda b, h, i: (b * nq_blocks + i, h)),
        scratch_shapes=[pltpu.VMEM((n_keys, MLA_QK_PAD), BF16),
                        pltpu.VMEM((MLA_V + ONES_ROWS, n_keys), BF16)],
        compiler_params=_params(("parallel", "parallel", "arbitrary"), 48),
        name="mla_attention",
    )(*args)


def _gelu(x):
    c = math.sqrt(2.0 / math.pi)
    return x * (0.5 + 0.5 * jnp.tanh(x * (c + (c * 0.044715) * (x * x))))


def _gmlp_kernel(u_ref, v_ref, w_ref, b_ref, o_ref):
    for n in range(u_ref.shape[0] // GMLP_CHUNK):
        rows = slice(n * GMLP_CHUNK, (n + 1) * GMLP_CHUNK)
        for g in range(GMLP_GROUPS):
            cols = slice(g * GMLP_DIM, (g + 1) * GMLP_DIM)
            v = _gelu(v_ref[rows, cols])
            vc = v - jnp.mean(v, axis=-1, keepdims=True)
            var = jnp.mean(vc * vc, axis=-1, keepdims=True)
            vn = (vc * lax.rsqrt(var + NORM_EPS)).astype(BF16)
            mixed = _dot(w_ref[g], vn) + b_ref[g]
            o_ref[rows, cols] = (_gelu(u_ref[rows, cols]) * mixed).astype(o_ref.dtype)


def _gmlp(z, u_col, v_col, w_sp, b_sp, *, tm=256):
    t = z.shape[0]
    return pl.pallas_call(
        _gmlp_kernel,
        out_shape=jax.ShapeDtypeStruct((t, GMLP_WIDTH), BF16),
        grid=(t // tm,),
        in_specs=[
            pl.BlockSpec((tm, GMLP_WIDTH), lambda i: (i, u_col)),
            pl.BlockSpec((tm, GMLP_WIDTH), lambda i: (i, v_col)),
            pl.BlockSpec((GMLP_GROUPS, GMLP_CHUNK, GMLP_CHUNK), lambda i: (0, 0, 0)),
            pl.BlockSpec((GMLP_GROUPS, GMLP_CHUNK, 1), lambda i: (0, 0, 0)),
        ],
        out_specs=pl.BlockSpec((tm, GMLP_WIDTH), lambda i: (i, 0)),
        compiler_params=_params(("parallel",), 32),
        name="gmlp",
    )(z, z, w_sp, b_sp)


def _swa_kernel(sink_ref, q_ref, kl_ref, vl_ref, kc_ref, vc_ref, o_ref, *, tq, n_lat):
    start = pl.program_id(1) * tq
    span0 = pl.multiple_of(jnp.clip(start - SWA_WINDOW, 0, n_lat - SWA_SPAN), SWA_WINDOW)
    k_span = kl_ref[pl.ds(span0, SWA_SPAN), :]
    k_ctx = kc_ref[...]
    vt_span = vl_ref[pl.ds(span0, SWA_SPAN), :].astype(F32).T.astype(BF16)
    vt_ctx = vc_ref[...].astype(F32).T.astype(BF16)
    ones_span = jnp.ones((ONES_ROWS, vt_span.shape[1]), BF16)
    ones_ctx = jnp.ones((ONES_ROWS, vt_ctx.shape[1]), BF16)

    k_pos = span0 + lax.broadcasted_iota(jnp.int32, (SWA_SPAN, tq), 0)
    q_pos = start + lax.broadcasted_iota(jnp.int32, (SWA_SPAN, tq), 1)
    bias = jnp.where(jnp.abs(k_pos - q_pos) <= SWA_WINDOW, 0.0, NEG_INF).astype(F32)
    bias = jnp.concatenate([bias] * SWA_GROUP, axis=1)

    lane = lax.broadcasted_iota(jnp.int32, (tq, LANES), 1)
    first = (lane % SWA_HEAD_DIM) < (SWA_HEAD_DIM // 2)
    keep = [jnp.where(first, 1.0, 0.0).astype(BF16), jnp.where(first, 0.0, 1.0).astype(BF16)]

    def scores(g):
        heads = range(g * SWA_GROUP, (g + 1) * SWA_GROUP)
        k_cols = slice(g * LANES, (g + 1) * LANES)
        q = jnp.concatenate(
            [q_ref[:, (h // 2) * LANES:(h // 2 + 1) * LANES] * keep[h % 2] for h in heads], axis=0)
        return _dot_nt(k_span[:, k_cols], q) + bias, _dot_nt(k_ctx[:, k_cols], q)

    s_next = scores(0)
    for g in range(SWA_KV_HEADS):
        heads = range(g * SWA_GROUP, (g + 1) * SWA_GROUP)
        kv_cols = slice(g * SWA_HEAD_DIM, (g + 1) * SWA_HEAD_DIM)
        s_l, s_c = s_next
        if g + 1 < SWA_KV_HEADS:
            s_next = scores(g + 1)
        sink = jnp.concatenate([jnp.full((1, tq), sink_ref[h] * LOG2E, F32) for h in heads], axis=1)
        m = jnp.maximum(jnp.maximum(jnp.max(s_l, axis=0, keepdims=True),
                                    jnp.max(s_c, axis=0, keepdims=True)), sink)
        p_l = jnp.exp2(s_l - m).astype(BF16)
        p_c = jnp.exp2(s_c - m).astype(BF16)
        vt_l = jnp.concatenate([vt_span[kv_cols, :], ones_span], axis=0)
        vt_c = jnp.concatenate([vt_ctx[kv_cols, :], ones_ctx], axis=0)
        o = _dot(vt_l, p_l) + _dot(vt_c, p_c)
        dim = SWA_HEAD_DIM
        o = o[0:dim] / (o[dim:dim + 1] + jnp.exp2(sink - m))
        for pair in range(SWA_GROUP // 2):
            cols = [slice((2 * pair + r) * tq, (2 * pair + r + 1) * tq) for r in range(2)]
            both = jnp.concatenate([o[:, cols[0]], o[:, cols[1]]], axis=0)
            group = g * (SWA_GROUP // 2) + pair
            o_ref[:, group * LANES:(group + 1) * LANES] = both.T.astype(o_ref.dtype)


def _swa_attention(z_l, z_c, sinks, *, batch, n_lat, n_ctx, tq=SWA_BLOCK):
    nq_blocks = n_lat // tq
    k_cols = 2 * SWA_KV_COLS
    return pl.pallas_call(
        functools.partial(_swa_kernel, tq=tq, n_lat=n_lat),
        out_shape=jax.ShapeDtypeStruct((batch * n_lat, SWA_Q_COLS), BF16),
        grid=(batch, nq_blocks),
        in_specs=[
            pl.BlockSpec(memory_space=pltpu.SMEM),
            pl.BlockSpec((tq, SWA_Q_COLS), lambda b, i: (b * nq_blocks + i, 0)),
            pl.BlockSpec((n_lat, k_cols), lambda b, i: (b, SWA_Q_COLS // k_cols)),
            pl.BlockSpec((n_lat, SWA_KV_COLS), lambda b, i: (b, (SWA_Q_COLS + k_cols) // SWA_KV_COLS)),
            pl.BlockSpec((n_ctx, k_cols), lambda b, i: (b, 0)),
            pl.BlockSpec((n_ctx, SWA_KV_COLS), lambda b, i: (b, k_cols // SWA_KV_COLS)),
        ],
        out_specs=pl.BlockSpec((tq, SWA_Q_COLS), lambda b, i: (b * nq_blocks + i, 0)),
        compiler_params=_params(("parallel", "arbitrary"), 48),
        name="swa_attention",
    )(sinks, z_l, z_l, z_l, z_c, z_c)


def _outproj_kernel(a1_ref, a2_ref, w1_ref, w2_ref, x_ref, gate_ref, o_ref):
    y = _dot(a1_ref[...], w1_ref[...]) + _dot(a2_ref[...], w2_ref[...])
    o_ref[...] = x_ref[...] + gate_ref[...] * y


def _outproj(a1, a1_col, a2, a2_col, w, x, mods, row_of, k_gate, *, tm=512):
    t, d = x.shape
    half = w.shape[0] // 2
    return pl.pallas_call(
        _outproj_kernel,
        out_shape=jax.ShapeDtypeStruct((t, d), F32),
        grid=(t // tm,),
        in_specs=[
            pl.BlockSpec((tm, half), lambda i: (i, a1_col)),
            pl.BlockSpec((tm, half), lambda i: (i, a2_col)),
            pl.BlockSpec((half, d), lambda i: (0, 0)),
            pl.BlockSpec((half, d), lambda i: (1, 0)),
            pl.BlockSpec((tm, d), lambda i: (i, 0)),
            pl.BlockSpec((None, 1, d), lambda i: (row_of(i), 0, k_gate)),
        ],
        out_specs=pl.BlockSpec((tm, d), lambda i: (i, 0)),
        compiler_params=_params(("parallel",), 48),
        name="outproj",
    )(a1, a2, w, w, x, mods)


def _mlp_kernel(*refs, final):
    if final:
        x_ref, g_ref, sh_ref, sc_ref, gate_ref, w1_ref, w2_ref, fg_ref, o_ref, h_ref = refs
    else:
        x_ref, g_ref, sh_ref, sc_ref, gate_ref, w1_ref, w2_ref, o_ref, h_ref = refs
    k = pl.program_id(1)

    @pl.when(k == 0)
    def _():
        def rows(sl):
            y = _rms_rows(x_ref, sl, g_ref[...]) * (1.0 + sc_ref[...]) + sh_ref[...]
            h_ref[sl, :] = y.astype(BF16)
            o_ref[sl, :] = jnp.zeros((sl.size, o_ref.shape[1]), F32)

        _for_row_chunks(x_ref.shape[0], rows)

    a = jnp.square(jnp.maximum(_dot(h_ref[...], w1_ref[...]), 0.0)).astype(BF16)
    o_ref[...] += _dot(a, w2_ref[...])

    @pl.when(k == pl.num_programs(1) - 1)
    def _():
        def rows(sl):
            out = x_ref[sl, :] + gate_ref[...] * o_ref[sl, :]
            if final:
                out = _rms(out, fg_ref[...])
            o_ref[sl, :] = out

        _for_row_chunks(x_ref.shape[0], rows)


def _mlp(x, g, mods, row_of, w1, w2, layer, *, final_g=None, tm=512, tf=1024, vmem_mib=48):
    t, d = x.shape
    f = w1.shape[2]
    in_specs = [
        pl.BlockSpec((tm, d), lambda i, k: (i, 0)),
        pl.BlockSpec((1, d), lambda i, k: (0, 0)),
        pl.BlockSpec((None, 1, d), lambda i, k: (row_of(i), 0, 3)),
        pl.BlockSpec((None, 1, d), lambda i, k: (row_of(i), 0, 4)),
        pl.BlockSpec((None, 1, d), lambda i, k: (row_of(i), 0, 5)),
        pl.BlockSpec((None, d, tf), lambda i, k: (layer, 0, k)),
        pl.BlockSpec((None, tf, d), lambda i, k: (layer, k, 0)),
    ]
    args = [x, g.reshape(1, d), mods, mods, mods, w1, w2]
    if final_g is not None:
        in_specs.append(pl.BlockSpec((1, d), lambda i, k: (0, 0)))
        args.append(final_g.reshape(1, d))
    return pl.pallas_call(
        functools.partial(_mlp_kernel, final=final_g is not None),
        out_shape=jax.ShapeDtypeStruct((t, d), F32),
        grid=(t // tm, f // tf),
        in_specs=in_specs,
        out_specs=pl.BlockSpec((tm, d), lambda i, k: (i, 0)),
        scratch_shapes=[pltpu.VMEM((tm, d), BF16)],
        compiler_params=_params(("parallel", "arbitrary"), vmem_mib),
        name="mlp",
    )(*args)


def _axial_angles(n_tokens, rot_dim):
    t = jnp.arange(n_tokens)
    row = (t // GRID_W).astype(F32)
    col = (t % GRID_W).astype(F32)
    n_freq = rot_dim // 4
    inv_freq = ROPE_BASE ** (-jnp.arange(n_freq, dtype=F32) / n_freq)
    ang = jnp.concatenate([row[:, None] * inv_freq, col[:, None] * inv_freq], axis=-1)
    return jnp.cos(ang), jnp.sin(ang)


def _swap_halves(w):
    half = w.shape[-1] // 2
    return jnp.concatenate([w[..., half:], w[..., :half]], axis=-1)


def kernel(x, c, ctx, c_ctx, norm1_g, w_mod, b_mod, norm2_g, w_ff1, w_ff2, even_w_in, mla_q_norm_g,
           mla_w_uq, mla_kv_norm_g, mla_w_ukv, gmlp_w_sp, gmlp_b_sp, even_w_out, odd_w_in, swa_sinks,
           odd_w_out, final_norm_g):
    batch, n_lat, d = x.shape
    n_ctx = ctx.shape[1]
    assert batch + 1 <= MOD_ROWS
    xl = x.reshape(batch * n_lat, d)
    xc = ctx.reshape(batch * n_ctx, d)

    tm = 1024

    def lat_row(block_rows):
        per_sample = n_lat // block_rows
        return lambda i: i // per_sample

    ctx_row = lambda i: batch

    cvec = jnp.concatenate([c, c_ctx[None, :], jnp.zeros((MOD_ROWS - batch - 1, d), F32)], axis=0)
    mods = _modulation(cvec, w_mod, b_mod)
    mods = mods.reshape(mods.shape[0], MOD_ROWS, 1, N_MOD * d)

    cos_m, sin_m = _axial_angles(n_lat, MLA_ROPE)
    zeros64 = jnp.zeros((n_lat, LANES // 2), F32)
    fold_cos = jnp.concatenate([cos_m, cos_m, zeros64], axis=-1)
    fold_sin = jnp.concatenate([-sin_m, sin_m, zeros64], axis=-1)
    keep = jnp.concatenate([jnp.ones((tm, LANES // 2), F32), jnp.zeros((tm, LANES // 2), F32)], axis=-1)
    drop = jnp.zeros((tm, LANES), F32)

    w_in = even_w_in[0]
    cq_w, ckv_w = w_in[:, :MLA_LORA], w_in[:, MLA_LORA:2 * MLA_LORA]
    kr_w = w_in[:, 2 * MLA_LORA:2 * MLA_LORA + MLA_ROPE]
    gm_w = w_in[:, 2 * MLA_LORA + MLA_ROPE:]
    w_in0 = jnp.concatenate([cq_w, ckv_w, gm_w, kr_w, _swap_halves(kr_w)], axis=-1).astype(BF16)
    n_in0 = w_in0.shape[1]
    kr_group = (n_in0 - LANES) // LANES
    kr_rope = (None, (kr_group,))

    wq = mla_w_uq[0].reshape(MLA_LORA, MLA_HEADS, MLA_NOPE + MLA_ROPE)
    wq_rope = wq[..., MLA_NOPE:]
    w_uq = jnp.concatenate([wq[..., :MLA_NOPE], wq_rope, _swap_halves(wq_rope)], axis=-1)
    w_uq = (w_uq * (MLA_SCALE * LOG2E)).reshape(MLA_LORA, MLA_HEADS * MLA_QK_PAD).astype(BF16)
    q_rope = (None, tuple(range(1, 2 * MLA_HEADS, 2)))
    w_ukv = mla_w_ukv[0].astype(BF16)
    w_sp = gmlp_w_sp[0].astype(BF16)
    b_sp = gmlp_b_sp[0][:, :, None]
    w_out0 = even_w_out[0].astype(BF16)
    w1_all, w2_all = w_ff1.astype(BF16), w_ff2.astype(BF16)
    tm_i = 512

    def layer0_tokens(xs, tabs, latent):
        row_of = (lambda rows: lat_row(rows)) if latent else (lambda rows: ctx_row)
        pos_blocks = (lambda rows: n_lat // rows) if latent else (lambda rows: 1)
        m0 = mods[0]
        z = _proj(xs, 0, d, norm1_g[0], w_in0, tm=tm_i, tn=n_in0, out_dtype=F32,
                  mod=(m0, row_of(tm_i), 0, 1), rope=kr_rope, tabs=tabs, pos_blocks=pos_blocks(tm_i),
                  vmem_mib=56, name="even_in_proj")
        q = _proj(z, 0, MLA_LORA, mla_q_norm_g[0], w_uq, tm=tm, tn=w_uq.shape[1], out_dtype=BF16,
                  rope=q_rope, tabs=tabs, pos_blocks=pos_blocks(tm), name="mla_q_proj")
        kv = _proj(z, 1, MLA_LORA, mla_kv_norm_g[0], w_ukv, tm=tm, tn=w_ukv.shape[1], out_dtype=BF16,
                   name="mla_kv_proj")
        gm = _gmlp(z, 1, 2, w_sp, b_sp)
        return z, q, kv, gm

    z_l, q_l, kv_l, gm_l = layer0_tokens(xl, (fold_cos, fold_sin), True)
    z_c, q_c, kv_c, gm_c = layer0_tokens(xc, (keep, drop), False)
    att_l = _mla_attention(q_l, kv_c, z_c, kr_group, batch=batch, n_q=n_lat, n_ctx=n_ctx,
                           kv_l=kv_l, z_l=z_l, tq=n_lat)
    att_c = _mla_attention(q_c, kv_c, z_c, kr_group, batch=batch, n_q=n_ctx, n_ctx=n_ctx)

    tm_o = 512
    xl = _outproj(att_l, 0, gm_l, 0, w_out0, xl, mods[0], lat_row(tm_o), 2, tm=tm_o)
    xc = _outproj(att_c, 0, gm_c, 0, w_out0, xc, mods[0], ctx_row, 2, tm=tm_o)
    tm_f = 512
    xl = _mlp(xl, norm2_g[0], mods[0], lat_row(tm_f), w1_all, w2_all, 0, tm=tm_f)
    xc = _mlp(xc, norm2_g[0], mods[0], ctx_row, w1_all, w2_all, 0, tm=tm_f)

    cos_s, sin_s = _axial_angles(n_lat, SWA_HEAD_DIM)
    pair_cos = jnp.concatenate([cos_s] * 4, axis=-1)
    pair_sin = jnp.concatenate([-sin_s, -sin_s, sin_s, sin_s], axis=-1)
    half = SWA_HEAD_DIM // 2
    w_in = odd_w_in[0]
    wq = (w_in[:, :SWA_Q_COLS] * (SWA_SCALE * LOG2E)).reshape(d, SWA_HEADS // 2, 2, 2, half)
    wq = wq.transpose(0, 1, 3, 2, 4).reshape(d, SWA_Q_COLS)
    wk = w_in[:, SWA_Q_COLS:SWA_Q_COLS + SWA_KV_COLS].reshape(d, SWA_KV_HEADS, 2, 1, half)
    wk = jnp.broadcast_to(wk, (d, SWA_KV_HEADS, 2, 2, half)).reshape(d, 2 * SWA_KV_COLS)
    wv = w_in[:, SWA_Q_COLS + SWA_KV_COLS:]
    w_in1 = jnp.concatenate([wq, wk, wv], axis=-1).astype(BF16)
    w_kv1 = jnp.concatenate([wk, wv], axis=-1).astype(BF16)
    rot_groups = (SWA_Q_COLS + 2 * SWA_KV_COLS) // LANES
    m1 = mods[1]
    z1_l = _proj(xl, 0, d, norm1_g[1], w_in1, tm=tm_i, tn=w_in1.shape[1], out_dtype=BF16,
                 mod=(m1, lat_row(tm_i), 0, 1), rope=(None, tuple(range(rot_groups))),
                 tabs=(pair_cos, pair_sin), pos_blocks=n_lat // tm_i, vmem_mib=56, name="odd_in_proj")
    z1_c = _proj(xc, 0, d, norm1_g[1], w_kv1, tm=tm, tn=w_kv1.shape[1], out_dtype=BF16,
                 mod=(m1, ctx_row, 0, 1), name="odd_ctx_kv_proj")
    att = _swa_attention(z1_l, z1_c, swa_sinks[0], batch=batch, n_lat=n_lat, n_ctx=n_ctx)
    xl = _outproj(att, 0, att, 1, odd_w_out[0].astype(BF16), xl, m1, lat_row(tm_o), 2, tm=tm_o)
    out = _mlp(xl, norm2_g[1], m1, lat_row(tm_f), w1_all, w2_all, 1, final_g=final_norm_g, tm=tm_f)
    return out.reshape(batch, n_lat, d)
```

```python
import functools
import math

import jax
import jax.numpy as jnp
from jax import lax
from jax.experimental import pallas as pl
from jax.experimental.pallas import tpu as pltpu

F32 = jnp.float32
BF16 = jnp.bfloat16

LANES = 128
MIB = 1 << 20

GRID_W = 64
N_MOD = 6
NORM_EPS = 1e-6
ROPE_BASE = 10000.0
NEG_INF = -1e30

MLA_HEADS = 8
MLA_LORA = 512
MLA_NOPE = 128
MLA_ROPE = 64
MLA_V = 128
MLA_QK_PAD = 2 * LANES
MLA_SCALE = 1.0 / math.sqrt(MLA_NOPE + MLA_ROPE)
MLA_Q_SUB = 512

GMLP_GROUPS = 8
GMLP_DIM = 128
GMLP_CHUNK = 128
GMLP_WIDTH = GMLP_GROUPS * GMLP_DIM

SWA_HEADS = 32
SWA_KV_HEADS = 4
SWA_GROUP = SWA_HEADS // SWA_KV_HEADS
SWA_HEAD_DIM = 64
SWA_WINDOW = 128
SWA_BLOCK = 128
SWA_SPAN = SWA_BLOCK + 2 * SWA_WINDOW
SWA_SCALE = 1.0 / math.sqrt(SWA_HEAD_DIM)
SWA_Q_COLS = SWA_HEADS * SWA_HEAD_DIM
SWA_KV_COLS = SWA_KV_HEADS * SWA_HEAD_DIM
SWA_UNIT = SWA_GROUP

MOD_ROWS = 16
ROW_CHUNK = 128
STATS_CHUNK = 256
ONES_ROWS = 16
LOG2E = math.log2(math.e)


def _params(semantics, vmem_mib):
    return pltpu.CompilerParams(dimension_semantics=semantics, vmem_limit_bytes=vmem_mib * MIB)


def _rms(x, g):
    return x * lax.rsqrt(jnp.mean(x * x, axis=-1, keepdims=True) + NORM_EPS) * g


def _for_row_chunks(n_rows, fn, rows=ROW_CHUNK):
    rows = min(rows, n_rows)

    def body(r, carry):
        fn(pl.ds(pl.multiple_of(r * rows, rows), rows))
        return carry

    lax.fori_loop(0, n_rows // rows, body, 0)


def _norm_modulate(x_ref, r_ref, store, g, scale=None, shift=None):
    n_rows = x_ref.shape[0]

    def stats(sl):
        x = x_ref[sl, :]
        r_ref[sl, :] = lax.rsqrt(jnp.mean(x * x, axis=-1, keepdims=True) + NORM_EPS)

    _for_row_chunks(n_rows, stats, rows=STATS_CHUNK)
    gain = g if scale is None else g * (1.0 + scale)

    def apply(sl):
        y = x_ref[sl, :] * r_ref[sl, :] * gain
        store(sl, y if shift is None else y + shift)

    _for_row_chunks(n_rows, apply)


def _dot(a, b):
    return jnp.dot(a, b, preferred_element_type=F32)


def _dot_nt(a, b):
    return lax.dot_general(a, b, (((1,), (1,)), ((), ())), preferred_element_type=F32)


def _mod_kernel(c_ref, w_ref, b_ref, o_ref):
    c = c_ref[...]
    s = (c * jax.nn.sigmoid(c)).astype(BF16)
    o_ref[...] = _dot(s, w_ref[...].astype(BF16)) + b_ref[...]


def _modulation(cvec, w_mod, b_mod, *, tn=1024):
    depth, d, n = w_mod.shape
    return pl.pallas_call(
        _mod_kernel,
        out_shape=jax.ShapeDtypeStruct((depth, MOD_ROWS, n), F32),
        grid=(depth, n // tn),
        in_specs=[
            pl.BlockSpec((MOD_ROWS, d), lambda l, j: (0, 0)),
            pl.BlockSpec((None, d, tn), lambda l, j: (l, 0, j)),
            pl.BlockSpec((None, 1, tn), lambda l, j: (l, 0, j)),
        ],
        out_specs=pl.BlockSpec((None, MOD_ROWS, tn), lambda l, j: (l, 0, j)),
        compiler_params=_params(("parallel", "parallel"), 40),
        name="modulation",
    )(cvec, w_mod, b_mod.reshape(depth, 1, n))


def _proj_kernel(*refs, modulated, rope):
    it = iter(refs)
    x_ref, g_ref = next(it), next(it)
    sh_ref = sc_ref = None
    if modulated:
        sh_ref, sc_ref = next(it), next(it)
    w_ref = next(it)
    tabs = [next(it) for _ in range(2 if rope else 0)]
    o_ref, h_ref, r_ref = next(it), next(it), next(it)
    j = pl.program_id(1)

    @pl.when(j == 0)
    def _():
        def store(sl, y):
            h_ref[sl, :] = y.astype(BF16)

        if modulated:
            _norm_modulate(x_ref, r_ref, store, g_ref[...], sc_ref[...], sh_ref[...])
        else:
            _norm_modulate(x_ref, r_ref, store, g_ref[...])

    z = _dot(h_ref[...], w_ref[...])
    groups_per_tile = z.shape[1] // LANES

    if rope is None:
        o_ref[...] = z.astype(o_ref.dtype)
        return

    tile, groups = rope

    def rotated(t):
        return t * tabs[0][...] + pltpu.roll(t, LANES // 2, 1) * tabs[1][...]

    if tile is None:
        for gi in range(groups_per_tile):
            sl = slice(gi * LANES, (gi + 1) * LANES)
            t = z[:, sl]
            o_ref[:, sl] = (rotated(t) if gi in groups else t).astype(o_ref.dtype)
    else:
        o_ref[...] = z.astype(o_ref.dtype)

        @pl.when(j == tile)
        def _():
            for gi in groups:
                sl = slice(gi * LANES, (gi + 1) * LANES)
                o_ref[:, sl] = rotated(z[:, sl]).astype(o_ref.dtype)


def _proj(x, xcol, k, g, w, *, tm, tn, out_dtype, mod=None, rope=None, tabs=(), pos_blocks=1,
          vmem_mib=48, name="proj"):
    t = x.shape[0]
    n = w.shape[1]
    assert t % tm == 0 and n % tn == 0 and w.shape[0] == k
    in_specs = [
        pl.BlockSpec((tm, k), lambda i, j: (i, xcol)),
        pl.BlockSpec((1, k), lambda i, j: (0, 0)),
    ]
    args = [x, g.reshape(1, k)]
    if mod is not None:
        mods, row_of, k_shift, k_scale = mod
        in_specs += [
            pl.BlockSpec((None, 1, k), lambda i, j: (row_of(i), 0, k_shift)),
            pl.BlockSpec((None, 1, k), lambda i, j: (row_of(i), 0, k_scale)),
        ]
        args += [mods, mods]
    w_mode = {"pipeline_mode": pl.Buffered(1)} if n == tn else {}
    in_specs.append(pl.BlockSpec((k, tn), lambda i, j: (0, j), **w_mode))
    args.append(w)
    for tab in tabs:
        in_specs.append(pl.BlockSpec((tm, LANES), lambda i, j: (i % pos_blocks, 0)))
        args.append(tab)
    return pl.pallas_call(
        functools.partial(_proj_kernel, modulated=mod is not None, rope=rope),
        out_shape=jax.ShapeDtypeStruct((t, n), out_dtype),
        grid=(t // tm, n // tn),
        in_specs=in_specs,
        out_specs=pl.BlockSpec((tm, tn), lambda i, j: (i, j)),
        scratch_shapes=[pltpu.VMEM((tm, k), BF16), pltpu.VMEM((tm, 1), F32)],
        compiler_params=_params(("parallel", "arbitrary"), vmem_mib),
        name=name,
    )(*args)


def _mla_kernel(*refs, n_lat, n_ctx):
    if n_lat:
        q_ref, knl_ref, vl_ref, krl_ref, knc_ref, vc_ref, krc_ref, o_ref, k_scr, vt_scr = refs
    else:
        q_ref, knc_ref, vc_ref, krc_ref, o_ref, k_scr, vt_scr = refs

    @pl.when(pl.program_id(2) == 0)
    def _():
        if n_lat:
            k_scr[0:n_lat, 0:LANES] = knl_ref[...]
            k_scr[0:n_lat, LANES:] = krl_ref[...].astype(BF16)
            vt_scr[0:MLA_V, 0:n_lat] = vl_ref[...].astype(F32).T.astype(BF16)
        k_scr[n_lat:, 0:LANES] = knc_ref[...]
        k_scr[n_lat:, LANES:] = krc_ref[...].astype(BF16)
        vt_scr[0:MLA_V, n_lat:] = vc_ref[...].astype(F32).T.astype(BF16)
        vt_scr[MLA_V:, :] = jnp.ones((ONES_ROWS, n_lat + n_ctx), BF16)

    n_sub = min(MLA_Q_SUB, q_ref.shape[0])
    subs = [slice(r, r + n_sub) for r in range(0, q_ref.shape[0], n_sub)]
    def scores(i):
        return _dot_nt(k_scr[...], q_ref[subs[i], :])

    def weights(s):
        return jnp.exp2(s - jnp.max(s, axis=0, keepdims=True)).astype(BF16)

    n = len(subs)
    s = [scores(i) if i < 2 else None for i in range(n)]
    p = weights(s[0])
    for i, rows in enumerate(subs):
        if i + 2 < n:
            s[i + 2] = scores(i + 2)
        o = _dot(vt_scr[...], p)
        if i + 1 < n:
            p = weights(s[i + 1])
        o_ref[rows, :] = (o[0:MLA_V] / o[MLA_V:MLA_V + 1]).T.astype(o_ref.dtype)


def _mla_attention(q, kv_c, z_c, kr_col, *, batch, n_q, n_ctx, kv_l=None, z_l=None, tq=256):
    n_lat = 0 if kv_l is None else kv_l.shape[0] // batch
    nq_blocks = n_q // tq
    in_specs = [pl.BlockSpec((tq, MLA_QK_PAD), lambda b, h, i: (b * nq_blocks + i, h))]
    args = [q]
    if n_lat:
        in_specs += [
            pl.BlockSpec((n_lat, LANES), lambda b, h, i: (b, 2 * h)),
            pl.BlockSpec((n_lat, LANES), lambda b, h, i: (b, 2 * h + 1)),
            pl.BlockSpec((n_lat, LANES), lambda b, h, i: (b, kr_col)),
        ]
        args += [kv_l, kv_l, z_l]
    in_specs += [
        pl.BlockSpec((n_ctx, LANES), lambda b, h, i: (b, 2 * h)),
        pl.BlockSpec((n_ctx, LANES), lambda b, h, i: (b, 2 * h + 1)),
        pl.BlockSpec((n_ctx, LANES), lambda b, h, i: (b, kr_col)),
    ]
    args += [kv_c, kv_c, z_c]
    n_keys = n_lat + n_ctx
    return pl.pallas_call(
        functools.partial(_mla_kernel, n_lat=n_lat, n_ctx=n_ctx),
        out_shape=jax.ShapeDtypeStruct((batch * n_q, MLA_HEADS * MLA_V), BF16),
        grid=(batch, MLA_HEADS, nq_blocks),
        in_specs=in_specs,
        out_specs=pl.BlockSpec((tq, MLA_V), lambda b, h, i: (b * nq_blocks + i, h)),
        scratch_shapes=[pltpu.VMEM((n_keys, MLA_QK_PAD), BF16),
                        pltpu.VMEM((MLA_V + ONES_ROWS, n_keys), BF16)],
        compiler_params=_params(("parallel", "parallel", "arbitrary"), 48),
        name="mla_attention",
    )(*args)


def _gelu(x):
    c = math.sqrt(2.0 / math.pi)
    return x * (0.5 + 0.5 * jnp.tanh(x * (c + (c * 0.044715) * (x * x))))


def _gmlp_kernel(u_ref, v_ref, w_ref, b_ref, o_ref):
    tiles = [(slice(n * GMLP_CHUNK, (n + 1) * GMLP_CHUNK), slice(g * GMLP_DIM, (g + 1) * GMLP_DIM), g)
             for n in range(u_ref.shape[0] // GMLP_CHUNK) for g in range(GMLP_GROUPS)]
    v = [_gelu(v_ref[rows, cols]) for rows, cols, _ in tiles]
    vc = [x - jnp.mean(x, axis=-1, keepdims=True) for x in v]
    var = [jnp.mean(x * x, axis=-1, keepdims=True) for x in vc]
    vn = [(x * lax.rsqrt(s + NORM_EPS)).astype(BF16) for x, s in zip(vc, var)]
    for (rows, cols, g), x in zip(tiles, vn):
        mixed = _dot(w_ref[g], x) + b_ref[g]
        o_ref[rows, cols] = (_gelu(u_ref[rows, cols]) * mixed).astype(o_ref.dtype)


def _gmlp(z, u_col, v_col, w_sp, b_sp, *, tm=256):
    t = z.shape[0]
    return pl.pallas_call(
        _gmlp_kernel,
        out_shape=jax.ShapeDtypeStruct((t, GMLP_WIDTH), BF16),
        grid=(t // tm,),
        in_specs=[
            pl.BlockSpec((tm, GMLP_WIDTH), lambda i: (i, u_col)),
            pl.BlockSpec((tm, GMLP_WIDTH), lambda i: (i, v_col)),
            pl.BlockSpec((GMLP_GROUPS, GMLP_CHUNK, GMLP_CHUNK), lambda i: (0, 0, 0)),
            pl.BlockSpec((GMLP_GROUPS, GMLP_CHUNK, 1), lambda i: (0, 0, 0)),
        ],
        out_specs=pl.BlockSpec((tm, GMLP_WIDTH), lambda i: (i, 0)),
        compiler_params=_params(("parallel",), 32),
        name="gmlp",
    )(z, z, w_sp, b_sp)


def _swa_kernel(sink_ref, q_ref, kl_ref, vl_ref, kc_ref, vc_ref, o_ref, *, tq, n_lat):
    start = pl.program_id(1) * tq
    span0 = pl.multiple_of(jnp.clip(start - SWA_WINDOW, 0, n_lat - SWA_SPAN), SWA_WINDOW)
    k_span = kl_ref[pl.ds(span0, SWA_SPAN), :]
    k_ctx = kc_ref[...]
    vt_span = vl_ref[pl.ds(span0, SWA_SPAN), :].astype(F32).T.astype(BF16)
    vt_ctx = vc_ref[...].astype(F32).T.astype(BF16)
    ones_span = jnp.ones((ONES_ROWS, vt_span.shape[1]), BF16)
    ones_ctx = jnp.ones((ONES_ROWS, vt_ctx.shape[1]), BF16)

    k_pos = span0 + lax.broadcasted_iota(jnp.int32, (SWA_SPAN, tq), 0)
    q_pos = start + lax.broadcasted_iota(jnp.int32, (SWA_SPAN, tq), 1)
    bias = jnp.where(jnp.abs(k_pos - q_pos) <= SWA_WINDOW, 0.0, NEG_INF).astype(F32)
    bias = jnp.concatenate([bias] * SWA_UNIT, axis=1)

    lane = lax.broadcasted_iota(jnp.int32, (tq, LANES), 1)
    first = (lane % SWA_HEAD_DIM) < (SWA_HEAD_DIM // 2)
    keep = [jnp.where(first, 1.0, 0.0).astype(BF16), jnp.where(first, 0.0, 1.0).astype(BF16)]

    units = [range(h0, h0 + SWA_UNIT) for h0 in range(0, SWA_HEADS, SWA_UNIT)]

    def scores(heads):
        g = heads[0] // SWA_GROUP
        k_cols = slice(g * LANES, (g + 1) * LANES)
        q = jnp.concatenate(
            [q_ref[:, (h // 2) * LANES:(h // 2 + 1) * LANES] * keep[h % 2] for h in heads], axis=0)
        return _dot_nt(k_span[:, k_cols], q) + bias, _dot_nt(k_ctx[:, k_cols], q)

    s_next = scores(units[0])
    for u, heads in enumerate(units):
        g = heads[0] // SWA_GROUP
        kv_cols = slice(g * SWA_HEAD_DIM, (g + 1) * SWA_HEAD_DIM)
        s_l, s_c = s_next
        if u + 1 < len(units):
            s_next = scores(units[u + 1])
        sink = jnp.concatenate([jnp.full((1, tq), sink_ref[h] * LOG2E, F32) for h in heads], axis=1)
        m = jnp.maximum(jnp.maximum(jnp.max(s_l, axis=0, keepdims=True),
                                    jnp.max(s_c, axis=0, keepdims=True)), sink)
        p_l = jnp.exp2(s_l - m).astype(BF16)
        p_c = jnp.exp2(s_c - m).astype(BF16)
        vt_l = jnp.concatenate([vt_span[kv_cols, :], ones_span], axis=0)
        vt_c = jnp.concatenate([vt_ctx[kv_cols, :], ones_ctx], axis=0)
        o = _dot(vt_l, p_l) + _dot(vt_c, p_c)
        dim = SWA_HEAD_DIM
        o = o[0:dim] / (o[dim:dim + 1] + jnp.exp2(sink - m))
        for pair in range(SWA_UNIT // 2):
            cols = [slice((2 * pair + r) * tq, (2 * pair + r + 1) * tq) for r in range(2)]
            both = jnp.concatenate([o[:, cols[0]], o[:, cols[1]]], axis=0)
            group = heads[0] // 2 + pair
            o_ref[:, group * LANES:(group + 1) * LANES] = both.T.astype(o_ref.dtype)


def _swa_attention(z_l, z_c, sinks, *, batch, n_lat, n_ctx, tq=SWA_BLOCK):
    nq_blocks = n_lat // tq
    k_cols = 2 * SWA_KV_COLS
    return pl.pallas_call(
        functools.partial(_swa_kernel, tq=tq, n_lat=n_lat),
        out_shape=jax.ShapeDtypeStruct((batch * n_lat, SWA_Q_COLS), BF16),
        grid=(batch, nq_blocks),
        in_specs=[
            pl.BlockSpec(memory_space=pltpu.SMEM),
            pl.BlockSpec((tq, SWA_Q_COLS), lambda b, i: (b * nq_blocks + i, 0)),
            pl.BlockSpec((n_lat, k_cols), lambda b, i: (b, SWA_Q_COLS // k_cols)),
            pl.BlockSpec((n_lat, SWA_KV_COLS), lambda b, i: (b, (SWA_Q_COLS + k_cols) // SWA_KV_COLS)),
            pl.BlockSpec((n_ctx, k_cols), lambda b, i: (b, 0)),
            pl.BlockSpec((n_ctx, SWA_KV_COLS), lambda b, i: (b, k_cols // SWA_KV_COLS)),
        ],
        out_specs=pl.BlockSpec((tq, SWA_Q_COLS), lambda b, i: (b * nq_blocks + i, 0)),
        compiler_params=_params(("parallel", "arbitrary"), 48),
        name="swa_attention",
    )(sinks, z_l, z_l, z_l, z_c, z_c)


def _outproj_kernel(a1_ref, a2_ref, w1_ref, w2_ref, x_ref, gate_ref, o_ref):
    y = _dot(a1_ref[...], w1_ref[...]) + _dot(a2_ref[...], w2_ref[...])
    o_ref[...] = x_ref[...] + gate_ref[...] * y


def _outproj(a1, a1_col, a2, a2_col, w, x, mods, row_of, k_gate, *, tm=512):
    t, d = x.shape
    half = w.shape[0] // 2
    return pl.pallas_call(
        _outproj_kernel,
        out_shape=jax.ShapeDtypeStruct((t, d), F32),
        grid=(t // tm,),
        in_specs=[
            pl.BlockSpec((tm, half), lambda i: (i, a1_col)),
            pl.BlockSpec((tm, half), lambda i: (i, a2_col)),
            pl.BlockSpec((half, d), lambda i: (0, 0)),
            pl.BlockSpec((half, d), lambda i: (1, 0)),
            pl.BlockSpec((tm, d), lambda i: (i, 0)),
            pl.BlockSpec((None, 1, d), lambda i: (row_of(i), 0, k_gate)),
        ],
        out_specs=pl.BlockSpec((tm, d), lambda i: (i, 0)),
        compiler_params=_params(("parallel",), 48),
        name="outproj",
    )(a1, a2, w, w, x, mods)


def _mlp_kernel(*refs, final):
    if final:
        x_ref, g_ref, sh_ref, sc_ref, gate_ref, w1_ref, w2_ref, fg_ref, o_ref, h_ref, r_ref = refs
    else:
        x_ref, g_ref, sh_ref, sc_ref, gate_ref, w1_ref, w2_ref, o_ref, h_ref, r_ref = refs
    k = pl.program_id(1)

    def ff_chunk():
        a = jnp.square(jnp.maximum(_dot(h_ref[...], w1_ref[...]), 0.0)).astype(BF16)
        return _dot(a, w2_ref[...])

    @pl.when(k == 0)
    def _():
        def store(sl, y):
            h_ref[sl, :] = y.astype(BF16)

        _norm_modulate(x_ref, r_ref, store, g_ref[...], sc_ref[...], sh_ref[...])
        o_ref[...] = ff_chunk()

    @pl.when(k > 0)
    def _():
        o_ref[...] += ff_chunk()

    @pl.when(k == pl.num_programs(1) - 1)
    def _():
        def rows(sl):
            out = x_ref[sl, :] + gate_ref[...] * o_ref[sl, :]
            if final:
                out = _rms(out, fg_ref[...])
            o_ref[sl, :] = out

        _for_row_chunks(x_ref.shape[0], rows)


def _mlp(x, g, mods, row_of, w1, w2, layer, *, final_g=None, tm=512, tf=1024, vmem_mib=56):
    t, d = x.shape
    f = w1.shape[2]
    in_specs = [
        pl.BlockSpec((tm, d), lambda i, k: (i, 0)),
        pl.BlockSpec((1, d), lambda i, k: (0, 0)),
        pl.BlockSpec((None, 1, d), lambda i, k: (row_of(i), 0, 3)),
        pl.BlockSpec((None, 1, d), lambda i, k: (row_of(i), 0, 4)),
        pl.BlockSpec((None, 1, d), lambda i, k: (row_of(i), 0, 5)),
        pl.BlockSpec((None, d, tf), lambda i, k: (layer, 0, k)),
        pl.BlockSpec((None, tf, d), lambda i, k: (layer, k, 0)),
    ]
    args = [x, g.reshape(1, d), mods, mods, mods, w1, w2]
    if final_g is not None:
        in_specs.append(pl.BlockSpec((1, d), lambda i, k: (0, 0)))
        args.append(final_g.reshape(1, d))
    return pl.pallas_call(
        functools.partial(_mlp_kernel, final=final_g is not None),
        out_shape=jax.ShapeDtypeStruct((t, d), F32),
        grid=(t // tm, f // tf),
        in_specs=in_specs,
        out_specs=pl.BlockSpec((tm, d), lambda i, k: (i, 0)),
        scratch_shapes=[pltpu.VMEM((tm, d), BF16), pltpu.VMEM((tm, 1), F32)],
        compiler_params=_params(("parallel", "arbitrary"), vmem_mib),
        name="mlp",
    )(*args)


def _axial_angles(n_tokens, rot_dim):
    t = jnp.arange(n_tokens)
    row = (t // GRID_W).astype(F32)
    col = (t % GRID_W).astype(F32)
    n_freq = rot_dim // 4
    inv_freq = ROPE_BASE ** (-jnp.arange(n_freq, dtype=F32) / n_freq)
    ang = jnp.concatenate([row[:, None] * inv_freq, col[:, None] * inv_freq], axis=-1)
    return jnp.cos(ang), jnp.sin(ang)


def _swap_halves(w):
    half = w.shape[-1] // 2
    return jnp.concatenate([w[..., half:], w[..., :half]], axis=-1)


def kernel(x, c, ctx, c_ctx, norm1_g, w_mod, b_mod, norm2_g, w_ff1, w_ff2, even_w_in, mla_q_norm_g,
           mla_w_uq, mla_kv_norm_g, mla_w_ukv, gmlp_w_sp, gmlp_b_sp, even_w_out, odd_w_in, swa_sinks,
           odd_w_out, final_norm_g):
    batch, n_lat, d = x.shape
    n_ctx = ctx.shape[1]
    assert batch + 1 <= MOD_ROWS
    xl = x.reshape(batch * n_lat, d)
    xc = ctx.reshape(batch * n_ctx, d)

    tm = 1024

    def lat_row(block_rows):
        per_sample = n_lat // block_rows
        return lambda i: i // per_sample

    ctx_row = lambda i: batch

    cvec = jnp.concatenate([c, c_ctx[None, :], jnp.zeros((MOD_ROWS - batch - 1, d), F32)], axis=0)
    mods = _modulation(cvec, w_mod, b_mod)
    mods = mods.reshape(mods.shape[0], MOD_ROWS, 1, N_MOD * d)

    cos_m, sin_m = _axial_angles(n_lat, MLA_ROPE)
    zeros64 = jnp.zeros((n_lat, LANES // 2), F32)
    fold_cos = jnp.concatenate([cos_m, cos_m, zeros64], axis=-1)
    fold_sin = jnp.concatenate([-sin_m, sin_m, zeros64], axis=-1)
    keep = jnp.concatenate([jnp.ones((tm, LANES // 2), F32), jnp.zeros((tm, LANES // 2), F32)], axis=-1)
    drop = jnp.zeros((tm, LANES), F32)

    w_in = even_w_in[0]
    cq_w, ckv_w = w_in[:, :MLA_LORA], w_in[:, MLA_LORA:2 * MLA_LORA]
    kr_w = w_in[:, 2 * MLA_LORA:2 * MLA_LORA + MLA_ROPE]
    gm_w = w_in[:, 2 * MLA_LORA + MLA_ROPE:]
    w_in0 = jnp.concatenate([cq_w, ckv_w, gm_w, kr_w, _swap_halves(kr_w)], axis=-1).astype(BF16)
    n_in0 = w_in0.shape[1]
    kr_group = (n_in0 - LANES) // LANES
    kr_rope = (None, (kr_group,))

    wq = mla_w_uq[0].reshape(MLA_LORA, MLA_HEADS, MLA_NOPE + MLA_ROPE)
    wq_rope = wq[..., MLA_NOPE:]
    w_uq = jnp.concatenate([wq[..., :MLA_NOPE], wq_rope, _swap_halves(wq_rope)], axis=-1)
    w_uq = (w_uq * (MLA_SCALE * LOG2E)).reshape(MLA_LORA, MLA_HEADS * MLA_QK_PAD).astype(BF16)
    q_rope = (None, tuple(range(1, 2 * MLA_HEADS, 2)))
    w_ukv = mla_w_ukv[0].astype(BF16)
    w_sp = gmlp_w_sp[0].astype(BF16)
    b_sp = gmlp_b_sp[0][:, :, None]
    w_out0 = even_w_out[0].astype(BF16)
    w1_all, w2_all = w_ff1.astype(BF16), w_ff2.astype(BF16)
    tm_i = 512

    def layer0_tokens(xs, tabs, latent):
        row_of = (lambda rows: lat_row(rows)) if latent else (lambda rows: ctx_row)
        pos_blocks = (lambda rows: n_lat // rows) if latent else (lambda rows: 1)
        m0 = mods[0]
        z = _proj(xs, 0, d, norm1_g[0], w_in0, tm=tm_i, tn=n_in0, out_dtype=F32,
                  mod=(m0, row_of(tm_i), 0, 1), rope=kr_rope, tabs=tabs, pos_blocks=pos_blocks(tm_i),
                  vmem_mib=56, name="even_in_proj")
        q = _proj(z, 0, MLA_LORA, mla_q_norm_g[0], w_uq, tm=tm, tn=w_uq.shape[1], out_dtype=BF16,
                  rope=q_rope, tabs=tabs, pos_blocks=pos_blocks(tm), name="mla_q_proj")
        kv = _proj(z, 1, MLA_LORA, mla_kv_norm_g[0], w_ukv, tm=tm, tn=w_ukv.shape[1], out_dtype=BF16,
                   name="mla_kv_proj")
        gm = _gmlp(z, 1, 2, w_sp, b_sp)
        return z, q, kv, gm

    z_l, q_l, kv_l, gm_l = layer0_tokens(xl, (fold_cos, fold_sin), True)
    z_c, q_c, kv_c, gm_c = layer0_tokens(xc, (keep, drop), False)
    att_l = _mla_attention(q_l, kv_c, z_c, kr_group, batch=batch, n_q=n_lat, n_ctx=n_ctx,
                           kv_l=kv_l, z_l=z_l, tq=n_lat)
    att_c = _mla_attention(q_c, kv_c, z_c, kr_group, batch=batch, n_q=n_ctx, n_ctx=n_ctx)

    tm_o = 512
    xl = _outproj(att_l, 0, gm_l, 0, w_out0, xl, mods[0], lat_row(tm_o), 2, tm=tm_o)
    xc = _outproj(att_c, 0, gm_c, 0, w_out0, xc, mods[0], ctx_row, 2, tm=tm_o)
    tm_f = 512
    xl = _mlp(xl, norm2_g[0], mods[0], lat_row(tm_f), w1_all, w2_all, 0, tm=tm_f)
    xc = _mlp(xc, norm2_g[0], mods[0], ctx_row, w1_all, w2_all, 0, tm=tm_f)

    cos_s, sin_s = _axial_angles(n_lat, SWA_HEAD_DIM)
    pair_cos = jnp.concatenate([cos_s] * 4, axis=-1)
    pair_sin = jnp.concatenate([-sin_s, -sin_s, sin_s, sin_s], axis=-1)
    half = SWA_HEAD_DIM // 2
    w_in = odd_w_in[0]
    wq = (w_in[:, :SWA_Q_COLS] * (SWA_SCALE * LOG2E)).reshape(d, SWA_HEADS // 2, 2, 2, half)
    wq = wq.transpose(0, 1, 3, 2, 4).reshape(d, SWA_Q_COLS)
    wk = w_in[:, SWA_Q_COLS:SWA_Q_COLS + SWA_KV_COLS].reshape(d, SWA_KV_HEADS, 2, 1, half)
    wk = jnp.broadcast_to(wk, (d, SWA_KV_HEADS, 2, 2, half)).reshape(d, 2 * SWA_KV_COLS)
    wv = w_in[:, SWA_Q_COLS + SWA_KV_COLS:]
    w_in1 = jnp.concatenate([wq, wk, wv], axis=-1).astype(BF16)
    w_kv1 = jnp.concatenate([wk, wv], axis=-1).astype(BF16)
    rot_groups = (SWA_Q_COLS + 2 * SWA_KV_COLS) // LANES
    m1 = mods[1]
    z1_l = _proj(xl, 0, d, norm1_g[1], w_in1, tm=tm_i, tn=w_in1.shape[1], out_dtype=BF16,
                 mod=(m1, lat_row(tm_i), 0, 1), rope=(None, tuple(range(rot_groups))),
                 tabs=(pair_cos, pair_sin), pos_blocks=n_lat // tm_i, vmem_mib=56, name="odd_in_proj")
    z1_c = _proj(xc, 0, d, norm1_g[1], w_kv1, tm=tm, tn=w_kv1.shape[1], out_dtype=BF16,
                 mod=(m1, ctx_row, 0, 1), name="odd_ctx_kv_proj")
    att = _swa_attention(z1_l, z1_c, swa_sinks[0], batch=batch, n_lat=n_lat, n_ctx=n_ctx)
    xl = _outproj(att, 0, att, 1, odd_w_out[0].astype(BF16), xl, m1, lat_row(tm_o), 2, tm=tm_o)
    out = _mlp(xl, norm2_g[1], m1, lat_row(tm_f), w1_all, w2_all, 1, final_g=final_norm_g, tm=tm_f)
    return out.reshape(batch, n_lat, d)
```

```python
import functools
import math

import jax
import jax.numpy as jnp
from jax import lax
from jax.experimental import pallas as pl
from jax.experimental.pallas import tpu as pltpu

F32 = jnp.float32
BF16 = jnp.bfloat16

LANES = 128
MIB = 1 << 20

GRID_W = 64
N_MOD = 6
NORM_EPS = 1e-6
ROPE_BASE = 10000.0
NEG_INF = -1e30

MLA_HEADS = 8
MLA_LORA = 512
MLA_NOPE = 128
MLA_ROPE = 64
MLA_V = 128
MLA_QK_PAD = 2 * LANES
MLA_SCALE = 1.0 / math.sqrt(MLA_NOPE + MLA_ROPE)
MLA_Q_SUB = 512

GMLP_GROUPS = 8
GMLP_DIM = 128
GMLP_CHUNK = 128
GMLP_WIDTH = GMLP_GROUPS * GMLP_DIM

SWA_HEADS = 32
SWA_KV_HEADS = 4
SWA_GROUP = SWA_HEADS // SWA_KV_HEADS
SWA_HEAD_DIM = 64
SWA_WINDOW = 128
SWA_BLOCK = 128
SWA_SPAN = SWA_BLOCK + 2 * SWA_WINDOW
SWA_SCALE = 1.0 / math.sqrt(SWA_HEAD_DIM)
SWA_Q_COLS = SWA_HEADS * SWA_HEAD_DIM
SWA_KV_COLS = SWA_KV_HEADS * SWA_HEAD_DIM
SWA_UNIT = SWA_GROUP

MOD_ROWS = 16
ROW_CHUNK = 128
STATS_CHUNK = 256
NARROW_ROW = 512
ONES_ROWS = 16
LOG2E = math.log2(math.e)


def _params(semantics, vmem_mib):
    return pltpu.CompilerParams(dimension_semantics=semantics, vmem_limit_bytes=vmem_mib * MIB)


def _rms(x, g):
    return x * lax.rsqrt(jnp.mean(x * x, axis=-1, keepdims=True) + NORM_EPS) * g


def _for_row_chunks(n_rows, fn, rows=ROW_CHUNK):
    rows = min(rows, n_rows)

    def body(r, carry):
        fn(pl.ds(pl.multiple_of(r * rows, rows), rows))
        return carry

    lax.fori_loop(0, n_rows // rows, body, 0)


def _norm_modulate(x_ref, r_ref, store, g, scale=None, shift=None):
    n_rows = x_ref.shape[0]
    gain = g if scale is None else g * (1.0 + scale)

    if x_ref.shape[1] <= NARROW_ROW:
        def whole(sl):
            y = _rms(x_ref[sl, :], gain)
            store(sl, y if shift is None else y + shift)

        _for_row_chunks(n_rows, whole)
        return

    def stats(sl):
        x = x_ref[sl, :]
        r_ref[sl, :] = lax.rsqrt(jnp.mean(x * x, axis=-1, keepdims=True) + NORM_EPS)

    _for_row_chunks(n_rows, stats, rows=STATS_CHUNK)

    def apply(sl):
        y = x_ref[sl, :] * r_ref[sl, :] * gain
        store(sl, y if shift is None else y + shift)

    _for_row_chunks(n_rows, apply)


def _dot(a, b):
    return jnp.dot(a, b, preferred_element_type=F32)


def _dot_nt(a, b):
    return lax.dot_general(a, b, (((1,), (1,)), ((), ())), preferred_element_type=F32)


def _mod_kernel(c_ref, w_ref, b_ref, o_ref):
    c = c_ref[...]
    s = (c * jax.nn.sigmoid(c)).astype(BF16)
    o_ref[...] = _dot(s, w_ref[...].astype(BF16)) + b_ref[...]


def _modulation(cvec, w_mod, b_mod, *, tn=1024):
    depth, d, n = w_mod.shape
    return pl.pallas_call(
        _mod_kernel,
        out_shape=jax.ShapeDtypeStruct((depth, MOD_ROWS, n), F32),
        grid=(depth, n // tn),
        in_specs=[
            pl.BlockSpec((MOD_ROWS, d), lambda l, j: (0, 0)),
            pl.BlockSpec((None, d, tn), lambda l, j: (l, 0, j)),
            pl.BlockSpec((None, 1, tn), lambda l, j: (l, 0, j)),
        ],
        out_specs=pl.BlockSpec((None, MOD_ROWS, tn), lambda l, j: (l, 0, j)),
        compiler_params=_params(("parallel", "parallel"), 40),
        name="modulation",
    )(cvec, w_mod, b_mod.reshape(depth, 1, n))


def _proj_kernel(*refs, modulated, rope):
    it = iter(refs)
    x_ref, g_ref = next(it), next(it)
    sh_ref = sc_ref = None
    if modulated:
        sh_ref, sc_ref = next(it), next(it)
    w_ref = next(it)
    tabs = [next(it) for _ in range(2 if rope else 0)]
    o_ref, h_ref, r_ref = next(it), next(it), next(it)
    j = pl.program_id(1)

    @pl.when(j == 0)
    def _():
        def store(sl, y):
            h_ref[sl, :] = y.astype(BF16)

        if modulated:
            _norm_modulate(x_ref, r_ref, store, g_ref[...], sc_ref[...], sh_ref[...])
        else:
            _norm_modulate(x_ref, r_ref, store, g_ref[...])

    z = _dot(h_ref[...], w_ref[...])
    groups_per_tile = z.shape[1] // LANES

    if rope is None:
        o_ref[...] = z.astype(o_ref.dtype)
        return

    tile, groups = rope

    def rotated(t):
        return t * tabs[0][...] + pltpu.roll(t, LANES // 2, 1) * tabs[1][...]

    if tile is None:
        for gi in range(groups_per_tile):
            sl = slice(gi * LANES, (gi + 1) * LANES)
            t = z[:, sl]
            o_ref[:, sl] = (rotated(t) if gi in groups else t).astype(o_ref.dtype)
    else:
        o_ref[...] = z.astype(o_ref.dtype)

        @pl.when(j == tile)
        def _():
            for gi in groups:
                sl = slice(gi * LANES, (gi + 1) * LANES)
                o_ref[:, sl] = rotated(z[:, sl]).astype(o_ref.dtype)


def _proj(x, xcol, k, g, w, *, tm, tn, out_dtype, mod=None, rope=None, tabs=(), pos_blocks=1,
          vmem_mib=48, name="proj"):
    t = x.shape[0]
    n = w.shape[1]
    assert t % tm == 0 and n % tn == 0 and w.shape[0] == k
    in_specs = [
        pl.BlockSpec((tm, k), lambda i, j: (i, xcol)),
        pl.BlockSpec((1, k), lambda i, j: (0, 0)),
    ]
    args = [x, g.reshape(1, k)]
    if mod is not None:
        mods, row_of, k_shift, k_scale = mod
        in_specs += [
            pl.BlockSpec((None, 1, k), lambda i, j: (row_of(i), 0, k_shift)),
            pl.BlockSpec((None, 1, k), lambda i, j: (row_of(i), 0, k_scale)),
        ]
        args += [mods, mods]
    w_mode = {"pipeline_mode": pl.Buffered(1)} if n == tn else {}
    in_specs.append(pl.BlockSpec((k, tn), lambda i, j: (0, j), **w_mode))
    args.append(w)
    for tab in tabs:
        in_specs.append(pl.BlockSpec((tm, LANES), lambda i, j: (i % pos_blocks, 0)))
        args.append(tab)
    return pl.pallas_call(
        functools.partial(_proj_kernel, modulated=mod is not None, rope=rope),
        out_shape=jax.ShapeDtypeStruct((t, n), out_dtype),
        grid=(t // tm, n // tn),
        in_specs=in_specs,
        out_specs=pl.BlockSpec((tm, tn), lambda i, j: (i, j)),
        scratch_shapes=[pltpu.VMEM((tm, k), BF16), pltpu.VMEM((tm, 1), F32)],
        compiler_params=_params(("parallel", "arbitrary"), vmem_mib),
        name=name,
    )(*args)


def _mla_kernel(*refs, n_lat, n_ctx):
    if n_lat:
        q_ref, knl_ref, vl_ref, krl_ref, knc_ref, vc_ref, krc_ref, o_ref, k_scr, vt_scr = refs
    else:
        q_ref, knc_ref, vc_ref, krc_ref, o_ref, k_scr, vt_scr = refs

    @pl.when(pl.program_id(2) == 0)
    def _():
        if n_lat:
            k_scr[0:n_lat, 0:LANES] = knl_ref[...]
            k_scr[0:n_lat, LANES:] = krl_ref[...].astype(BF16)
            vt_scr[0:MLA_V, 0:n_lat] = vl_ref[...].astype(F32).T.astype(BF16)
        k_scr[n_lat:, 0:LANES] = knc_ref[...]
        k_scr[n_lat:, LANES:] = krc_ref[...].astype(BF16)
        vt_scr[0:MLA_V, n_lat:] = vc_ref[...].astype(F32).T.astype(BF16)
        vt_scr[MLA_V:, :] = jnp.ones((ONES_ROWS, n_lat + n_ctx), BF16)

    n_sub = min(MLA_Q_SUB, q_ref.shape[0])
    subs = [slice(r, r + n_sub) for r in range(0, q_ref.shape[0], n_sub)]
    def scores(i):
        return _dot_nt(k_scr[...], q_ref[subs[i], :])

    def weights(s):
        return jnp.exp2(s - jnp.max(s, axis=0, keepdims=True)).astype(BF16)

    n = len(subs)
    s = [scores(i) if i < 2 else None for i in range(n)]
    p = weights(s[0])
    for i, rows in enumerate(subs):
        if i + 2 < n:
            s[i + 2] = scores(i + 2)
        o = _dot(vt_scr[...], p)
        if i + 1 < n:
            p = weights(s[i + 1])
        o_ref[rows, :] = (o[0:MLA_V] / o[MLA_V:MLA_V + 1]).T.astype(o_ref.dtype)


def _mla_attention(q, kv_c, z_c, kr_col, *, batch, n_q, n_ctx, kv_l=None, z_l=None, tq=256):
    n_lat = 0 if kv_l is None else kv_l.shape[0] // batch
    nq_blocks = n_q // tq
    in_specs = [pl.BlockSpec((tq, MLA_QK_PAD), lambda b, h, i: (b * nq_blocks + i, h))]
    args = [q]
    if n_lat:
        in_specs += [
            pl.BlockSpec((n_lat, LANES), lambda b, h, i: (b, 2 * h)),
            pl.BlockSpec((n_lat, LANES), lambda b, h, i: (b, 2 * h + 1)),
            pl.BlockSpec((n_lat, LANES), lambda b, h, i: (b, kr_col)),
        ]
        args += [kv_l, kv_l, z_l]
    in_specs += [
        pl.BlockSpec((n_ctx, LANES), lambda b, h, i: (b, 2 * h)),
        pl.BlockSpec((n_ctx, LANES), lambda b, h, i: (b, 2 * h + 1)),
        pl.BlockSpec((n_ctx, LANES), lambda b, h, i: (b, kr_col)),
    ]
    args += [kv_c, kv_c, z_c]
    n_keys = n_lat + n_ctx
    return pl.pallas_call(
        functools.partial(_mla_kernel, n_lat=n_lat, n_ctx=n_ctx),
        out_shape=jax.ShapeDtypeStruct((batch * n_q, MLA_HEADS * MLA_V), BF16),
        grid=(batch, MLA_HEADS, nq_blocks),
        in_specs=in_specs,
        out_specs=pl.BlockSpec((tq, MLA_V), lambda b, h, i: (b * nq_blocks + i, h)),
        scratch_shapes=[pltpu.VMEM((n_keys, MLA_QK_PAD), BF16),
                        pltpu.VMEM((MLA_V + ONES_ROWS, n_keys), BF16)],
        compiler_params=_params(("parallel", "parallel", "arbitrary"), 48),
        name="mla_attention",
    )(*args)


def _gelu(x):
    c = math.sqrt(2.0 / math.pi)
    return x * (0.5 + 0.5 * jnp.tanh(x * (c + (c * 0.044715) * (x * x))))


def _gmlp_kernel(u_ref, v_ref, w_ref, b_ref, o_ref):
    tiles = [(slice(n * GMLP_CHUNK, (n + 1) * GMLP_CHUNK), slice(g * GMLP_DIM, (g + 1) * GMLP_DIM), g)
             for n in range(u_ref.shape[0] // GMLP_CHUNK) for g in range(GMLP_GROUPS)]
    v = [_gelu(v_ref[rows, cols]) for rows, cols, _ in tiles]
    vc = [x - jnp.mean(x, axis=-1, keepdims=True) for x in v]
    var = [jnp.mean(x * x, axis=-1, keepdims=True) for x in vc]
    vn = [(x * lax.rsqrt(s + NORM_EPS)).astype(BF16) for x, s in zip(vc, var)]
    for (rows, cols, g), x in zip(tiles, vn):
        mixed = _dot(w_ref[g], x) + b_ref[g]
        o_ref[rows, cols] = (_gelu(u_ref[rows, cols]) * mixed).astype(o_ref.dtype)


def _gmlp(z, u_col, v_col, w_sp, b_sp, *, tm=256):
    t = z.shape[0]
    return pl.pallas_call(
        _gmlp_kernel,
        out_shape=jax.ShapeDtypeStruct((t, GMLP_WIDTH), BF16),
        grid=(t // tm,),
        in_specs=[
            pl.BlockSpec((tm, GMLP_WIDTH), lambda i: (i, u_col)),
            pl.BlockSpec((tm, GMLP_WIDTH), lambda i: (i, v_col)),
            pl.BlockSpec((GMLP_GROUPS, GMLP_CHUNK, GMLP_CHUNK), lambda i: (0, 0, 0)),
            pl.BlockSpec((GMLP_GROUPS, GMLP_CHUNK, 1), lambda i: (0, 0, 0)),
        ],
        out_specs=pl.BlockSpec((tm, GMLP_WIDTH), lambda i: (i, 0)),
        compiler_params=_params(("parallel",), 32),
        name="gmlp",
    )(z, z, w_sp, b_sp)


def _swa_kernel(sink_ref, q_ref, kl_ref, vl_ref, kc_ref, vc_ref, o_ref, *, tq, n_lat):
    start = pl.program_id(1) * tq
    span0 = pl.multiple_of(jnp.clip(start - SWA_WINDOW, 0, n_lat - SWA_SPAN), SWA_WINDOW)
    k_span = kl_ref[pl.ds(span0, SWA_SPAN), :]
    k_ctx = kc_ref[...]
    vt_span = vl_ref[pl.ds(span0, SWA_SPAN), :].astype(F32).T.astype(BF16)
    vt_ctx = vc_ref[...].astype(F32).T.astype(BF16)
    ones_span = jnp.ones((ONES_ROWS, vt_span.shape[1]), BF16)
    ones_ctx = jnp.ones((ONES_ROWS, vt_ctx.shape[1]), BF16)

    k_pos = span0 + lax.broadcasted_iota(jnp.int32, (SWA_SPAN, tq), 0)
    q_pos = start + lax.broadcasted_iota(jnp.int32, (SWA_SPAN, tq), 1)
    bias = jnp.where(jnp.abs(k_pos - q_pos) <= SWA_WINDOW, 0.0, NEG_INF).astype(F32)
    bias = jnp.concatenate([bias] * SWA_UNIT, axis=1)

    lane = lax.broadcasted_iota(jnp.int32, (tq, LANES), 1)
    first = (lane % SWA_HEAD_DIM) < (SWA_HEAD_DIM // 2)
    keep = [jnp.where(first, 1.0, 0.0).astype(BF16), jnp.where(first, 0.0, 1.0).astype(BF16)]

    units = [range(h0, h0 + SWA_UNIT) for h0 in range(0, SWA_HEADS, SWA_UNIT)]

    def scores(heads):
        g = heads[0] // SWA_GROUP
        k_cols = slice(g * LANES, (g + 1) * LANES)
        q = jnp.concatenate(
            [q_ref[:, (h // 2) * LANES:(h // 2 + 1) * LANES] * keep[h % 2] for h in heads], axis=0)
        return _dot_nt(k_span[:, k_cols], q) + bias, _dot_nt(k_ctx[:, k_cols], q)

    s_next = scores(units[0])
    for u, heads in enumerate(units):
        g = heads[0] // SWA_GROUP
        kv_cols = slice(g * SWA_HEAD_DIM, (g + 1) * SWA_HEAD_DIM)
        s_l, s_c = s_next
        if u + 1 < len(units):
            s_next = scores(units[u + 1])
        sink = jnp.concatenate([jnp.full((1, tq), sink_ref[h] * LOG2E, F32) for h in heads], axis=1)
        m = jnp.maximum(jnp.maximum(jnp.max(s_l, axis=0, keepdims=True),
                                    jnp.max(s_c, axis=0, keepdims=True)), sink)
        p_l = jnp.exp2(s_l - m).astype(BF16)
        p_c = jnp.exp2(s_c - m).astype(BF16)
        vt_l = jnp.concatenate([vt_span[kv_cols, :], ones_span], axis=0)
        vt_c = jnp.concatenate([vt_ctx[kv_cols, :], ones_ctx], axis=0)
        o = _dot(vt_l, p_l) + _dot(vt_c, p_c)
        dim = SWA_HEAD_DIM
        o = o[0:dim] / (o[dim:dim + 1] + jnp.exp2(sink - m))
        for pair in range(SWA_UNIT // 2):
            cols = [slice((2 * pair + r) * tq, (2 * pair + r + 1) * tq) for r in range(2)]
            both = jnp.concatenate([o[:, cols[0]], o[:, cols[1]]], axis=0)
            group = heads[0] // 2 + pair
            o_ref[:, group * LANES:(group + 1) * LANES] = both.T.astype(o_ref.dtype)


def _swa_attention(z_l, z_c, sinks, *, batch, n_lat, n_ctx, tq=SWA_BLOCK):
    nq_blocks = n_lat // tq
    k_cols = 2 * SWA_KV_COLS
    return pl.pallas_call(
        functools.partial(_swa_kernel, tq=tq, n_lat=n_lat),
        out_shape=jax.ShapeDtypeStruct((batch * n_lat, SWA_Q_COLS), BF16),
        grid=(batch, nq_blocks),
        in_specs=[
            pl.BlockSpec(memory_space=pltpu.SMEM),
            pl.BlockSpec((tq, SWA_Q_COLS), lambda b, i: (b * nq_blocks + i, 0)),
            pl.BlockSpec((n_lat, k_cols), lambda b, i: (b, SWA_Q_COLS // k_cols)),
            pl.BlockSpec((n_lat, SWA_KV_COLS), lambda b, i: (b, (SWA_Q_COLS + k_cols) // SWA_KV_COLS)),
            pl.BlockSpec((n_ctx, k_cols), lambda b, i: (b, 0)),
            pl.BlockSpec((n_ctx, SWA_KV_COLS), lambda b, i: (b, k_cols // SWA_KV_COLS)),
        ],
        out_specs=pl.BlockSpec((tq, SWA_Q_COLS), lambda b, i: (b * nq_blocks + i, 0)),
        compiler_params=_params(("parallel", "arbitrary"), 48),
        name="swa_attention",
    )(sinks, z_l, z_l, z_l, z_c, z_c)


def _outproj_kernel(a1_ref, a2_ref, w1_ref, w2_ref, x_ref, gate_ref, o_ref):
    y = _dot(a1_ref[...], w1_ref[...]) + _dot(a2_ref[...], w2_ref[...])
    o_ref[...] = x_ref[...] + gate_ref[...] * y


def _outproj(a1, a1_col, a2, a2_col, w, x, mods, row_of, k_gate, *, tm=512):
    t, d = x.shape
    half = w.shape[0] // 2
    return pl.pallas_call(
        _outproj_kernel,
        out_shape=jax.ShapeDtypeStruct((t, d), F32),
        grid=(t // tm,),
        in_specs=[
            pl.BlockSpec((tm, half), lambda i: (i, a1_col)),
            pl.BlockSpec((tm, half), lambda i: (i, a2_col)),
            pl.BlockSpec((half, d), lambda i: (0, 0)),
            pl.BlockSpec((half, d), lambda i: (1, 0)),
            pl.BlockSpec((tm, d), lambda i: (i, 0)),
            pl.BlockSpec((None, 1, d), lambda i: (row_of(i), 0, k_gate)),
        ],
        out_specs=pl.BlockSpec((tm, d), lambda i: (i, 0)),
        compiler_params=_params(("parallel",), 48),
        name="outproj",
    )(a1, a2, w, w, x, mods)


def _mlp_kernel(*refs, final):
    if final:
        x_ref, g_ref, sh_ref, sc_ref, gate_ref, w1_ref, w2_ref, fg_ref, o_ref, h_ref, r_ref = refs
    else:
        x_ref, g_ref, sh_ref, sc_ref, gate_ref, w1_ref, w2_ref, o_ref, h_ref, r_ref = refs
    k = pl.program_id(1)

    def ff_chunk():
        a = jnp.square(jnp.maximum(_dot(h_ref[...], w1_ref[...]), 0.0)).astype(BF16)
        return _dot(a, w2_ref[...])

    @pl.when(k == 0)
    def _():
        def store(sl, y):
            h_ref[sl, :] = y.astype(BF16)

        _norm_modulate(x_ref, r_ref, store, g_ref[...], sc_ref[...], sh_ref[...])
        o_ref[...] = ff_chunk()

    @pl.when(k > 0)
    def _():
        o_ref[...] += ff_chunk()

    @pl.when(k == pl.num_programs(1) - 1)
    def _():
        def rows(sl):
            out = x_ref[sl, :] + gate_ref[...] * o_ref[sl, :]
            if final:
                out = _rms(out, fg_ref[...])
            o_ref[sl, :] = out

        _for_row_chunks(x_ref.shape[0], rows)


def _mlp(x, g, mods, row_of, w1, w2, layer, *, final_g=None, tm=1024, vmem_mib=58):
    t, d = x.shape
    n_k, tf = w1.shape[1], w1.shape[3]
    in_specs = [
        pl.BlockSpec((tm, d), lambda i, k: (i, 0)),
        pl.BlockSpec((1, d), lambda i, k: (0, 0)),
        pl.BlockSpec((None, 1, d), lambda i, k: (row_of(i), 0, 3)),
        pl.BlockSpec((None, 1, d), lambda i, k: (row_of(i), 0, 4)),
        pl.BlockSpec((None, 1, d), lambda i, k: (row_of(i), 0, 5)),
        pl.BlockSpec((None, None, d, tf), lambda i, k: (layer, k, 0, 0)),
        pl.BlockSpec((None, tf, d), lambda i, k: (layer, k, 0)),
    ]
    args = [x, g.reshape(1, d), mods, mods, mods, w1, w2]
    if final_g is not None:
        in_specs.append(pl.BlockSpec((1, d), lambda i, k: (0, 0)))
        args.append(final_g.reshape(1, d))
    return pl.pallas_call(
        functools.partial(_mlp_kernel, final=final_g is not None),
        out_shape=jax.ShapeDtypeStruct((t, d), F32),
        grid=(t // tm, n_k),
        in_specs=in_specs,
        out_specs=pl.BlockSpec((tm, d), lambda i, k: (i, 0)),
        scratch_shapes=[pltpu.VMEM((tm, d), BF16), pltpu.VMEM((tm, 1), F32)],
        compiler_params=_params(("parallel", "arbitrary"), vmem_mib),
        name="mlp",
    )(*args)


def _axial_angles(n_tokens, rot_dim):
    t = jnp.arange(n_tokens)
    row = (t // GRID_W).astype(F32)
    col = (t % GRID_W).astype(F32)
    n_freq = rot_dim // 4
    inv_freq = ROPE_BASE ** (-jnp.arange(n_freq, dtype=F32) / n_freq)
    ang = jnp.concatenate([row[:, None] * inv_freq, col[:, None] * inv_freq], axis=-1)
    return jnp.cos(ang), jnp.sin(ang)


def _swap_halves(w):
    half = w.shape[-1] // 2
    return jnp.concatenate([w[..., half:], w[..., :half]], axis=-1)


def kernel(x, c, ctx, c_ctx, norm1_g, w_mod, b_mod, norm2_g, w_ff1, w_ff2, even_w_in, mla_q_norm_g,
           mla_w_uq, mla_kv_norm_g, mla_w_ukv, gmlp_w_sp, gmlp_b_sp, even_w_out, odd_w_in, swa_sinks,
           odd_w_out, final_norm_g):
    batch, n_lat, d = x.shape
    n_ctx = ctx.shape[1]
    assert batch + 1 <= MOD_ROWS
    xl = x.reshape(batch * n_lat, d)
    xc = ctx.reshape(batch * n_ctx, d)

    tm = 1024

    def lat_row(block_rows):
        per_sample = n_lat // block_rows
        return lambda i: i // per_sample

    ctx_row = lambda i: batch

    cvec = jnp.concatenate([c, c_ctx[None, :], jnp.zeros((MOD_ROWS - batch - 1, d), F32)], axis=0)
    mods = _modulation(cvec, w_mod, b_mod)
    mods = mods.reshape(mods.shape[0], MOD_ROWS, 1, N_MOD * d)

    cos_m, sin_m = _axial_angles(n_lat, MLA_ROPE)
    zeros64 = jnp.zeros((n_lat, LANES // 2), F32)
    fold_cos = jnp.concatenate([cos_m, cos_m, zeros64], axis=-1)
    fold_sin = jnp.concatenate([-sin_m, sin_m, zeros64], axis=-1)
    keep = jnp.concatenate([jnp.ones((tm, LANES // 2), F32), jnp.zeros((tm, LANES // 2), F32)], axis=-1)
    drop = jnp.zeros((tm, LANES), F32)

    w_in = even_w_in[0]
    cq_w, ckv_w = w_in[:, :MLA_LORA], w_in[:, MLA_LORA:2 * MLA_LORA]
    kr_w = w_in[:, 2 * MLA_LORA:2 * MLA_LORA + MLA_ROPE]
    gm_w = w_in[:, 2 * MLA_LORA + MLA_ROPE:]
    w_in0 = jnp.concatenate([cq_w, ckv_w, gm_w, kr_w, _swap_halves(kr_w)], axis=-1).astype(BF16)
    n_in0 = w_in0.shape[1]
    kr_group = (n_in0 - LANES) // LANES
    kr_rope = (None, (kr_group,))

    wq = mla_w_uq[0].reshape(MLA_LORA, MLA_HEADS, MLA_NOPE + MLA_ROPE)
    wq_rope = wq[..., MLA_NOPE:]
    w_uq = jnp.concatenate([wq[..., :MLA_NOPE], wq_rope, _swap_halves(wq_rope)], axis=-1)
    w_uq = (w_uq * (MLA_SCALE * LOG2E)).reshape(MLA_LORA, MLA_HEADS * MLA_QK_PAD).astype(BF16)
    q_rope = (None, tuple(range(1, 2 * MLA_HEADS, 2)))
    w_ukv = mla_w_ukv[0].astype(BF16)
    w_sp = gmlp_w_sp[0].astype(BF16)
    b_sp = gmlp_b_sp[0][:, :, None]
    w_out0 = even_w_out[0].astype(BF16)
    tf = 512
    n_layers, _, d_ff = w_ff1.shape
    w1_all = w_ff1.astype(BF16).reshape(n_layers, d, d_ff // tf, tf).transpose(0, 2, 1, 3)
    w2_all = w_ff2.astype(BF16)
    tm_i = 512

    def layer0_tokens(xs, tabs, latent):
        row_of = (lambda rows: lat_row(rows)) if latent else (lambda rows: ctx_row)
        pos_blocks = (lambda rows: n_lat // rows) if latent else (lambda rows: 1)
        m0 = mods[0]
        z = _proj(xs, 0, d, norm1_g[0], w_in0, tm=tm_i, tn=n_in0, out_dtype=F32,
                  mod=(m0, row_of(tm_i), 0, 1), rope=kr_rope, tabs=tabs, pos_blocks=pos_blocks(tm_i),
                  vmem_mib=56, name="even_in_proj")
        q = _proj(z, 0, MLA_LORA, mla_q_norm_g[0], w_uq, tm=tm, tn=w_uq.shape[1], out_dtype=BF16,
                  rope=q_rope, tabs=tabs, pos_blocks=pos_blocks(tm), name="mla_q_proj")
        kv = _proj(z, 1, MLA_LORA, mla_kv_norm_g[0], w_ukv, tm=tm, tn=w_ukv.shape[1], out_dtype=BF16,
                   name="mla_kv_proj")
        gm = _gmlp(z, 1, 2, w_sp, b_sp)
        return z, q, kv, gm

    z_l, q_l, kv_l, gm_l = layer0_tokens(xl, (fold_cos, fold_sin), True)
    z_c, q_c, kv_c, gm_c = layer0_tokens(xc, (keep, drop), False)
    att_l = _mla_attention(q_l, kv_c, z_c, kr_group, batch=batch, n_q=n_lat, n_ctx=n_ctx,
                           kv_l=kv_l, z_l=z_l, tq=n_lat)
    att_c = _mla_attention(q_c, kv_c, z_c, kr_group, batch=batch, n_q=n_ctx, n_ctx=n_ctx)

    tm_o = 512
    xl = _outproj(att_l, 0, gm_l, 0, w_out0, xl, mods[0], lat_row(tm_o), 2, tm=tm_o)
    xc = _outproj(att_c, 0, gm_c, 0, w_out0, xc, mods[0], ctx_row, 2, tm=tm_o)
    tm_f = 1024
    xl = _mlp(xl, norm2_g[0], mods[0], lat_row(tm_f), w1_all, w2_all, 0, tm=tm_f)
    xc = _mlp(xc, norm2_g[0], mods[0], ctx_row, w1_all, w2_all, 0, tm=tm_f)

    cos_s, sin_s = _axial_angles(n_lat, SWA_HEAD_DIM)
    pair_cos = jnp.concatenate([cos_s] * 4, axis=-1)
    pair_sin = jnp.concatenate([-sin_s, -sin_s, sin_s, sin_s], axis=-1)
    half = SWA_HEAD_DIM // 2
    w_in = odd_w_in[0]
    wq = (w_in[:, :SWA_Q_COLS] * (SWA_SCALE * LOG2E)).reshape(d, SWA_HEADS // 2, 2, 2, half)
    wq = wq.transpose(0, 1, 3, 2, 4).reshape(d, SWA_Q_COLS)
    wk = w_in[:, SWA_Q_COLS:SWA_Q_COLS + SWA_KV_COLS].reshape(d, SWA_KV_HEADS, 2, 1, half)
    wk = jnp.broadcast_to(wk, (d, SWA_KV_HEADS, 2, 2, half)).reshape(d, 2 * SWA_KV_COLS)
    wv = w_in[:, SWA_Q_COLS + SWA_KV_COLS:]
    w_in1 = jnp.concatenate([wq, wk, wv], axis=-1).astype(BF16)
    w_kv1 = jnp.concatenate([wk, wv], axis=-1).astype(BF16)
    rot_groups = (SWA_Q_COLS + 2 * SWA_KV_COLS) // LANES
    m1 = mods[1]
    z1_l = _proj(xl, 0, d, norm1_g[1], w_in1, tm=tm_i, tn=w_in1.shape[1], out_dtype=BF16,
                 mod=(m1, lat_row(tm_i), 0, 1), rope=(None, tuple(range(rot_groups))),
                 tabs=(pair_cos, pair_sin), pos_blocks=n_lat // tm_i, vmem_mib=56, name="odd_in_proj")
    z1_c = _proj(xc, 0, d, norm1_g[1], w_kv1, tm=tm, tn=w_kv1.shape[1], out_dtype=BF16,
                 mod=(m1, ctx_row, 0, 1), name="odd_ctx_kv_proj")
    att = _swa_attention(z1_l, z1_c, swa_sinks[0], batch=batch, n_lat=n_lat, n_ctx=n_ctx)
    xl = _outproj(att, 0, att, 1, odd_w_out[0].astype(BF16), xl, m1, lat_row(tm_o), 2, tm=tm_o)
    out = _mlp(xl, norm2_g[1], m1, lat_row(tm_f), w1_all, w2_all, 1, final_g=final_norm_g, tm=tm_f)
    return out.reshape(batch, n_lat, d)
```

```python
import functools
import math

import jax
import jax.numpy as jnp
from jax import lax
from jax.experimental import pallas as pl
from jax.experimental.pallas import tpu as pltpu

F32 = jnp.float32
BF16 = jnp.bfloat16

LANES = 128
MIB = 1 << 20

GRID_W = 64
N_MOD = 6
NORM_EPS = 1e-6
ROPE_BASE = 10000.0
NEG_INF = -1e30

MLA_HEADS = 8
MLA_LORA = 512
MLA_NOPE = 128
MLA_ROPE = 64
MLA_V = 128
MLA_QK_PAD = 2 * LANES
MLA_SCALE = 1.0 / math.sqrt(MLA_NOPE + MLA_ROPE)
MLA_Q_SUB = 512

GMLP_GROUPS = 8
GMLP_DIM = 128
GMLP_CHUNK = 128
GMLP_WIDTH = GMLP_GROUPS * GMLP_DIM

SWA_HEADS = 32
SWA_KV_HEADS = 4
SWA_GROUP = SWA_HEADS // SWA_KV_HEADS
SWA_HEAD_DIM = 64
SWA_WINDOW = 128
SWA_BLOCK = 128
SWA_SPAN = SWA_BLOCK + 2 * SWA_WINDOW
SWA_SCALE = 1.0 / math.sqrt(SWA_HEAD_DIM)
SWA_Q_COLS = SWA_HEADS * SWA_HEAD_DIM
SWA_KV_COLS = SWA_KV_HEADS * SWA_HEAD_DIM
SWA_UNIT = SWA_GROUP

MOD_ROWS = 16
ROW_CHUNK = 128
STATS_CHUNK = 256
NARROW_ROW = 512
ONES_ROWS = 16
LOG2E = math.log2(math.e)


def _params(semantics, vmem_mib):
    return pltpu.CompilerParams(dimension_semantics=semantics, vmem_limit_bytes=vmem_mib * MIB)


def _rms(x, g):
    return x * lax.rsqrt(jnp.mean(x * x, axis=-1, keepdims=True) + NORM_EPS) * g


def _for_row_chunks(n_rows, fn, rows=ROW_CHUNK):
    rows = min(rows, n_rows)

    def body(r, carry):
        fn(pl.ds(pl.multiple_of(r * rows, rows), rows))
        return carry

    lax.fori_loop(0, n_rows // rows, body, 0)


def _norm_modulate(x_ref, r_ref, store, g, scale=None, shift=None):
    n_rows = x_ref.shape[0]
    gain = g if scale is None else g * (1.0 + scale)

    if x_ref.shape[1] <= NARROW_ROW:
        def whole(sl):
            y = _rms(x_ref[sl, :], gain)
            store(sl, y if shift is None else y + shift)

        _for_row_chunks(n_rows, whole)
        return

    def stats(sl):
        x = x_ref[sl, :]
        r_ref[sl, :] = lax.rsqrt(jnp.mean(x * x, axis=-1, keepdims=True) + NORM_EPS)

    _for_row_chunks(n_rows, stats, rows=STATS_CHUNK)

    def apply(sl):
        y = x_ref[sl, :] * r_ref[sl, :] * gain
        store(sl, y if shift is None else y + shift)

    _for_row_chunks(n_rows, apply)


def _dot(a, b):
    return jnp.dot(a, b, preferred_element_type=F32)


def _dot_nt(a, b):
    return lax.dot_general(a, b, (((1,), (1,)), ((), ())), preferred_element_type=F32)


def _mod_kernel(c_ref, w_ref, b_ref, o_ref):
    c = c_ref[...]
    s = (c * jax.nn.sigmoid(c)).astype(BF16)
    o_ref[...] = _dot(s, w_ref[...].astype(BF16)) + b_ref[...]


def _modulation(cvec, w_mod, b_mod, *, tn=1024):
    depth, d, n = w_mod.shape
    return pl.pallas_call(
        _mod_kernel,
        out_shape=jax.ShapeDtypeStruct((depth, MOD_ROWS, n), F32),
        grid=(depth, n // tn),
        in_specs=[
            pl.BlockSpec((MOD_ROWS, d), lambda l, j: (0, 0)),
            pl.BlockSpec((None, d, tn), lambda l, j: (l, 0, j)),
            pl.BlockSpec((None, 1, tn), lambda l, j: (l, 0, j)),
        ],
        out_specs=pl.BlockSpec((None, MOD_ROWS, tn), lambda l, j: (l, 0, j)),
        compiler_params=_params(("parallel", "parallel"), 40),
        name="modulation",
    )(cvec, w_mod, b_mod.reshape(depth, 1, n))


def _proj_kernel(*refs, modulated, rope):
    it = iter(refs)
    x_ref, g_ref = next(it), next(it)
    sh_ref = sc_ref = None
    if modulated:
        sh_ref, sc_ref = next(it), next(it)
    w_ref = next(it)
    tabs = [next(it) for _ in range(2 if rope else 0)]
    o_ref, h_ref, r_ref = next(it), next(it), next(it)
    j = pl.program_id(1)

    @pl.when(j == 0)
    def _():
        def store(sl, y):
            h_ref[sl, :] = y.astype(BF16)

        if modulated:
            _norm_modulate(x_ref, r_ref, store, g_ref[...], sc_ref[...], sh_ref[...])
        else:
            _norm_modulate(x_ref, r_ref, store, g_ref[...])

    z = _dot(h_ref[...], w_ref[...])
    groups_per_tile = z.shape[1] // LANES

    if rope is None:
        o_ref[...] = z.astype(o_ref.dtype)
        return

    tile, groups = rope

    def rotated(t):
        return t * tabs[0][...] + pltpu.roll(t, LANES // 2, 1) * tabs[1][...]

    if tile is None:
        for gi in range(groups_per_tile):
            sl = slice(gi * LANES, (gi + 1) * LANES)
            t = z[:, sl]
            o_ref[:, sl] = (rotated(t) if gi in groups else t).astype(o_ref.dtype)
    else:
        o_ref[...] = z.astype(o_ref.dtype)

        @pl.when(j == tile)
        def _():
            for gi in groups:
                sl = slice(gi * LANES, (gi + 1) * LANES)
                o_ref[:, sl] = rotated(z[:, sl]).astype(o_ref.dtype)


def _proj(x, xcol, k, g, w, *, tm, tn, out_dtype, mod=None, rope=None, tabs=(), pos_blocks=1,
          vmem_mib=48, name="proj"):
    t = x.shape[0]
    n = w.shape[1]
    assert t % tm == 0 and n % tn == 0 and w.shape[0] == k
    in_specs = [
        pl.BlockSpec((tm, k), lambda i, j: (i, xcol)),
        pl.BlockSpec((1, k), lambda i, j: (0, 0)),
    ]
    args = [x, g.reshape(1, k)]
    if mod is not None:
        mods, row_of, k_shift, k_scale = mod
        in_specs += [
            pl.BlockSpec((None, 1, k), lambda i, j: (row_of(i), 0, k_shift)),
            pl.BlockSpec((None, 1, k), lambda i, j: (row_of(i), 0, k_scale)),
        ]
        args += [mods, mods]
    w_mode = {"pipeline_mode": pl.Buffered(1)} if n == tn else {}
    in_specs.append(pl.BlockSpec((k, tn), lambda i, j: (0, j), **w_mode))
    args.append(w)
    for tab in tabs:
        in_specs.append(pl.BlockSpec((tm, LANES), lambda i, j: (i % pos_blocks, 0)))
        args.append(tab)
    return pl.pallas_call(
        functools.partial(_proj_kernel, modulated=mod is not None, rope=rope),
        out_shape=jax.ShapeDtypeStruct((t, n), out_dtype),
        grid=(t // tm, n // tn),
        in_specs=in_specs,
        out_specs=pl.BlockSpec((tm, tn), lambda i, j: (i, j)),
        scratch_shapes=[pltpu.VMEM((tm, k), BF16), pltpu.VMEM((tm, 1), F32)],
        compiler_params=_params(("parallel", "arbitrary"), vmem_mib),
        name=name,
    )(*args)


def _mla_kernel(*refs, n_lat, n_ctx, n_cast):
    refs = list(refs)
    n_in = 7 if n_lat else 4
    cast_in = refs[n_in:n_in + n_cast]
    cast_out = refs[n_in + n_cast + 1:n_in + 2 * n_cast + 1]
    del refs[n_in + n_cast + 1:n_in + 2 * n_cast + 1], refs[n_in:n_in + n_cast]
    if n_lat:
        q_ref, knl_ref, vl_ref, krl_ref, knc_ref, vc_ref, krc_ref, o_ref, k_scr, vt_scr = refs
    else:
        q_ref, knc_ref, vc_ref, krc_ref, o_ref, k_scr, vt_scr = refs
    for src, dst in zip(cast_in, cast_out):
        dst[...] = src[...].astype(dst.dtype)

    @pl.when(pl.program_id(2) == 0)
    def _():
        if n_lat:
            k_scr[0:n_lat, 0:LANES] = knl_ref[...]
            k_scr[0:n_lat, LANES:] = krl_ref[...].astype(BF16)
            vt_scr[0:MLA_V, 0:n_lat] = vl_ref[...].astype(F32).T.astype(BF16)
        k_scr[n_lat:, 0:LANES] = knc_ref[...]
        k_scr[n_lat:, LANES:] = krc_ref[...].astype(BF16)
        vt_scr[0:MLA_V, n_lat:] = vc_ref[...].astype(F32).T.astype(BF16)
        vt_scr[MLA_V:, :] = jnp.ones((ONES_ROWS, n_lat + n_ctx), BF16)

    n_sub = min(MLA_Q_SUB, q_ref.shape[0])
    subs = [slice(r, r + n_sub) for r in range(0, q_ref.shape[0], n_sub)]
    def scores(i):
        return _dot_nt(k_scr[...], q_ref[subs[i], :])

    def weights(s):
        return jnp.exp2(s - jnp.max(s, axis=0, keepdims=True)).astype(BF16)

    n = len(subs)
    s = [scores(i) if i < 2 else None for i in range(n)]
    p = weights(s[0])
    for i, rows in enumerate(subs):
        if i + 2 < n:
            s[i + 2] = scores(i + 2)
        o = _dot(vt_scr[...], p)
        if i + 1 < n:
            p = weights(s[i + 1])
        o_ref[rows, :] = (o[0:MLA_V] / o[MLA_V:MLA_V + 1]).T.astype(o_ref.dtype)


def _mla_attention(q, kv_c, z_c, kr_col, *, batch, n_q, n_ctx, kv_l=None, z_l=None, tq=256, casts=()):
    n_lat = 0 if kv_l is None else kv_l.shape[0] // batch
    nq_blocks = n_q // tq
    in_specs = [pl.BlockSpec((tq, MLA_QK_PAD), lambda b, h, i: (b * nq_blocks + i, h))]
    args = [q]
    if n_lat:
        in_specs += [
            pl.BlockSpec((n_lat, LANES), lambda b, h, i: (b, 2 * h)),
            pl.BlockSpec((n_lat, LANES), lambda b, h, i: (b, 2 * h + 1)),
            pl.BlockSpec((n_lat, LANES), lambda b, h, i: (b, kr_col)),
        ]
        args += [kv_l, kv_l, z_l]
    in_specs += [
        pl.BlockSpec((n_ctx, LANES), lambda b, h, i: (b, 2 * h)),
        pl.BlockSpec((n_ctx, LANES), lambda b, h, i: (b, 2 * h + 1)),
        pl.BlockSpec((n_ctx, LANES), lambda b, h, i: (b, kr_col)),
    ]
    args += [kv_c, kv_c, z_c]
    n_keys = n_lat + n_ctx
    n_steps = batch * MLA_HEADS * nq_blocks
    step_of = lambda b, h, i: ((b * MLA_HEADS + h) * nq_blocks + i, 0, 0)
    out_shape = [jax.ShapeDtypeStruct((batch * n_q, MLA_HEADS * MLA_V), BF16)]
    out_specs = [pl.BlockSpec((tq, MLA_V), lambda b, h, i: (b * nq_blocks + i, h))]
    for w in casts:
        assert w.shape[0] == n_steps
        slab = pl.BlockSpec((None,) + w.shape[1:], step_of)
        in_specs.append(slab)
        args.append(w)
        out_specs.append(slab)
        out_shape.append(jax.ShapeDtypeStruct(w.shape, BF16))
    outs = pl.pallas_call(
        functools.partial(_mla_kernel, n_lat=n_lat, n_ctx=n_ctx, n_cast=len(casts)),
        out_shape=out_shape,
        grid=(batch, MLA_HEADS, nq_blocks),
        in_specs=in_specs,
        out_specs=out_specs,
        scratch_shapes=[pltpu.VMEM((n_keys, MLA_QK_PAD), BF16),
                        pltpu.VMEM((MLA_V + ONES_ROWS, n_keys), BF16)],
        compiler_params=_params(("parallel", "parallel", "arbitrary"), 56),
        name="mla_attention",
    )(*args)
    return outs[0], outs[1:]


def _gelu(x):
    c = math.sqrt(2.0 / math.pi)
    return x * (0.5 + 0.5 * jnp.tanh(x * (c + (c * 0.044715) * (x * x))))


def _gmlp_kernel(u_ref, v_ref, w_ref, b_ref, o_ref):
    tiles = [(slice(n * GMLP_CHUNK, (n + 1) * GMLP_CHUNK), slice(g * GMLP_DIM, (g + 1) * GMLP_DIM), g)
             for n in range(u_ref.shape[0] // GMLP_CHUNK) for g in range(GMLP_GROUPS)]
    v = [_gelu(v_ref[rows, cols]) for rows, cols, _ in tiles]
    vc = [x - jnp.mean(x, axis=-1, keepdims=True) for x in v]
    var = [jnp.mean(x * x, axis=-1, keepdims=True) for x in vc]
    vn = [(x * lax.rsqrt(s + NORM_EPS)).astype(BF16) for x, s in zip(vc, var)]
    for (rows, cols, g), x in zip(tiles, vn):
        mixed = _dot(w_ref[g], x) + b_ref[g]
        o_ref[rows, cols] = (_gelu(u_ref[rows, cols]) * mixed).astype(o_ref.dtype)


def _gmlp(z, u_col, v_col, w_sp, b_sp, *, tm=256):
    t = z.shape[0]
    return pl.pallas_call(
        _gmlp_kernel,
        out_shape=jax.ShapeDtypeStruct((t, GMLP_WIDTH), BF16),
        grid=(t // tm,),
        in_specs=[
            pl.BlockSpec((tm, GMLP_WIDTH), lambda i: (i, u_col)),
            pl.BlockSpec((tm, GMLP_WIDTH), lambda i: (i, v_col)),
            pl.BlockSpec((GMLP_GROUPS, GMLP_CHUNK, GMLP_CHUNK), lambda i: (0, 0, 0)),
            pl.BlockSpec((GMLP_GROUPS, GMLP_CHUNK, 1), lambda i: (0, 0, 0)),
        ],
        out_specs=pl.BlockSpec((tm, GMLP_WIDTH), lambda i: (i, 0)),
        compiler_params=_params(("parallel",), 32),
        name="gmlp",
    )(z, z, w_sp, b_sp)


def _swa_kernel(sink_ref, q_ref, kl_ref, vl_ref, kc_ref, vc_ref, o_ref, *, tq, n_lat):
    start = pl.program_id(1) * tq
    span0 = pl.multiple_of(jnp.clip(start - SWA_WINDOW, 0, n_lat - SWA_SPAN), SWA_WINDOW)
    k_span = kl_ref[pl.ds(span0, SWA_SPAN), :]
    k_ctx = kc_ref[...]
    vt_span = vl_ref[pl.ds(span0, SWA_SPAN), :].astype(F32).T.astype(BF16)
    vt_ctx = vc_ref[...].astype(F32).T.astype(BF16)
    ones_span = jnp.ones((ONES_ROWS, vt_span.shape[1]), BF16)
    ones_ctx = jnp.ones((ONES_ROWS, vt_ctx.shape[1]), BF16)

    k_pos = span0 + lax.broadcasted_iota(jnp.int32, (SWA_SPAN, tq), 0)
    q_pos = start + lax.broadcasted_iota(jnp.int32, (SWA_SPAN, tq), 1)
    bias = jnp.where(jnp.abs(k_pos - q_pos) <= SWA_WINDOW, 0.0, NEG_INF).astype(F32)
    bias = jnp.concatenate([bias] * SWA_UNIT, axis=1)

    lane = lax.broadcasted_iota(jnp.int32, (tq, LANES), 1)
    first = (lane % SWA_HEAD_DIM) < (SWA_HEAD_DIM // 2)
    keep = [jnp.where(first, 1.0, 0.0).astype(BF16), jnp.where(first, 0.0, 1.0).astype(BF16)]

    units = [range(h0, h0 + SWA_UNIT) for h0 in range(0, SWA_HEADS, SWA_UNIT)]

    def scores(heads):
        g = heads[0] // SWA_GROUP
        k_cols = slice(g * LANES, (g + 1) * LANES)
        q = jnp.concatenate(
            [q_ref[:, (h // 2) * LANES:(h // 2 + 1) * LANES] * keep[h % 2] for h in heads], axis=0)
        return _dot_nt(k_span[:, k_cols], q) + bias, _dot_nt(k_ctx[:, k_cols], q)

    s_next = scores(units[0])
    for u, heads in enumerate(units):
        g = heads[0] // SWA_GROUP
        kv_cols = slice(g * SWA_HEAD_DIM, (g + 1) * SWA_HEAD_DIM)
        s_l, s_c = s_next
        if u + 1 < len(units):
            s_next = scores(units[u + 1])
        sink = jnp.concatenate([jnp.full((1, tq), sink_ref[h] * LOG2E, F32) for h in heads], axis=1)
        m = jnp.maximum(jnp.maximum(jnp.max(s_l, axis=0, keepdims=True),
                                    jnp.max(s_c, axis=0, keepdims=True)), sink)
        p_l = jnp.exp2(s_l - m).astype(BF16)
        p_c = jnp.exp2(s_c - m).astype(BF16)
        vt_l = jnp.concatenate([vt_span[kv_cols, :], ones_span], axis=0)
        vt_c = jnp.concatenate([vt_ctx[kv_cols, :], ones_ctx], axis=0)
        o = _dot(vt_l, p_l) + _dot(vt_c, p_c)
        dim = SWA_HEAD_DIM
        o = o[0:dim] / (o[dim:dim + 1] + jnp.exp2(sink - m))
        for pair in range(SWA_UNIT // 2):
            cols = [slice((2 * pair + r) * tq, (2 * pair + r + 1) * tq) for r in range(2)]
            both = jnp.concatenate([o[:, cols[0]], o[:, cols[1]]], axis=0)
            group = heads[0] // 2 + pair
            o_ref[:, group * LANES:(group + 1) * LANES] = both.T.astype(o_ref.dtype)


def _swa_attention(z_l, z_c, sinks, *, batch, n_lat, n_ctx, tq=SWA_BLOCK):
    nq_blocks = n_lat // tq
    k_cols = 2 * SWA_KV_COLS
    return pl.pallas_call(
        functools.partial(_swa_kernel, tq=tq, n_lat=n_lat),
        out_shape=jax.ShapeDtypeStruct((batch * n_lat, SWA_Q_COLS), BF16),
        grid=(batch, nq_blocks),
        in_specs=[
            pl.BlockSpec(memory_space=pltpu.SMEM),
            pl.BlockSpec((tq, SWA_Q_COLS), lambda b, i: (b * nq_blocks + i, 0)),
            pl.BlockSpec((n_lat, k_cols), lambda b, i: (b, SWA_Q_COLS // k_cols)),
            pl.BlockSpec((n_lat, SWA_KV_COLS), lambda b, i: (b, (SWA_Q_COLS + k_cols) // SWA_KV_COLS)),
            pl.BlockSpec((n_ctx, k_cols), lambda b, i: (b, 0)),
            pl.BlockSpec((n_ctx, SWA_KV_COLS), lambda b, i: (b, k_cols // SWA_KV_COLS)),
        ],
        out_specs=pl.BlockSpec((tq, SWA_Q_COLS), lambda b, i: (b * nq_blocks + i, 0)),
        compiler_params=_params(("parallel", "arbitrary"), 48),
        name="swa_attention",
    )(sinks, z_l, z_l, z_l, z_c, z_c)


def _outproj_kernel(a1_ref, a2_ref, w1_ref, w2_ref, x_ref, gate_ref, o_ref):
    y = _dot(a1_ref[...], w1_ref[...]) + _dot(a2_ref[...], w2_ref[...])
    o_ref[...] = x_ref[...] + gate_ref[...] * y


def _outproj(a1, a1_col, a2, a2_col, w, x, mods, row_of, k_gate, *, tm=512):
    t, d = x.shape
    half = w.shape[0] // 2
    return pl.pallas_call(
        _outproj_kernel,
        out_shape=jax.ShapeDtypeStruct((t, d), F32),
        grid=(t // tm,),
        in_specs=[
            pl.BlockSpec((tm, half), lambda i: (i, a1_col)),
            pl.BlockSpec((tm, half), lambda i: (i, a2_col)),
            pl.BlockSpec((half, d), lambda i: (0, 0)),
            pl.BlockSpec((half, d), lambda i: (1, 0)),
            pl.BlockSpec((tm, d), lambda i: (i, 0)),
            pl.BlockSpec((None, 1, d), lambda i: (row_of(i), 0, k_gate)),
        ],
        out_specs=pl.BlockSpec((tm, d), lambda i: (i, 0)),
        compiler_params=_params(("parallel",), 48),
        name="outproj",
    )(a1, a2, w, w, x, mods)


def _mlp_kernel(*refs, final):
    if final:
        x_ref, g_ref, sh_ref, sc_ref, gate_ref, w1_ref, w2_ref, fg_ref, o_ref, h_ref, r_ref = refs
    else:
        x_ref, g_ref, sh_ref, sc_ref, gate_ref, w1_ref, w2_ref, o_ref, h_ref, r_ref = refs
    k = pl.program_id(1)

    def ff_chunk():
        a = jnp.square(jnp.maximum(_dot(h_ref[...], w1_ref[...]), 0.0)).astype(BF16)
        return _dot(a, w2_ref[...])

    @pl.when(k == 0)
    def _():
        def store(sl, y):
            h_ref[sl, :] = y.astype(BF16)

        _norm_modulate(x_ref, r_ref, store, g_ref[...], sc_ref[...], sh_ref[...])
        o_ref[...] = ff_chunk()

    @pl.when(k > 0)
    def _():
        o_ref[...] += ff_chunk()

    @pl.when(k == pl.num_programs(1) - 1)
    def _():
        def rows(sl):
            out = x_ref[sl, :] + gate_ref[...] * o_ref[sl, :]
            if final:
                out = _rms(out, fg_ref[...])
            o_ref[sl, :] = out

        _for_row_chunks(x_ref.shape[0], rows)


def _mlp(x, g, mods, row_of, w1, w2, layer, *, final_g=None, tm=512, tf=1024, vmem_mib=56):
    t, d = x.shape
    n_k = w1.shape[2] // tf
    in_specs = [
        pl.BlockSpec((tm, d), lambda i, k: (i, 0)),
        pl.BlockSpec((1, d), lambda i, k: (0, 0)),
        pl.BlockSpec((None, 1, d), lambda i, k: (row_of(i), 0, 3)),
        pl.BlockSpec((None, 1, d), lambda i, k: (row_of(i), 0, 4)),
        pl.BlockSpec((None, 1, d), lambda i, k: (row_of(i), 0, 5)),
        pl.BlockSpec((None, d, tf), lambda i, k: (layer, 0, k)),
        pl.BlockSpec((None, tf, d), lambda i, k: (layer, k, 0)),
    ]
    args = [x, g.reshape(1, d), mods, mods, mods, w1, w2]
    if final_g is not None:
        in_specs.append(pl.BlockSpec((1, d), lambda i, k: (0, 0)))
        args.append(final_g.reshape(1, d))
    return pl.pallas_call(
        functools.partial(_mlp_kernel, final=final_g is not None),
        out_shape=jax.ShapeDtypeStruct((t, d), F32),
        grid=(t // tm, n_k),
        in_specs=in_specs,
        out_specs=pl.BlockSpec((tm, d), lambda i, k: (i, 0)),
        scratch_shapes=[pltpu.VMEM((tm, d), BF16), pltpu.VMEM((tm, 1), F32)],
        compiler_params=_params(("parallel", "arbitrary"), vmem_mib),
        name="mlp",
    )(*args)


def _axial_angles(n_tokens, rot_dim):
    t = jnp.arange(n_tokens)
    row = (t // GRID_W).astype(F32)
    col = (t % GRID_W).astype(F32)
    n_freq = rot_dim // 4
    inv_freq = ROPE_BASE ** (-jnp.arange(n_freq, dtype=F32) / n_freq)
    ang = jnp.concatenate([row[:, None] * inv_freq, col[:, None] * inv_freq], axis=-1)
    return jnp.cos(ang), jnp.sin(ang)


def _swap_halves(w):
    half = w.shape[-1] // 2
    return jnp.concatenate([w[..., half:], w[..., :half]], axis=-1)


def kernel(x, c, ctx, c_ctx, norm1_g, w_mod, b_mod, norm2_g, w_ff1, w_ff2, even_w_in, mla_q_norm_g,
           mla_w_uq, mla_kv_norm_g, mla_w_ukv, gmlp_w_sp, gmlp_b_sp, even_w_out, odd_w_in, swa_sinks,
           odd_w_out, final_norm_g):
    batch, n_lat, d = x.shape
    n_ctx = ctx.shape[1]
    assert batch + 1 <= MOD_ROWS
    xl = x.reshape(batch * n_lat, d)
    xc = ctx.reshape(batch * n_ctx, d)

    tm = 1024

    def lat_row(block_rows):
        per_sample = n_lat // block_rows
        return lambda i: i // per_sample

    ctx_row = lambda i: batch

    cvec = jnp.concatenate([c, c_ctx[None, :], jnp.zeros((MOD_ROWS - batch - 1, d), F32)], axis=0)
    mods = _modulation(cvec, w_mod, b_mod)
    mods = mods.reshape(mods.shape[0], MOD_ROWS, 1, N_MOD * d)

    cos_m, sin_m = _axial_angles(n_lat, MLA_ROPE)
    zeros64 = jnp.zeros((n_lat, LANES // 2), F32)
    fold_cos = jnp.concatenate([cos_m, cos_m, zeros64], axis=-1)
    fold_sin = jnp.concatenate([-sin_m, sin_m, zeros64], axis=-1)
    keep = jnp.concatenate([jnp.ones((tm, LANES // 2), F32), jnp.zeros((tm, LANES // 2), F32)], axis=-1)
    drop = jnp.zeros((tm, LANES), F32)

    w_in = even_w_in[0]
    cq_w, ckv_w = w_in[:, :MLA_LORA], w_in[:, MLA_LORA:2 * MLA_LORA]
    kr_w = w_in[:, 2 * MLA_LORA:2 * MLA_LORA + MLA_ROPE]
    gm_w = w_in[:, 2 * MLA_LORA + MLA_ROPE:]
    w_in0 = jnp.concatenate([cq_w, ckv_w, gm_w, kr_w, _swap_halves(kr_w)], axis=-1).astype(BF16)
    n_in0 = w_in0.shape[1]
    kr_group = (n_in0 - LANES) // LANES
    kr_rope = (None, (kr_group,))

    wq = mla_w_uq[0].reshape(MLA_LORA, MLA_HEADS, MLA_NOPE + MLA_ROPE)
    wq_rope = wq[..., MLA_NOPE:]
    w_uq = jnp.concatenate([wq[..., :MLA_NOPE], wq_rope, _swap_halves(wq_rope)], axis=-1)
    w_uq = (w_uq * (MLA_SCALE * LOG2E)).reshape(MLA_LORA, MLA_HEADS * MLA_QK_PAD).astype(BF16)
    q_rope = (None, tuple(range(1, 2 * MLA_HEADS, 2)))
    w_ukv = mla_w_ukv[0].astype(BF16)
    w_sp = gmlp_w_sp[0].astype(BF16)
    b_sp = gmlp_b_sp[0][:, :, None]
    w_out0 = even_w_out[0].astype(BF16)
    tm_i = 512

    def layer0_tokens(xs, tabs, latent):
        row_of = (lambda rows: lat_row(rows)) if latent else (lambda rows: ctx_row)
        pos_blocks = (lambda rows: n_lat // rows) if latent else (lambda rows: 1)
        m0 = mods[0]
        z = _proj(xs, 0, d, norm1_g[0], w_in0, tm=tm_i, tn=n_in0, out_dtype=F32,
                  mod=(m0, row_of(tm_i), 0, 1), rope=kr_rope, tabs=tabs, pos_blocks=pos_blocks(tm_i),
                  vmem_mib=56, name="even_in_proj")
        q = _proj(z, 0, MLA_LORA, mla_q_norm_g[0], w_uq, tm=tm, tn=w_uq.shape[1], out_dtype=BF16,
                  rope=q_rope, tabs=tabs, pos_blocks=pos_blocks(tm), name="mla_q_proj")
        kv = _proj(z, 1, MLA_LORA, mla_kv_norm_g[0], w_ukv, tm=tm, tn=w_ukv.shape[1], out_dtype=BF16,
                   name="mla_kv_proj")
        gm = _gmlp(z, 1, 2, w_sp, b_sp)
        return z, q, kv, gm

    z_l, q_l, kv_l, gm_l = layer0_tokens(xl, (fold_cos, fold_sin), True)
    z_c, q_c, kv_c, gm_c = layer0_tokens(xc, (keep, drop), False)
    cast_steps = batch * MLA_HEADS
    ff_slabs = [w.reshape(cast_steps, -1, w.shape[-1]) for w in (w_ff1, w_ff2)]
    att_l, ff_bf16 = _mla_attention(q_l, kv_c, z_c, kr_group, batch=batch, n_q=n_lat, n_ctx=n_ctx,
                                    kv_l=kv_l, z_l=z_l, tq=n_lat, casts=ff_slabs)
    w1_all, w2_all = ff_bf16[0].reshape(w_ff1.shape), ff_bf16[1].reshape(w_ff2.shape)
    att_c, _ = _mla_attention(q_c, kv_c, z_c, kr_group, batch=batch, n_q=n_ctx, n_ctx=n_ctx)

    tm_o = 512
    xl = _outproj(att_l, 0, gm_l, 0, w_out0, xl, mods[0], lat_row(tm_o), 2, tm=tm_o)
    xc = _outproj(att_c, 0, gm_c, 0, w_out0, xc, mods[0], ctx_row, 2, tm=tm_o)
    tm_f = 512
    xl = _mlp(xl, norm2_g[0], mods[0], lat_row(tm_f), w1_all, w2_all, 0, tm=tm_f)
    xc = _mlp(xc, norm2_g[0], mods[0], ctx_row, w1_all, w2_all, 0, tm=tm_f)

    cos_s, sin_s = _axial_angles(n_lat, SWA_HEAD_DIM)
    pair_cos = jnp.concatenate([cos_s] * 4, axis=-1)
    pair_sin = jnp.concatenate([-sin_s, -sin_s, sin_s, sin_s], axis=-1)
    half = SWA_HEAD_DIM // 2
    w_in = odd_w_in[0]
    wq = (w_in[:, :SWA_Q_COLS] * (SWA_SCALE * LOG2E)).reshape(d, SWA_HEADS // 2, 2, 2, half)
    wq = wq.transpose(0, 1, 3, 2, 4).reshape(d, SWA_Q_COLS)
    wk = w_in[:, SWA_Q_COLS:SWA_Q_COLS + SWA_KV_COLS].reshape(d, SWA_KV_HEADS, 2, 1, half)
    wk = jnp.broadcast_to(wk, (d, SWA_KV_HEADS, 2, 2, half)).reshape(d, 2 * SWA_KV_COLS)
    wv = w_in[:, SWA_Q_COLS + SWA_KV_COLS:]
    w_in1 = jnp.concatenate([wq, wk, wv], axis=-1).astype(BF16)
    w_kv1 = jnp.concatenate([wk, wv], axis=-1).astype(BF16)
    rot_groups = (SWA_Q_COLS + 2 * SWA_KV_COLS) // LANES
    m1 = mods[1]
    z1_l = _proj(xl, 0, d, norm1_g[1], w_in1, tm=tm_i, tn=w_in1.shape[1], out_dtype=BF16,
                 mod=(m1, lat_row(tm_i), 0, 1), rope=(None, tuple(range(rot_groups))),
                 tabs=(pair_cos, pair_sin), pos_blocks=n_lat // tm_i, vmem_mib=56, name="odd_in_proj")
    z1_c = _proj(xc, 0, d, norm1_g[1], w_kv1, tm=tm, tn=w_kv1.shape[1], out_dtype=BF16,
                 mod=(m1, ctx_row, 0, 1), name="odd_ctx_kv_proj")
    att = _swa_attention(z1_l, z1_c, swa_sinks[0], batch=batch, n_lat=n_lat, n_ctx=n_ctx)
    xl = _outproj(att, 0, att, 1, odd_w_out[0].astype(BF16), xl, m1, lat_row(tm_o), 2, tm=tm_o)
    out = _mlp(xl, norm2_g[1], m1, lat_row(tm_f), w1_all, w2_all, 1, final_g=final_norm_g, tm=tm_f)
    return out.reshape(batch, n_lat, d)
```

```python
import functools
import math

import jax
import jax.numpy as jnp
from jax import lax
from jax.experimental import pallas as pl
from jax.experimental.pallas import tpu as pltpu

F32 = jnp.float32
BF16 = jnp.bfloat16

LANES = 128
MIB = 1 << 20

GRID_W = 64
N_MOD = 6
NORM_EPS = 1e-6
ROPE_BASE = 10000.0
NEG_INF = -1e30

MLA_HEADS = 8
MLA_LORA = 512
MLA_NOPE = 128
MLA_ROPE = 64
MLA_V = 128
MLA_QK_PAD = 2 * LANES
MLA_SCALE = 1.0 / math.sqrt(MLA_NOPE + MLA_ROPE)
MLA_Q_SUB = 512

GMLP_GROUPS = 8
GMLP_DIM = 128
GMLP_CHUNK = 128
GMLP_WIDTH = GMLP_GROUPS * GMLP_DIM

SWA_HEADS = 32
SWA_KV_HEADS = 4
SWA_GROUP = SWA_HEADS // SWA_KV_HEADS
SWA_HEAD_DIM = 64
SWA_WINDOW = 128
SWA_BLOCK = 128
SWA_SPAN = SWA_BLOCK + 2 * SWA_WINDOW
SWA_SCALE = 1.0 / math.sqrt(SWA_HEAD_DIM)
SWA_Q_COLS = SWA_HEADS * SWA_HEAD_DIM
SWA_KV_COLS = SWA_KV_HEADS * SWA_HEAD_DIM
SWA_UNIT = SWA_GROUP

MOD_ROWS = 16
ROW_CHUNK = 128
STATS_CHUNK = 256
NARROW_ROW = 512
ONES_ROWS = 16
LOG2E = math.log2(math.e)


def _params(semantics, vmem_mib):
    return pltpu.CompilerParams(dimension_semantics=semantics, vmem_limit_bytes=vmem_mib * MIB)


def _rms(x, g):
    return x * lax.rsqrt(jnp.mean(x * x, axis=-1, keepdims=True) + NORM_EPS) * g


def _for_row_chunks(n_rows, fn, rows=ROW_CHUNK):
    rows = min(rows, n_rows)

    def body(r, carry):
        fn(pl.ds(pl.multiple_of(r * rows, rows), rows))
        return carry

    lax.fori_loop(0, n_rows // rows, body, 0)


def _norm_modulate(x_ref, r_ref, store, g, scale=None, shift=None):
    n_rows = x_ref.shape[0]
    gain = g if scale is None else g * (1.0 + scale)

    if x_ref.shape[1] <= NARROW_ROW:
        def whole(sl):
            y = _rms(x_ref[sl, :], gain)
            store(sl, y if shift is None else y + shift)

        _for_row_chunks(n_rows, whole)
        return

    def stats(sl):
        x = x_ref[sl, :]
        r_ref[sl, :] = lax.rsqrt(jnp.mean(x * x, axis=-1, keepdims=True) + NORM_EPS)

    _for_row_chunks(n_rows, stats, rows=STATS_CHUNK)

    def apply(sl):
        y = x_ref[sl, :] * r_ref[sl, :] * gain
        store(sl, y if shift is None else y + shift)

    _for_row_chunks(n_rows, apply)


def _dot(a, b):
    return jnp.dot(a, b, preferred_element_type=F32)


def _dot_nt(a, b):
    return lax.dot_general(a, b, (((1,), (1,)), ((), ())), preferred_element_type=F32)


def _mod_kernel(c_ref, w_ref, b_ref, o_ref):
    c = c_ref[...]
    s = (c * jax.nn.sigmoid(c)).astype(BF16)
    o_ref[...] = _dot(s, w_ref[...].astype(BF16)) + b_ref[...]


def _modulation(cvec, w_mod, b_mod, *, tn=1024):
    depth, d, n = w_mod.shape
    return pl.pallas_call(
        _mod_kernel,
        out_shape=jax.ShapeDtypeStruct((depth, MOD_ROWS, n), F32),
        grid=(depth, n // tn),
        in_specs=[
            pl.BlockSpec((MOD_ROWS, d), lambda l, j: (0, 0)),
            pl.BlockSpec((None, d, tn), lambda l, j: (l, 0, j)),
            pl.BlockSpec((None, 1, tn), lambda l, j: (l, 0, j)),
        ],
        out_specs=pl.BlockSpec((None, MOD_ROWS, tn), lambda l, j: (l, 0, j)),
        compiler_params=_params(("parallel", "parallel"), 40),
        name="modulation",
    )(cvec, w_mod, b_mod.reshape(depth, 1, n))


def _proj_kernel(*refs, modulated, rope):
    it = iter(refs)
    x_ref, g_ref = next(it), next(it)
    sh_ref = sc_ref = None
    if modulated:
        sh_ref, sc_ref = next(it), next(it)
    w_ref = next(it)
    tabs = [next(it) for _ in range(2 if rope else 0)]
    o_ref, h_ref, r_ref = next(it), next(it), next(it)
    j = pl.program_id(1)

    @pl.when(j == 0)
    def _():
        def store(sl, y):
            h_ref[sl, :] = y.astype(BF16)

        if modulated:
            _norm_modulate(x_ref, r_ref, store, g_ref[...], sc_ref[...], sh_ref[...])
        else:
            _norm_modulate(x_ref, r_ref, store, g_ref[...])

    z = _dot(h_ref[...], w_ref[...])
    groups_per_tile = z.shape[1] // LANES

    if rope is None:
        o_ref[...] = z.astype(o_ref.dtype)
        return

    tile, groups = rope

    def rotated(t):
        return t * tabs[0][...] + pltpu.roll(t, LANES // 2, 1) * tabs[1][...]

    if tile is None:
        for gi in range(groups_per_tile):
            sl = slice(gi * LANES, (gi + 1) * LANES)
            t = z[:, sl]
            o_ref[:, sl] = (rotated(t) if gi in groups else t).astype(o_ref.dtype)
    else:
        o_ref[...] = z.astype(o_ref.dtype)

        @pl.when(j == tile)
        def _():
            for gi in groups:
                sl = slice(gi * LANES, (gi + 1) * LANES)
                o_ref[:, sl] = rotated(z[:, sl]).astype(o_ref.dtype)


def _proj(x, xcol, k, g, w, *, tm, tn, out_dtype, mod=None, rope=None, tabs=(), pos_blocks=1,
          vmem_mib=48, name="proj"):
    t = x.shape[0]
    n = w.shape[1]
    assert t % tm == 0 and n % tn == 0 and w.shape[0] == k
    in_specs = [
        pl.BlockSpec((tm, k), lambda i, j: (i, xcol)),
        pl.BlockSpec((1, k), lambda i, j: (0, 0)),
    ]
    args = [x, g.reshape(1, k)]
    if mod is not None:
        mods, row_of, k_shift, k_scale = mod
        in_specs += [
            pl.BlockSpec((None, 1, k), lambda i, j: (row_of(i), 0, k_shift)),
            pl.BlockSpec((None, 1, k), lambda i, j: (row_of(i), 0, k_scale)),
        ]
        args += [mods, mods]
    w_mode = {"pipeline_mode": pl.Buffered(1)} if n == tn else {}
    in_specs.append(pl.BlockSpec((k, tn), lambda i, j: (0, j), **w_mode))
    args.append(w)
    for tab in tabs:
        in_specs.append(pl.BlockSpec((tm, LANES), lambda i, j: (i % pos_blocks, 0)))
        args.append(tab)
    return pl.pallas_call(
        functools.partial(_proj_kernel, modulated=mod is not None, rope=rope),
        out_shape=jax.ShapeDtypeStruct((t, n), out_dtype),
        grid=(t // tm, n // tn),
        in_specs=in_specs,
        out_specs=pl.BlockSpec((tm, tn), lambda i, j: (i, j)),
        scratch_shapes=[pltpu.VMEM((tm, k), BF16), pltpu.VMEM((tm, 1), F32)],
        compiler_params=_params(("parallel", "arbitrary"), vmem_mib),
        name=name,
    )(*args)


def _even_in_kernel(x_ref, g_ref, sh_ref, sc_ref, w_ref, cos_ref, sin_ref, gq_ref, gkv_ref, wq_ref,
                    wkv_ref, z_ref, q_ref, kv_ref, h_ref, r_ref, *, lora, rot_group, q_groups):
    def store(sl, y):
        h_ref[sl, :] = y.astype(BF16)

    _norm_modulate(x_ref, r_ref, store, g_ref[...], sc_ref[...], sh_ref[...])
    z = _dot(h_ref[...], w_ref[...])

    def rotated(t):
        return t * cos_ref[...] + pltpu.roll(t, LANES // 2, 1) * sin_ref[...]

    kept = z[:, 2 * lora:]
    for gi in range(kept.shape[1] // LANES):
        sl = slice(gi * LANES, (gi + 1) * LANES)
        z_ref[:, sl] = rotated(kept[:, sl]) if gi == rot_group else kept[:, sl]

    q = _dot(_rms(z[:, :lora], gq_ref[...]).astype(BF16), wq_ref[...])
    for gi in range(q.shape[1] // LANES):
        sl = slice(gi * LANES, (gi + 1) * LANES)
        q_ref[:, sl] = (rotated(q[:, sl]) if gi in q_groups else q[:, sl]).astype(q_ref.dtype)
    kv = _dot(_rms(z[:, lora:2 * lora], gkv_ref[...]).astype(BF16), wkv_ref[...])
    kv_ref[...] = kv.astype(kv_ref.dtype)


def _even_in(x, g, mods, row_of, w, tabs, pos_blocks, gq, gkv, wq, wkv, *, tm, lora, rot_group,
             q_groups, vmem_mib=58):
    t, d = x.shape
    n = w.shape[1]
    n_keep = n - 2 * lora
    once = {"pipeline_mode": pl.Buffered(1)}
    row_blk = lambda i: (i, 0)
    fixed = lambda i: (0, 0)
    return pl.pallas_call(
        functools.partial(_even_in_kernel, lora=lora, rot_group=rot_group, q_groups=q_groups),
        out_shape=[jax.ShapeDtypeStruct((t, n_keep), F32),
                   jax.ShapeDtypeStruct((t, wq.shape[1]), BF16),
                   jax.ShapeDtypeStruct((t, wkv.shape[1]), BF16)],
        grid=(t // tm,),
        in_specs=[
            pl.BlockSpec((tm, d), row_blk),
            pl.BlockSpec((1, d), fixed),
            pl.BlockSpec((None, 1, d), lambda i: (row_of(i), 0, 0)),
            pl.BlockSpec((None, 1, d), lambda i: (row_of(i), 0, 1)),
            pl.BlockSpec((d, n), fixed, **once),
            pl.BlockSpec((tm, LANES), lambda i: (i % pos_blocks, 0)),
            pl.BlockSpec((tm, LANES), lambda i: (i % pos_blocks, 0)),
            pl.BlockSpec((1, lora), fixed),
            pl.BlockSpec((1, lora), fixed),
            pl.BlockSpec(wq.shape, fixed, **once),
            pl.BlockSpec(wkv.shape, fixed, **once),
        ],
        out_specs=[pl.BlockSpec((tm, n_keep), row_blk),
                   pl.BlockSpec((tm, wq.shape[1]), row_blk),
                   pl.BlockSpec((tm, wkv.shape[1]), row_blk)],
        scratch_shapes=[pltpu.VMEM((tm, d), BF16), pltpu.VMEM((tm, 1), F32)],
        compiler_params=_params(("parallel",), vmem_mib),
        name="even_in_proj",
    )(x, g.reshape(1, d), mods, mods, w, tabs[0], tabs[1], gq.reshape(1, lora), gkv.reshape(1, lora),
      wq, wkv)


def _mla_kernel(*refs, n_lat, n_ctx, n_cast):
    refs = list(refs)
    n_in = 7 if n_lat else 4
    cast_in = refs[n_in:n_in + n_cast]
    cast_out = refs[n_in + n_cast + 1:n_in + 2 * n_cast + 1]
    del refs[n_in + n_cast + 1:n_in + 2 * n_cast + 1], refs[n_in:n_in + n_cast]
    if n_lat:
        q_ref, knl_ref, vl_ref, krl_ref, knc_ref, vc_ref, krc_ref, o_ref, k_scr, vt_scr = refs
    else:
        q_ref, knc_ref, vc_ref, krc_ref, o_ref, k_scr, vt_scr = refs
    for src, dst in zip(cast_in, cast_out):
        dst[...] = src[...].astype(dst.dtype)

    @pl.when(pl.program_id(2) == 0)
    def _():
        if n_lat:
            k_scr[0:n_lat, 0:LANES] = knl_ref[...]
            k_scr[0:n_lat, LANES:] = krl_ref[...].astype(BF16)
            vt_scr[0:MLA_V, 0:n_lat] = vl_ref[...].astype(F32).T.astype(BF16)
        k_scr[n_lat:, 0:LANES] = knc_ref[...]
        k_scr[n_lat:, LANES:] = krc_ref[...].astype(BF16)
        vt_scr[0:MLA_V, n_lat:] = vc_ref[...].astype(F32).T.astype(BF16)
        vt_scr[MLA_V:, :] = jnp.ones((ONES_ROWS, n_lat + n_ctx), BF16)

    n_sub = min(MLA_Q_SUB, q_ref.shape[0])
    subs = [slice(r, r + n_sub) for r in range(0, q_ref.shape[0], n_sub)]
    def scores(i):
        return _dot_nt(k_scr[...], q_ref[subs[i], :])

    def weights(s):
        return jnp.exp2(s - jnp.max(s, axis=0, keepdims=True)).astype(BF16)

    n = len(subs)
    s = [scores(i) if i < 2 else None for i in range(n)]
    p = weights(s[0])
    for i, rows in enumerate(subs):
        if i + 2 < n:
            s[i + 2] = scores(i + 2)
        o = _dot(vt_scr[...], p)
        if i + 1 < n:
            p = weights(s[i + 1])
        o_ref[rows, :] = (o[0:MLA_V] / o[MLA_V:MLA_V + 1]).T.astype(o_ref.dtype)


def _mla_attention(q, kv_c, z_c, kr_col, *, batch, n_q, n_ctx, kv_l=None, z_l=None, tq=256, casts=()):
    n_lat = 0 if kv_l is None else kv_l.shape[0] // batch
    nq_blocks = n_q // tq
    in_specs = [pl.BlockSpec((tq, MLA_QK_PAD), lambda b, h, i: (b * nq_blocks + i, h))]
    args = [q]
    if n_lat:
        in_specs += [
            pl.BlockSpec((n_lat, LANES), lambda b, h, i: (b, 2 * h)),
            pl.BlockSpec((n_lat, LANES), lambda b, h, i: (b, 2 * h + 1)),
            pl.BlockSpec((n_lat, LANES), lambda b, h, i: (b, kr_col)),
        ]
        args += [kv_l, kv_l, z_l]
    in_specs += [
        pl.BlockSpec((n_ctx, LANES), lambda b, h, i: (b, 2 * h)),
        pl.BlockSpec((n_ctx, LANES), lambda b, h, i: (b, 2 * h + 1)),
        pl.BlockSpec((n_ctx, LANES), lambda b, h, i: (b, kr_col)),
    ]
    args += [kv_c, kv_c, z_c]
    n_keys = n_lat + n_ctx
    n_steps = batch * MLA_HEADS * nq_blocks
    step_of = lambda b, h, i: ((b * MLA_HEADS + h) * nq_blocks + i, 0, 0)
    out_shape = [jax.ShapeDtypeStruct((batch * n_q, MLA_HEADS * MLA_V), BF16)]
    out_specs = [pl.BlockSpec((tq, MLA_V), lambda b, h, i: (b * nq_blocks + i, h))]
    for w in casts:
        assert w.shape[0] == n_steps
        slab = pl.BlockSpec((None,) + w.shape[1:], step_of)
        in_specs.append(slab)
        args.append(w)
        out_specs.append(slab)
        out_shape.append(jax.ShapeDtypeStruct(w.shape, BF16))
    outs = pl.pallas_call(
        functools.partial(_mla_kernel, n_lat=n_lat, n_ctx=n_ctx, n_cast=len(casts)),
        out_shape=out_shape,
        grid=(batch, MLA_HEADS, nq_blocks),
        in_specs=in_specs,
        out_specs=out_specs,
        scratch_shapes=[pltpu.VMEM((n_keys, MLA_QK_PAD), BF16),
                        pltpu.VMEM((MLA_V + ONES_ROWS, n_keys), BF16)],
        compiler_params=_params(("parallel", "parallel", "arbitrary"), 56),
        name="mla_attention",
    )(*args)
    return outs[0], outs[1:]


def _gelu(x):
    c = math.sqrt(2.0 / math.pi)
    return x * (0.5 + 0.5 * jnp.tanh(x * (c + (c * 0.044715) * (x * x))))


def _gmlp_kernel(u_ref, v_ref, w_ref, b_ref, o_ref):
    tiles = [(slice(n * GMLP_CHUNK, (n + 1) * GMLP_CHUNK), slice(g * GMLP_DIM, (g + 1) * GMLP_DIM), g)
             for n in range(u_ref.shape[0] // GMLP_CHUNK) for g in range(GMLP_GROUPS)]
    v = [_gelu(v_ref[rows, cols]) for rows, cols, _ in tiles]
    vc = [x - jnp.mean(x, axis=-1, keepdims=True) for x in v]
    var = [jnp.mean(x * x, axis=-1, keepdims=True) for x in vc]
    vn = [(x * lax.rsqrt(s + NORM_EPS)).astype(BF16) for x, s in zip(vc, var)]
    for (rows, cols, g), x in zip(tiles, vn):
        mixed = _dot(w_ref[g], x) + b_ref[g]
        o_ref[rows, cols] = (_gelu(u_ref[rows, cols]) * mixed).astype(o_ref.dtype)


def _gmlp(z, u_col, v_col, w_sp, b_sp, *, tm=256):
    t = z.shape[0]
    return pl.pallas_call(
        _gmlp_kernel,
        out_shape=jax.ShapeDtypeStruct((t, GMLP_WIDTH), BF16),
        grid=(t // tm,),
        in_specs=[
            pl.BlockSpec((tm, GMLP_WIDTH), lambda i: (i, u_col)),
            pl.BlockSpec((tm, GMLP_WIDTH), lambda i: (i, v_col)),
            pl.BlockSpec((GMLP_GROUPS, GMLP_CHUNK, GMLP_CHUNK), lambda i: (0, 0, 0)),
            pl.BlockSpec((GMLP_GROUPS, GMLP_CHUNK, 1), lambda i: (0, 0, 0)),
        ],
        out_specs=pl.BlockSpec((tm, GMLP_WIDTH), lambda i: (i, 0)),
        compiler_params=_params(("parallel",), 32),
        name="gmlp",
    )(z, z, w_sp, b_sp)


def _swa_kernel(sink_ref, q_ref, kl_ref, vl_ref, kc_ref, vc_ref, o_ref, *, tq, n_lat):
    start = pl.program_id(1) * tq
    span0 = pl.multiple_of(jnp.clip(start - SWA_WINDOW, 0, n_lat - SWA_SPAN), SWA_WINDOW)
    k_span = kl_ref[pl.ds(span0, SWA_SPAN), :]
    k_ctx = kc_ref[...]
    vt_span = vl_ref[pl.ds(span0, SWA_SPAN), :].astype(F32).T.astype(BF16)
    vt_ctx = vc_ref[...].astype(F32).T.astype(BF16)
    ones_span = jnp.ones((ONES_ROWS, vt_span.shape[1]), BF16)
    ones_ctx = jnp.ones((ONES_ROWS, vt_ctx.shape[1]), BF16)

    k_pos = span0 + lax.broadcasted_iota(jnp.int32, (SWA_SPAN, tq), 0)
    q_pos = start + lax.broadcasted_iota(jnp.int32, (SWA_SPAN, tq), 1)
    bias = jnp.where(jnp.abs(k_pos - q_pos) <= SWA_WINDOW, 0.0, NEG_INF).astype(F32)
    bias = jnp.concatenate([bias] * SWA_UNIT, axis=1)

    lane = lax.broadcasted_iota(jnp.int32, (tq, LANES), 1)
    first = (lane % SWA_HEAD_DIM) < (SWA_HEAD_DIM // 2)
    keep = [jnp.where(first, 1.0, 0.0).astype(BF16), jnp.where(first, 0.0, 1.0).astype(BF16)]

    units = [range(h0, h0 + SWA_UNIT) for h0 in range(0, SWA_HEADS, SWA_UNIT)]

    def scores(heads):
        g = heads[0] // SWA_GROUP
        k_cols = slice(g * LANES, (g + 1) * LANES)
        q = jnp.concatenate(
            [q_ref[:, (h // 2) * LANES:(h // 2 + 1) * LANES] * keep[h % 2] for h in heads], axis=0)
        return _dot_nt(k_span[:, k_cols], q) + bias, _dot_nt(k_ctx[:, k_cols], q)

    s_next = scores(units[0])
    for u, heads in enumerate(units):
        g = heads[0] // SWA_GROUP
        kv_cols = slice(g * SWA_HEAD_DIM, (g + 1) * SWA_HEAD_DIM)
        s_l, s_c = s_next
        if u + 1 < len(units):
            s_next = scores(units[u + 1])
        sink = jnp.concatenate([jnp.full((1, tq), sink_ref[h] * LOG2E, F32) for h in heads], axis=1)
        m = jnp.maximum(jnp.maximum(jnp.max(s_l, axis=0, keepdims=True),
                                    jnp.max(s_c, axis=0, keepdims=True)), sink)
        p_l = jnp.exp2(s_l - m).astype(BF16)
        p_c = jnp.exp2(s_c - m).astype(BF16)
        vt_l = jnp.concatenate([vt_span[kv_cols, :], ones_span], axis=0)
        vt_c = jnp.concatenate([vt_ctx[kv_cols, :], ones_ctx], axis=0)
        o = _dot(vt_l, p_l) + _dot(vt_c, p_c)
        dim = SWA_HEAD_DIM
        o = o[0:dim] / (o[dim:dim + 1] + jnp.exp2(sink - m))
        for pair in range(SWA_UNIT // 2):
            cols = [slice((2 * pair + r) * tq, (2 * pair + r + 1) * tq) for r in range(2)]
            both = jnp.concatenate([o[:, cols[0]], o[:, cols[1]]], axis=0)
            group = heads[0] // 2 + pair
            o_ref[:, group * LANES:(group + 1) * LANES] = both.T.astype(o_ref.dtype)


def _swa_attention(z_l, z_c, sinks, *, batch, n_lat, n_ctx, tq=SWA_BLOCK):
    nq_blocks = n_lat // tq
    k_cols = 2 * SWA_KV_COLS
    return pl.pallas_call(
        functools.partial(_swa_kernel, tq=tq, n_lat=n_lat),
        out_shape=jax.ShapeDtypeStruct((batch * n_lat, SWA_Q_COLS), BF16),
        grid=(batch, nq_blocks),
        in_specs=[
            pl.BlockSpec(memory_space=pltpu.SMEM),
            pl.BlockSpec((tq, SWA_Q_COLS), lambda b, i: (b * nq_blocks + i, 0)),
            pl.BlockSpec((n_lat, k_cols), lambda b, i: (b, SWA_Q_COLS // k_cols)),
            pl.BlockSpec((n_lat, SWA_KV_COLS), lambda b, i: (b, (SWA_Q_COLS + k_cols) // SWA_KV_COLS)),
            pl.BlockSpec((n_ctx, k_cols), lambda b, i: (b, 0)),
            pl.BlockSpec((n_ctx, SWA_KV_COLS), lambda b, i: (b, k_cols // SWA_KV_COLS)),
        ],
        out_specs=pl.BlockSpec((tq, SWA_Q_COLS), lambda b, i: (b * nq_blocks + i, 0)),
        compiler_params=_params(("parallel", "arbitrary"), 48),
        name="swa_attention",
    )(sinks, z_l, z_l, z_l, z_c, z_c)


def _outproj_kernel(a1_ref, a2_ref, w1_ref, w2_ref, x_ref, gate_ref, o_ref):
    y = _dot(a1_ref[...], w1_ref[...]) + _dot(a2_ref[...], w2_ref[...])
    o_ref[...] = x_ref[...] + gate_ref[...] * y


def _outproj(a1, a1_col, a2, a2_col, w, x, mods, row_of, k_gate, *, tm=512):
    t, d = x.shape
    half = w.shape[0] // 2
    return pl.pallas_call(
        _outproj_kernel,
        out_shape=jax.ShapeDtypeStruct((t, d), F32),
        grid=(t // tm,),
        in_specs=[
            pl.BlockSpec((tm, half), lambda i: (i, a1_col)),
            pl.BlockSpec((tm, half), lambda i: (i, a2_col)),
            pl.BlockSpec((half, d), lambda i: (0, 0)),
            pl.BlockSpec((half, d), lambda i: (1, 0)),
            pl.BlockSpec((tm, d), lambda i: (i, 0)),
            pl.BlockSpec((None, 1, d), lambda i: (row_of(i), 0, k_gate)),
        ],
        out_specs=pl.BlockSpec((tm, d), lambda i: (i, 0)),
        compiler_params=_params(("parallel",), 48),
        name="outproj",
    )(a1, a2, w, w, x, mods)


def _mlp_kernel(*refs, final):
    if final:
        x_ref, g_ref, sh_ref, sc_ref, gate_ref, w1_ref, w2_ref, fg_ref, o_ref, h_ref, r_ref = refs
    else:
        x_ref, g_ref, sh_ref, sc_ref, gate_ref, w1_ref, w2_ref, o_ref, h_ref, r_ref = refs
    k = pl.program_id(1)

    def ff_chunk():
        a = jnp.square(jnp.maximum(_dot(h_ref[...], w1_ref[...]), 0.0)).astype(BF16)
        return _dot(a, w2_ref[...])

    @pl.when(k == 0)
    def _():
        def store(sl, y):
            h_ref[sl, :] = y.astype(BF16)

        _norm_modulate(x_ref, r_ref, store, g_ref[...], sc_ref[...], sh_ref[...])
        o_ref[...] = ff_chunk()

    @pl.when(k > 0)
    def _():
        o_ref[...] += ff_chunk()

    @pl.when(k == pl.num_programs(1) - 1)
    def _():
        def rows(sl):
            out = x_ref[sl, :] + gate_ref[...] * o_ref[sl, :]
            if final:
                out = _rms(out, fg_ref[...])
            o_ref[sl, :] = out

        _for_row_chunks(x_ref.shape[0], rows)


def _mlp(x, g, mods, row_of, w1, w2, layer, *, final_g=None, tm=512, tf=1024, vmem_mib=56):
    t, d = x.shape
    n_k = w1.shape[2] // tf
    in_specs = [
        pl.BlockSpec((tm, d), lambda i, k: (i, 0)),
        pl.BlockSpec((1, d), lambda i, k: (0, 0)),
        pl.BlockSpec((None, 1, d), lambda i, k: (row_of(i), 0, 3)),
        pl.BlockSpec((None, 1, d), lambda i, k: (row_of(i), 0, 4)),
        pl.BlockSpec((None, 1, d), lambda i, k: (row_of(i), 0, 5)),
        pl.BlockSpec((None, d, tf), lambda i, k: (layer, 0, k)),
        pl.BlockSpec((None, tf, d), lambda i, k: (layer, k, 0)),
    ]
    args = [x, g.reshape(1, d), mods, mods, mods, w1, w2]
    if final_g is not None:
        in_specs.append(pl.BlockSpec((1, d), lambda i, k: (0, 0)))
        args.append(final_g.reshape(1, d))
    return pl.pallas_call(
        functools.partial(_mlp_kernel, final=final_g is not None),
        out_shape=jax.ShapeDtypeStruct((t, d), F32),
        grid=(t // tm, n_k),
        in_specs=in_specs,
        out_specs=pl.BlockSpec((tm, d), lambda i, k: (i, 0)),
        scratch_shapes=[pltpu.VMEM((tm, d), BF16), pltpu.VMEM((tm, 1), F32)],
        compiler_params=_params(("parallel", "arbitrary"), vmem_mib),
        name="mlp",
    )(*args)


def _axial_angles(n_tokens, rot_dim):
    t = jnp.arange(n_tokens)
    row = (t // GRID_W).astype(F32)
    col = (t % GRID_W).astype(F32)
    n_freq = rot_dim // 4
    inv_freq = ROPE_BASE ** (-jnp.arange(n_freq, dtype=F32) / n_freq)
    ang = jnp.concatenate([row[:, None] * inv_freq, col[:, None] * inv_freq], axis=-1)
    return jnp.cos(ang), jnp.sin(ang)


def _swap_halves(w):
    half = w.shape[-1] // 2
    return jnp.concatenate([w[..., half:], w[..., :half]], axis=-1)


def kernel(x, c, ctx, c_ctx, norm1_g, w_mod, b_mod, norm2_g, w_ff1, w_ff2, even_w_in, mla_q_norm_g,
           mla_w_uq, mla_kv_norm_g, mla_w_ukv, gmlp_w_sp, gmlp_b_sp, even_w_out, odd_w_in, swa_sinks,
           odd_w_out, final_norm_g):
    batch, n_lat, d = x.shape
    n_ctx = ctx.shape[1]
    assert batch + 1 <= MOD_ROWS
    xl = x.reshape(batch * n_lat, d)
    xc = ctx.reshape(batch * n_ctx, d)

    tm = 1024

    def lat_row(block_rows):
        per_sample = n_lat // block_rows
        return lambda i: i // per_sample

    ctx_row = lambda i: batch

    cvec = jnp.concatenate([c, c_ctx[None, :], jnp.zeros((MOD_ROWS - batch - 1, d), F32)], axis=0)
    mods = _modulation(cvec, w_mod, b_mod)
    mods = mods.reshape(mods.shape[0], MOD_ROWS, 1, N_MOD * d)

    cos_m, sin_m = _axial_angles(n_lat, MLA_ROPE)
    zeros64 = jnp.zeros((n_lat, LANES // 2), F32)
    fold_cos = jnp.concatenate([cos_m, cos_m, zeros64], axis=-1)
    fold_sin = jnp.concatenate([-sin_m, sin_m, zeros64], axis=-1)
    keep = jnp.concatenate([jnp.ones((tm, LANES // 2), F32), jnp.zeros((tm, LANES // 2), F32)], axis=-1)
    drop = jnp.zeros((tm, LANES), F32)

    w_in = even_w_in[0]
    cq_w, ckv_w = w_in[:, :MLA_LORA], w_in[:, MLA_LORA:2 * MLA_LORA]
    kr_w = w_in[:, 2 * MLA_LORA:2 * MLA_LORA + MLA_ROPE]
    gm_w = w_in[:, 2 * MLA_LORA + MLA_ROPE:]
    w_in0 = jnp.concatenate([cq_w, ckv_w, gm_w, kr_w, _swap_halves(kr_w)], axis=-1).astype(BF16)
    kr_group = (w_in0.shape[1] - 2 * MLA_LORA - LANES) // LANES

    wq = mla_w_uq[0].reshape(MLA_LORA, MLA_HEADS, MLA_NOPE + MLA_ROPE)
    wq_rope = wq[..., MLA_NOPE:]
    w_uq = jnp.concatenate([wq[..., :MLA_NOPE], wq_rope, _swap_halves(wq_rope)], axis=-1)
    w_uq = (w_uq * (MLA_SCALE * LOG2E)).reshape(MLA_LORA, MLA_HEADS * MLA_QK_PAD).astype(BF16)
    q_groups = tuple(range(1, 2 * MLA_HEADS, 2))
    w_ukv = mla_w_ukv[0].astype(BF16)
    w_sp = gmlp_w_sp[0].astype(BF16)
    b_sp = gmlp_b_sp[0][:, :, None]
    tm_i = 512

    def layer0_tokens(xs, tabs, latent):
        row_of = lat_row(tm_i) if latent else ctx_row
        pos_blocks = n_lat // tm_i if latent else 1
        z, q, kv = _even_in(xs, norm1_g[0], mods[0], row_of, w_in0, tabs, pos_blocks, mla_q_norm_g[0],
                            mla_kv_norm_g[0], w_uq, w_ukv, tm=tm_i, lora=MLA_LORA, rot_group=kr_group,
                            q_groups=q_groups)
        gm = _gmlp(z, 0, 1, w_sp, b_sp)
        return z, q, kv, gm

    z_l, q_l, kv_l, gm_l = layer0_tokens(xl, (fold_cos, fold_sin), True)
    z_c, q_c, kv_c, gm_c = layer0_tokens(xc, (keep, drop), False)
    cast_steps = batch * MLA_HEADS
    late = (w_ff1, w_ff2, even_w_out, odd_w_out)
    slabs = [w.reshape(cast_steps, -1, w.shape[-1]) for w in late]
    att_l, late_bf16 = _mla_attention(q_l, kv_c, z_c, kr_group, batch=batch, n_q=n_lat, n_ctx=n_ctx,
                                      kv_l=kv_l, z_l=z_l, tq=n_lat, casts=slabs)
    w1_all, w2_all, w_out0, w_out1 = [b.reshape(w.shape) for b, w in zip(late_bf16, late)]
    w_out0, w_out1 = w_out0[0], w_out1[0]
    att_c, _ = _mla_attention(q_c, kv_c, z_c, kr_group, batch=batch, n_q=n_ctx, n_ctx=n_ctx)

    tm_o = 512
    xl = _outproj(att_l, 0, gm_l, 0, w_out0, xl, mods[0], lat_row(tm_o), 2, tm=tm_o)
    xc = _outproj(att_c, 0, gm_c, 0, w_out0, xc, mods[0], ctx_row, 2, tm=tm_o)
    tm_f = 512
    xl = _mlp(xl, norm2_g[0], mods[0], lat_row(tm_f), w1_all, w2_all, 0, tm=tm_f)
    xc = _mlp(xc, norm2_g[0], mods[0], ctx_row, w1_all, w2_all, 0, tm=tm_f)

    cos_s, sin_s = _axial_angles(n_lat, SWA_HEAD_DIM)
    pair_cos = jnp.concatenate([cos_s] * 4, axis=-1)
    pair_sin = jnp.concatenate([-sin_s, -sin_s, sin_s, sin_s], axis=-1)
    half = SWA_HEAD_DIM // 2
    w_in = odd_w_in[0]
    wq = (w_in[:, :SWA_Q_COLS] * (SWA_SCALE * LOG2E)).reshape(d, SWA_HEADS // 2, 2, 2, half)
    wq = wq.transpose(0, 1, 3, 2, 4).reshape(d, SWA_Q_COLS)
    wk = w_in[:, SWA_Q_COLS:SWA_Q_COLS + SWA_KV_COLS].reshape(d, SWA_KV_HEADS, 2, 1, half)
    wk = jnp.broadcast_to(wk, (d, SWA_KV_HEADS, 2, 2, half)).reshape(d, 2 * SWA_KV_COLS)
    wv = w_in[:, SWA_Q_COLS + SWA_KV_COLS:]
    w_in1 = jnp.concatenate([wq, wk, wv], axis=-1).astype(BF16)
    w_kv1 = jnp.concatenate([wk, wv], axis=-1).astype(BF16)
    rot_groups = (SWA_Q_COLS + 2 * SWA_KV_COLS) // LANES
    m1 = mods[1]
    z1_l = _proj(xl, 0, d, norm1_g[1], w_in1, tm=tm_i, tn=w_in1.shape[1], out_dtype=BF16,
                 mod=(m1, lat_row(tm_i), 0, 1), rope=(None, tuple(range(rot_groups))),
                 tabs=(pair_cos, pair_sin), pos_blocks=n_lat // tm_i, vmem_mib=56, name="odd_in_proj")
    z1_c = _proj(xc, 0, d, norm1_g[1], w_kv1, tm=tm, tn=w_kv1.shape[1], out_dtype=BF16,
                 mod=(m1, ctx_row, 0, 1), name="odd_ctx_kv_proj")
    att = _swa_attention(z1_l, z1_c, swa_sinks[0], batch=batch, n_lat=n_lat, n_ctx=n_ctx)
    xl = _outproj(att, 0, att, 1, w_out1, xl, m1, lat_row(tm_o), 2, tm=tm_o)
    out = _mlp(xl, norm2_g[1], m1, lat_row(tm_f), w1_all, w2_all, 1, final_g=final_norm_g, tm=tm_f)
    return out.reshape(batch, n_lat, d)
```

```python
import functools
import math

import jax
import jax.numpy as jnp
from jax import lax
from jax.experimental import pallas as pl
from jax.experimental.pallas import tpu as pltpu

F32 = jnp.float32
BF16 = jnp.bfloat16

LANES = 128
MIB = 1 << 20

GRID_W = 64
N_MOD = 6
NORM_EPS = 1e-6
ROPE_BASE = 10000.0
NEG_INF = -1e30

MLA_HEADS = 8
MLA_LORA = 512
MLA_NOPE = 128
MLA_ROPE = 64
MLA_V = 128
MLA_QK_PAD = 2 * LANES
MLA_SCALE = 1.0 / math.sqrt(MLA_NOPE + MLA_ROPE)
MLA_Q_SUB = 512

GMLP_GROUPS = 8
GMLP_DIM = 128
GMLP_CHUNK = 128
GMLP_WIDTH = GMLP_GROUPS * GMLP_DIM
GMLP_BATCH = 2

SWA_HEADS = 32
SWA_KV_HEADS = 4
SWA_GROUP = SWA_HEADS // SWA_KV_HEADS
SWA_HEAD_DIM = 64
SWA_WINDOW = 128
SWA_BLOCK = 128
SWA_SPAN = SWA_BLOCK + 2 * SWA_WINDOW
SWA_SCALE = 1.0 / math.sqrt(SWA_HEAD_DIM)
SWA_Q_COLS = SWA_HEADS * SWA_HEAD_DIM
SWA_KV_COLS = SWA_KV_HEADS * SWA_HEAD_DIM
SWA_UNIT = SWA_GROUP

MOD_ROWS = 16
ROW_CHUNK = 128
STATS_CHUNK = 256
NARROW_ROW = 512
ONES_ROWS = 16
LOG2E = math.log2(math.e)


def _params(semantics, vmem_mib):
    return pltpu.CompilerParams(dimension_semantics=semantics, vmem_limit_bytes=vmem_mib * MIB)


def _rms(x, g):
    return x * lax.rsqrt(jnp.mean(x * x, axis=-1, keepdims=True) + NORM_EPS) * g


def _for_row_chunks(n_rows, fn, rows=ROW_CHUNK):
    rows = min(rows, n_rows)

    def body(r, carry):
        fn(pl.ds(pl.multiple_of(r * rows, rows), rows))
        return carry

    lax.fori_loop(0, n_rows // rows, body, 0)


def _norm_modulate(x_ref, r_ref, store, g, scale=None, shift=None):
    n_rows = x_ref.shape[0]
    gain = g if scale is None else g * (1.0 + scale)

    if x_ref.shape[1] <= NARROW_ROW:
        def whole(sl):
            y = _rms(x_ref[sl, :], gain)
            store(sl, y if shift is None else y + shift)

        _for_row_chunks(n_rows, whole)
        return

    def stats(sl):
        x = x_ref[sl, :]
        r_ref[sl, :] = lax.rsqrt(jnp.mean(x * x, axis=-1, keepdims=True) + NORM_EPS)

    _for_row_chunks(n_rows, stats, rows=STATS_CHUNK)

    def apply(sl):
        y = x_ref[sl, :] * r_ref[sl, :] * gain
        store(sl, y if shift is None else y + shift)

    _for_row_chunks(n_rows, apply)


def _dot(a, b):
    return jnp.dot(a, b, preferred_element_type=F32)


def _dot_nt(a, b):
    return lax.dot_general(a, b, (((1,), (1,)), ((), ())), preferred_element_type=F32)


def _mod_kernel(c_ref, w_ref, b_ref, o_ref):
    c = c_ref[...]
    s = (c * jax.nn.sigmoid(c)).astype(BF16)
    o_ref[...] = _dot(s, w_ref[...].astype(BF16)) + b_ref[...]


def _modulation(cvec, w_mod, b_mod, *, tn=1024):
    depth, d, n = w_mod.shape
    return pl.pallas_call(
        _mod_kernel,
        out_shape=jax.ShapeDtypeStruct((depth, MOD_ROWS, n), F32),
        grid=(depth, n // tn),
        in_specs=[
            pl.BlockSpec((MOD_ROWS, d), lambda l, j: (0, 0)),
            pl.BlockSpec((None, d, tn), lambda l, j: (l, 0, j)),
            pl.BlockSpec((None, 1, tn), lambda l, j: (l, 0, j)),
        ],
        out_specs=pl.BlockSpec((None, MOD_ROWS, tn), lambda l, j: (l, 0, j)),
        compiler_params=_params(("parallel", "parallel"), 40),
        name="modulation",
    )(cvec, w_mod, b_mod.reshape(depth, 1, n))


def _proj_kernel(*refs, modulated, rope):
    it = iter(refs)
    x_ref, g_ref = next(it), next(it)
    sh_ref = sc_ref = None
    if modulated:
        sh_ref, sc_ref = next(it), next(it)
    w_ref = next(it)
    tabs = [next(it) for _ in range(2 if rope else 0)]
    o_ref, h_ref, r_ref = next(it), next(it), next(it)
    j = pl.program_id(1)

    @pl.when(j == 0)
    def _():
        def store(sl, y):
            h_ref[sl, :] = y.astype(BF16)

        if modulated:
            _norm_modulate(x_ref, r_ref, store, g_ref[...], sc_ref[...], sh_ref[...])
        else:
            _norm_modulate(x_ref, r_ref, store, g_ref[...])

    z = _dot(h_ref[...], w_ref[...])
    groups_per_tile = z.shape[1] // LANES

    if rope is None:
        o_ref[...] = z.astype(o_ref.dtype)
        return

    tile, groups = rope

    def rotated(t):
        return t * tabs[0][...] + pltpu.roll(t, LANES // 2, 1) * tabs[1][...]

    if tile is None:
        for gi in range(groups_per_tile):
            sl = slice(gi * LANES, (gi + 1) * LANES)
            t = z[:, sl]
            o_ref[:, sl] = (rotated(t) if gi in groups else t).astype(o_ref.dtype)
    else:
        o_ref[...] = z.astype(o_ref.dtype)

        @pl.when(j == tile)
        def _():
            for gi in groups:
                sl = slice(gi * LANES, (gi + 1) * LANES)
                o_ref[:, sl] = rotated(z[:, sl]).astype(o_ref.dtype)


def _proj(x, xcol, k, g, w, *, tm, tn, out_dtype, mod=None, rope=None, tabs=(), pos_blocks=1,
          vmem_mib=48, name="proj"):
    t = x.shape[0]
    n = w.shape[1]
    assert t % tm == 0 and n % tn == 0 and w.shape[0] == k
    in_specs = [
        pl.BlockSpec((tm, k), lambda i, j: (i, xcol)),
        pl.BlockSpec((1, k), lambda i, j: (0, 0)),
    ]
    args = [x, g.reshape(1, k)]
    if mod is not None:
        mods, row_of, k_shift, k_scale = mod
        in_specs += [
            pl.BlockSpec((None, 1, k), lambda i, j: (row_of(i), 0, k_shift)),
            pl.BlockSpec((None, 1, k), lambda i, j: (row_of(i), 0, k_scale)),
        ]
        args += [mods, mods]
    w_mode = {"pipeline_mode": pl.Buffered(1)} if n == tn else {}
    in_specs.append(pl.BlockSpec((k, tn), lambda i, j: (0, j), **w_mode))
    args.append(w)
    for tab in tabs:
        in_specs.append(pl.BlockSpec((tm, LANES), lambda i, j: (i % pos_blocks, 0)))
        args.append(tab)
    return pl.pallas_call(
        functools.partial(_proj_kernel, modulated=mod is not None, rope=rope),
        out_shape=jax.ShapeDtypeStruct((t, n), out_dtype),
        grid=(t // tm, n // tn),
        in_specs=in_specs,
        out_specs=pl.BlockSpec((tm, tn), lambda i, j: (i, j)),
        scratch_shapes=[pltpu.VMEM((tm, k), BF16), pltpu.VMEM((tm, 1), F32)],
        compiler_params=_params(("parallel", "arbitrary"), vmem_mib),
        name=name,
    )(*args)


def _even_in_kernel(x_ref, g_ref, sh_ref, sc_ref, w_ref, cos_ref, sin_ref, gq_ref, gkv_ref, wq_ref,
                    wkv_ref, wsp_ref, bsp_ref, kr_ref, q_ref, kv_ref, gm_ref, h_ref, r_ref, *, lora,
                    q_groups):
    def store(sl, y):
        h_ref[sl, :] = y.astype(BF16)

    _norm_modulate(x_ref, r_ref, store, g_ref[...], sc_ref[...], sh_ref[...])
    z = _dot(h_ref[...], w_ref[...])

    def rotated(t):
        return t * cos_ref[...] + pltpu.roll(t, LANES // 2, 1) * sin_ref[...]

    gm_lo, gm_hi = 2 * lora, 2 * lora + 2 * GMLP_WIDTH
    kr_ref[...] = rotated(z[:, gm_hi:])
    _gmlp_tiles(z[:, gm_lo:gm_lo + GMLP_WIDTH], z[:, gm_lo + GMLP_WIDTH:gm_hi], wsp_ref, bsp_ref, gm_ref)

    q = _dot(_rms(z[:, :lora], gq_ref[...]).astype(BF16), wq_ref[...])
    for gi in range(q.shape[1] // LANES):
        sl = slice(gi * LANES, (gi + 1) * LANES)
        q_ref[:, sl] = (rotated(q[:, sl]) if gi in q_groups else q[:, sl]).astype(q_ref.dtype)
    kv = _dot(_rms(z[:, lora:2 * lora], gkv_ref[...]).astype(BF16), wkv_ref[...])
    kv_ref[...] = kv.astype(kv_ref.dtype)


def _even_in(x, g, mods, row_of, w, tabs, pos_blocks, gq, gkv, wq, wkv, w_sp, b_sp, *, tm, lora,
             q_groups, vmem_mib=58):
    t, d = x.shape
    n = w.shape[1]
    assert n == 2 * lora + 2 * GMLP_WIDTH + LANES
    once = {"pipeline_mode": pl.Buffered(1)}
    row_blk = lambda i: (i, 0)
    fixed = lambda i: (0, 0)
    return pl.pallas_call(
        functools.partial(_even_in_kernel, lora=lora, q_groups=q_groups),
        out_shape=[jax.ShapeDtypeStruct((t, LANES), F32),
                   jax.ShapeDtypeStruct((t, wq.shape[1]), BF16),
                   jax.ShapeDtypeStruct((t, wkv.shape[1]), BF16),
                   jax.ShapeDtypeStruct((t, GMLP_WIDTH), BF16)],
        grid=(t // tm,),
        in_specs=[
            pl.BlockSpec((tm, d), row_blk),
            pl.BlockSpec((1, d), fixed),
            pl.BlockSpec((None, 1, d), lambda i: (row_of(i), 0, 0)),
            pl.BlockSpec((None, 1, d), lambda i: (row_of(i), 0, 1)),
            pl.BlockSpec((d, n), fixed, **once),
            pl.BlockSpec((tm, LANES), lambda i: (i % pos_blocks, 0)),
            pl.BlockSpec((tm, LANES), lambda i: (i % pos_blocks, 0)),
            pl.BlockSpec((1, lora), fixed),
            pl.BlockSpec((1, lora), fixed),
            pl.BlockSpec(wq.shape, fixed, **once),
            pl.BlockSpec(wkv.shape, fixed, **once),
            pl.BlockSpec(w_sp.shape, lambda i: (0, 0, 0)),
            pl.BlockSpec(b_sp.shape, lambda i: (0, 0, 0)),
        ],
        out_specs=[pl.BlockSpec((tm, LANES), row_blk),
                   pl.BlockSpec((tm, wq.shape[1]), row_blk),
                   pl.BlockSpec((tm, wkv.shape[1]), row_blk),
                   pl.BlockSpec((tm, GMLP_WIDTH), row_blk)],
        scratch_shapes=[pltpu.VMEM((tm, d), BF16), pltpu.VMEM((tm, 1), F32)],
        compiler_params=_params(("parallel",), vmem_mib),
        name="even_in_proj",
    )(x, g.reshape(1, d), mods, mods, w, tabs[0], tabs[1], gq.reshape(1, lora), gkv.reshape(1, lora),
      wq, wkv, w_sp, b_sp)


def _mla_kernel(*refs, n_lat, n_ctx, n_cast):
    refs = list(refs)
    n_in = 7 if n_lat else 4
    cast_in = refs[n_in:n_in + n_cast]
    cast_out = refs[n_in + n_cast + 1:n_in + 2 * n_cast + 1]
    del refs[n_in + n_cast + 1:n_in + 2 * n_cast + 1], refs[n_in:n_in + n_cast]
    if n_lat:
        q_ref, knl_ref, vl_ref, krl_ref, knc_ref, vc_ref, krc_ref, o_ref, k_scr, vt_scr = refs
    else:
        q_ref, knc_ref, vc_ref, krc_ref, o_ref, k_scr, vt_scr = refs
    for src, dst in zip(cast_in, cast_out):
        dst[...] = src[...].astype(dst.dtype)

    @pl.when(pl.program_id(2) == 0)
    def _():
        if n_lat:
            k_scr[0:n_lat, 0:LANES] = knl_ref[...]
            k_scr[0:n_lat, LANES:] = krl_ref[...].astype(BF16)
            vt_scr[0:MLA_V, 0:n_lat] = vl_ref[...].astype(F32).T.astype(BF16)
        k_scr[n_lat:, 0:LANES] = knc_ref[...]
        k_scr[n_lat:, LANES:] = krc_ref[...].astype(BF16)
        vt_scr[0:MLA_V, n_lat:] = vc_ref[...].astype(F32).T.astype(BF16)
        vt_scr[MLA_V:, :] = jnp.ones((ONES_ROWS, n_lat + n_ctx), BF16)

    n_sub = min(MLA_Q_SUB, q_ref.shape[0])
    subs = [slice(r, r + n_sub) for r in range(0, q_ref.shape[0], n_sub)]
    def scores(i):
        return _dot_nt(k_scr[...], q_ref[subs[i], :])

    def weights(s):
        return jnp.exp2(s - jnp.max(s, axis=0, keepdims=True)).astype(BF16)

    n = len(subs)
    s = [scores(i) if i < 2 else None for i in range(n)]
    p = weights(s[0])
    for i, rows in enumerate(subs):
        if i + 2 < n:
            s[i + 2] = scores(i + 2)
        o = _dot(vt_scr[...], p)
        if i + 1 < n:
            p = weights(s[i + 1])
        o_ref[rows, :] = (o[0:MLA_V] / o[MLA_V:MLA_V + 1]).T.astype(o_ref.dtype)


def _mla_attention(q, kv_c, z_c, kr_col, *, batch, n_q, n_ctx, kv_l=None, z_l=None, tq=256, casts=()):
    n_lat = 0 if kv_l is None else kv_l.shape[0] // batch
    nq_blocks = n_q // tq
    in_specs = [pl.BlockSpec((tq, MLA_QK_PAD), lambda b, h, i: (b * nq_blocks + i, h))]
    args = [q]
    if n_lat:
        in_specs += [
            pl.BlockSpec((n_lat, LANES), lambda b, h, i: (b, 2 * h)),
            pl.BlockSpec((n_lat, LANES), lambda b, h, i: (b, 2 * h + 1)),
            pl.BlockSpec((n_lat, LANES), lambda b, h, i: (b, kr_col)),
        ]
        args += [kv_l, kv_l, z_l]
    in_specs += [
        pl.BlockSpec((n_ctx, LANES), lambda b, h, i: (b, 2 * h)),
        pl.BlockSpec((n_ctx, LANES), lambda b, h, i: (b, 2 * h + 1)),
        pl.BlockSpec((n_ctx, LANES), lambda b, h, i: (b, kr_col)),
    ]
    args += [kv_c, kv_c, z_c]
    n_keys = n_lat + n_ctx
    n_steps = batch * MLA_HEADS * nq_blocks
    step_of = lambda b, h, i: ((b * MLA_HEADS + h) * nq_blocks + i, 0, 0)
    out_shape = [jax.ShapeDtypeStruct((batch * n_q, MLA_HEADS * MLA_V), BF16)]
    out_specs = [pl.BlockSpec((tq, MLA_V), lambda b, h, i: (b * nq_blocks + i, h))]
    for w in casts:
        assert w.shape[0] == n_steps
        slab = pl.BlockSpec((None,) + w.shape[1:], step_of)
        in_specs.append(slab)
        args.append(w)
        out_specs.append(slab)
        out_shape.append(jax.ShapeDtypeStruct(w.shape, BF16))
    outs = pl.pallas_call(
        functools.partial(_mla_kernel, n_lat=n_lat, n_ctx=n_ctx, n_cast=len(casts)),
        out_shape=out_shape,
        grid=(batch, MLA_HEADS, nq_blocks),
        in_specs=in_specs,
        out_specs=out_specs,
        scratch_shapes=[pltpu.VMEM((n_keys, MLA_QK_PAD), BF16),
                        pltpu.VMEM((MLA_V + ONES_ROWS, n_keys), BF16)],
        compiler_params=_params(("parallel", "parallel", "arbitrary"), 56),
        name="mla_attention",
    )(*args)
    return outs[0], outs[1:]


def _gelu(x):
    c = math.sqrt(2.0 / math.pi)
    return x * (0.5 + 0.5 * jnp.tanh(x * (c + (c * 0.044715) * (x * x))))


def _gmlp_tiles(u, v, w_ref, b_ref, o_ref):
    for n0 in range(0, u.shape[0] // GMLP_CHUNK, GMLP_BATCH):
        tiles = [(slice(n * GMLP_CHUNK, (n + 1) * GMLP_CHUNK), slice(g * GMLP_DIM, (g + 1) * GMLP_DIM), g)
                 for n in range(n0, n0 + GMLP_BATCH) for g in range(GMLP_GROUPS)]
        x = [_gelu(v[rows, cols]) for rows, cols, _ in tiles]
        xc = [t - jnp.mean(t, axis=-1, keepdims=True) for t in x]
        var = [jnp.mean(t * t, axis=-1, keepdims=True) for t in xc]
        xn = [(t * lax.rsqrt(s + NORM_EPS)).astype(BF16) for t, s in zip(xc, var)]
        for (rows, cols, g), t in zip(tiles, xn):
            mixed = _dot(w_ref[g], t) + b_ref[g]
            o_ref[rows, cols] = (_gelu(u[rows, cols]) * mixed).astype(o_ref.dtype)


def _swa_kernel(sink_ref, q_ref, kl_ref, vl_ref, kc_ref, vc_ref, o_ref, *, tq, n_lat):
    start = pl.program_id(1) * tq
    span0 = pl.multiple_of(jnp.clip(start - SWA_WINDOW, 0, n_lat - SWA_SPAN), SWA_WINDOW)
    k_span = kl_ref[pl.ds(span0, SWA_SPAN), :]
    k_ctx = kc_ref[...]
    vt_span = vl_ref[pl.ds(span0, SWA_SPAN), :].astype(F32).T.astype(BF16)
    vt_ctx = vc_ref[...].astype(F32).T.astype(BF16)
    ones_span = jnp.ones((ONES_ROWS, vt_span.shape[1]), BF16)
    ones_ctx = jnp.ones((ONES_ROWS, vt_ctx.shape[1]), BF16)

    k_pos = span0 + lax.broadcasted_iota(jnp.int32, (SWA_SPAN, tq), 0)
    q_pos = start + lax.broadcasted_iota(jnp.int32, (SWA_SPAN, tq), 1)
    bias = jnp.where(jnp.abs(k_pos - q_pos) <= SWA_WINDOW, 0.0, NEG_INF).astype(F32)
    bias = jnp.concatenate([bias] * SWA_UNIT, axis=1)

    lane = lax.broadcasted_iota(jnp.int32, (tq, LANES), 1)
    first = (lane % SWA_HEAD_DIM) < (SWA_HEAD_DIM // 2)
    keep = [jnp.where(first, 1.0, 0.0).astype(BF16), jnp.where(first, 0.0, 1.0).astype(BF16)]

    units = [range(h0, h0 + SWA_UNIT) for h0 in range(0, SWA_HEADS, SWA_UNIT)]

    def scores(heads):
        g = heads[0] // SWA_GROUP
        k_cols = slice(g * LANES, (g + 1) * LANES)
        q = jnp.concatenate(
            [q_ref[:, (h // 2) * LANES:(h // 2 + 1) * LANES] * keep[h % 2] for h in heads], axis=0)
        return _dot_nt(k_span[:, k_cols], q) + bias, _dot_nt(k_ctx[:, k_cols], q)

    s_next = scores(units[0])
    for u, heads in enumerate(units):
        g = heads[0] // SWA_GROUP
        kv_cols = slice(g * SWA_HEAD_DIM, (g + 1) * SWA_HEAD_DIM)
        s_l, s_c = s_next
        if u + 1 < len(units):
            s_next = scores(units[u + 1])
        sink = jnp.concatenate([jnp.full((1, tq), sink_ref[h] * LOG2E, F32) for h in heads], axis=1)
        m = jnp.maximum(jnp.maximum(jnp.max(s_l, axis=0, keepdims=True),
                                    jnp.max(s_c, axis=0, keepdims=True)), sink)
        p_l = jnp.exp2(s_l - m).astype(BF16)
        p_c = jnp.exp2(s_c - m).astype(BF16)
        vt_l = jnp.concatenate([vt_span[kv_cols, :], ones_span], axis=0)
        vt_c = jnp.concatenate([vt_ctx[kv_cols, :], ones_ctx], axis=0)
        o = _dot(vt_l, p_l) + _dot(vt_c, p_c)
        dim = SWA_HEAD_DIM
        o = o[0:dim] / (o[dim:dim + 1] + jnp.exp2(sink - m))
        for pair in range(SWA_UNIT // 2):
            cols = [slice((2 * pair + r) * tq, (2 * pair + r + 1) * tq) for r in range(2)]
            both = jnp.concatenate([o[:, cols[0]], o[:, cols[1]]], axis=0)
            group = heads[0] // 2 + pair
            o_ref[:, group * LANES:(group + 1) * LANES] = both.T.astype(o_ref.dtype)


def _swa_attention(z_l, z_c, sinks, *, batch, n_lat, n_ctx, tq=SWA_BLOCK):
    nq_blocks = n_lat // tq
    k_cols = 2 * SWA_KV_COLS
    return pl.pallas_call(
        functools.partial(_swa_kernel, tq=tq, n_lat=n_lat),
        out_shape=jax.ShapeDtypeStruct((batch * n_lat, SWA_Q_COLS), BF16),
        grid=(batch, nq_blocks),
        in_specs=[
            pl.BlockSpec(memory_space=pltpu.SMEM),
            pl.BlockSpec((tq, SWA_Q_COLS), lambda b, i: (b * nq_blocks + i, 0)),
            pl.BlockSpec((n_lat, k_cols), lambda b, i: (b, SWA_Q_COLS // k_cols)),
            pl.BlockSpec((n_lat, SWA_KV_COLS), lambda b, i: (b, (SWA_Q_COLS + k_cols) // SWA_KV_COLS)),
            pl.BlockSpec((n_ctx, k_cols), lambda b, i: (b, 0)),
            pl.BlockSpec((n_ctx, SWA_KV_COLS), lambda b, i: (b, k_cols // SWA_KV_COLS)),
        ],
        out_specs=pl.BlockSpec((tq, SWA_Q_COLS), lambda b, i: (b * nq_blocks + i, 0)),
        compiler_params=_params(("parallel", "arbitrary"), 48),
        name="swa_attention",
    )(sinks, z_l, z_l, z_l, z_c, z_c)


def _outproj_kernel(a1_ref, a2_ref, w1_ref, w2_ref, x_ref, gate_ref, o_ref):
    y = _dot(a1_ref[...], w1_ref[...]) + _dot(a2_ref[...], w2_ref[...])
    o_ref[...] = x_ref[...] + gate_ref[...] * y


def _outproj(a1, a1_col, a2, a2_col, w, x, mods, row_of, k_gate, *, tm=512):
    t, d = x.shape
    half = w.shape[0] // 2
    return pl.pallas_call(
        _outproj_kernel,
        out_shape=jax.ShapeDtypeStruct((t, d), F32),
        grid=(t // tm,),
        in_specs=[
            pl.BlockSpec((tm, half), lambda i: (i, a1_col)),
            pl.BlockSpec((tm, half), lambda i: (i, a2_col)),
            pl.BlockSpec((half, d), lambda i: (0, 0)),
            pl.BlockSpec((half, d), lambda i: (1, 0)),
            pl.BlockSpec((tm, d), lambda i: (i, 0)),
            pl.BlockSpec((None, 1, d), lambda i: (row_of(i), 0, k_gate)),
        ],
        out_specs=pl.BlockSpec((tm, d), lambda i: (i, 0)),
        compiler_params=_params(("parallel",), 48),
        name="outproj",
    )(a1, a2, w, w, x, mods)


def _mlp_kernel(*refs, final):
    if final:
        x_ref, g_ref, sh_ref, sc_ref, gate_ref, w1_ref, w2_ref, fg_ref, o_ref, h_ref, r_ref = refs
    else:
        x_ref, g_ref, sh_ref, sc_ref, gate_ref, w1_ref, w2_ref, o_ref, h_ref, r_ref = refs
    k = pl.program_id(1)

    def ff_chunk():
        a = jnp.square(jnp.maximum(_dot(h_ref[...], w1_ref[...]), 0.0)).astype(BF16)
        return _dot(a, w2_ref[...])

    @pl.when(k == 0)
    def _():
        def store(sl, y):
            h_ref[sl, :] = y.astype(BF16)

        _norm_modulate(x_ref, r_ref, store, g_ref[...], sc_ref[...], sh_ref[...])
        o_ref[...] = ff_chunk()

    @pl.when(k > 0)
    def _():
        o_ref[...] += ff_chunk()

    @pl.when(k == pl.num_programs(1) - 1)
    def _():
        def rows(sl):
            out = x_ref[sl, :] + gate_ref[...] * o_ref[sl, :]
            if final:
                out = _rms(out, fg_ref[...])
            o_ref[sl, :] = out

        _for_row_chunks(x_ref.shape[0], rows)


def _mlp(x, g, mods, row_of, w1, w2, layer, *, final_g=None, tm=512, tf=1024, vmem_mib=56):
    t, d = x.shape
    n_k = w1.shape[2] // tf
    in_specs = [
        pl.BlockSpec((tm, d), lambda i, k: (i, 0)),
        pl.BlockSpec((1, d), lambda i, k: (0, 0)),
        pl.BlockSpec((None, 1, d), lambda i, k: (row_of(i), 0, 3)),
        pl.BlockSpec((None, 1, d), lambda i, k: (row_of(i), 0, 4)),
        pl.BlockSpec((None, 1, d), lambda i, k: (row_of(i), 0, 5)),
        pl.BlockSpec((None, d, tf), lambda i, k: (layer, 0, k)),
        pl.BlockSpec((None, tf, d), lambda i, k: (layer, k, 0)),
    ]
    args = [x, g.reshape(1, d), mods, mods, mods, w1, w2]
    if final_g is not None:
        in_specs.append(pl.BlockSpec((1, d), lambda i, k: (0, 0)))
        args.append(final_g.reshape(1, d))
    return pl.pallas_call(
        functools.partial(_mlp_kernel, final=final_g is not None),
        out_shape=jax.ShapeDtypeStruct((t, d), F32),
        grid=(t // tm, n_k),
        in_specs=in_specs,
        out_specs=pl.BlockSpec((tm, d), lambda i, k: (i, 0)),
        scratch_shapes=[pltpu.VMEM((tm, d), BF16), pltpu.VMEM((tm, 1), F32)],
        compiler_params=_params(("parallel", "arbitrary"), vmem_mib),
        name="mlp",
    )(*args)


def _axial_angles(n_tokens, rot_dim):
    t = jnp.arange(n_tokens)
    row = (t // GRID_W).astype(F32)
    col = (t % GRID_W).astype(F32)
    n_freq = rot_dim // 4
    inv_freq = ROPE_BASE ** (-jnp.arange(n_freq, dtype=F32) / n_freq)
    ang = jnp.concatenate([row[:, None] * inv_freq, col[:, None] * inv_freq], axis=-1)
    return jnp.cos(ang), jnp.sin(ang)


def _swap_halves(w):
    half = w.shape[-1] // 2
    return jnp.concatenate([w[..., half:], w[..., :half]], axis=-1)


def kernel(x, c, ctx, c_ctx, norm1_g, w_mod, b_mod, norm2_g, w_ff1, w_ff2, even_w_in, mla_q_norm_g,
           mla_w_uq, mla_kv_norm_g, mla_w_ukv, gmlp_w_sp, gmlp_b_sp, even_w_out, odd_w_in, swa_sinks,
           odd_w_out, final_norm_g):
    batch, n_lat, d = x.shape
    n_ctx = ctx.shape[1]
    assert batch + 1 <= MOD_ROWS
    xl = x.reshape(batch * n_lat, d)
    xc = ctx.reshape(batch * n_ctx, d)

    tm = 1024

    def lat_row(block_rows):
        per_sample = n_lat // block_rows
        return lambda i: i // per_sample

    ctx_row = lambda i: batch

    cvec = jnp.concatenate([c, c_ctx[None, :], jnp.zeros((MOD_ROWS - batch - 1, d), F32)], axis=0)
    mods = _modulation(cvec, w_mod, b_mod)
    mods = mods.reshape(mods.shape[0], MOD_ROWS, 1, N_MOD * d)

    cos_m, sin_m = _axial_angles(n_lat, MLA_ROPE)
    zeros64 = jnp.zeros((n_lat, LANES // 2), F32)
    fold_cos = jnp.concatenate([cos_m, cos_m, zeros64], axis=-1)
    fold_sin = jnp.concatenate([-sin_m, sin_m, zeros64], axis=-1)
    keep = jnp.concatenate([jnp.ones((tm, LANES // 2), F32), jnp.zeros((tm, LANES // 2), F32)], axis=-1)
    drop = jnp.zeros((tm, LANES), F32)

    w_in = even_w_in[0]
    cq_w, ckv_w = w_in[:, :MLA_LORA], w_in[:, MLA_LORA:2 * MLA_LORA]
    kr_w = w_in[:, 2 * MLA_LORA:2 * MLA_LORA + MLA_ROPE]
    gm_w = w_in[:, 2 * MLA_LORA + MLA_ROPE:]
    w_in0 = jnp.concatenate([cq_w, ckv_w, gm_w, kr_w, _swap_halves(kr_w)], axis=-1).astype(BF16)
    kr_group = 0

    wq = mla_w_uq[0].reshape(MLA_LORA, MLA_HEADS, MLA_NOPE + MLA_ROPE)
    wq_rope = wq[..., MLA_NOPE:]
    w_uq = jnp.concatenate([wq[..., :MLA_NOPE], wq_rope, _swap_halves(wq_rope)], axis=-1)
    w_uq = (w_uq * (MLA_SCALE * LOG2E)).reshape(MLA_LORA, MLA_HEADS * MLA_QK_PAD).astype(BF16)
    q_groups = tuple(range(1, 2 * MLA_HEADS, 2))
    w_ukv = mla_w_ukv[0].astype(BF16)
    w_sp = gmlp_w_sp[0].astype(BF16)
    b_sp = gmlp_b_sp[0][:, :, None]
    tm_i = 512

    def layer0_tokens(xs, tabs, latent):
        row_of = lat_row(tm_i) if latent else ctx_row
        pos_blocks = n_lat // tm_i if latent else 1
        return _even_in(xs, norm1_g[0], mods[0], row_of, w_in0, tabs, pos_blocks, mla_q_norm_g[0],
                        mla_kv_norm_g[0], w_uq, w_ukv, w_sp, b_sp, tm=tm_i, lora=MLA_LORA,
                        q_groups=q_groups)

    z_l, q_l, kv_l, gm_l = layer0_tokens(xl, (fold_cos, fold_sin), True)
    z_c, q_c, kv_c, gm_c = layer0_tokens(xc, (keep, drop), False)
    cast_steps = batch * MLA_HEADS
    late = (w_ff1, w_ff2, even_w_out, odd_w_out)
    slabs = [w.reshape(cast_steps, -1, w.shape[-1]) for w in late]
    att_l, late_bf16 = _mla_attention(q_l, kv_c, z_c, kr_group, batch=batch, n_q=n_lat, n_ctx=n_ctx,
                                      kv_l=kv_l, z_l=z_l, tq=n_lat, casts=slabs)
    w1_all, w2_all, w_out0, w_out1 = [b.reshape(w.shape) for b, w in zip(late_bf16, late)]
    w_out0, w_out1 = w_out0[0], w_out1[0]
    att_c, _ = _mla_attention(q_c, kv_c, z_c, kr_group, batch=batch, n_q=n_ctx, n_ctx=n_ctx)

    tm_o = 512
    xl = _outproj(att_l, 0, gm_l, 0, w_out0, xl, mods[0], lat_row(tm_o), 2, tm=tm_o)
    xc = _outproj(att_c, 0, gm_c, 0, w_out0, xc, mods[0], ctx_row, 2, tm=tm_o)
    tm_f = 512
    xl = _mlp(xl, norm2_g[0], mods[0], lat_row(tm_f), w1_all, w2_all, 0, tm=tm_f)
    xc = _mlp(xc, norm2_g[0], mods[0], ctx_row, w1_all, w2_all, 0, tm=tm_f)

    cos_s, sin_s = _axial_angles(n_lat, SWA_HEAD_DIM)
    pair_cos = jnp.concatenate([cos_s] * 4, axis=-1)
    pair_sin = jnp.concatenate([-sin_s, -sin_s, sin_s, sin_s], axis=-1)
    half = SWA_HEAD_DIM // 2
    w_in = odd_w_in[0]
    wq = (w_in[:, :SWA_Q_COLS] * (SWA_SCALE * LOG2E)).reshape(d, SWA_HEADS // 2, 2, 2, half)
    wq = wq.transpose(0, 1, 3, 2, 4).reshape(d, SWA_Q_COLS)
    wk = w_in[:, SWA_Q_COLS:SWA_Q_COLS + SWA_KV_COLS].reshape(d, SWA_KV_HEADS, 2, 1, half)
    wk = jnp.broadcast_to(wk, (d, SWA_KV_HEADS, 2, 2, half)).reshape(d, 2 * SWA_KV_COLS)
    wv = w_in[:, SWA_Q_COLS + SWA_KV_COLS:]
    w_in1 = jnp.concatenate([wq, wk, wv], axis=-1).astype(BF16)
    w_kv1 = jnp.concatenate([wk, wv], axis=-1).astype(BF16)
    rot_groups = (SWA_Q_COLS + 2 * SWA_KV_COLS) // LANES
    m1 = mods[1]
    z1_l = _proj(xl, 0, d, norm1_g[1], w_in1, tm=tm_i, tn=w_in1.shape[1], out_dtype=BF16,
                 mod=(m1, lat_row(tm_i), 0, 1), rope=(None, tuple(range(rot_groups))),
                 tabs=(pair_cos, pair_sin), pos_blocks=n_lat // tm_i, vmem_mib=56, name="odd_in_proj")
    z1_c = _proj(xc, 0, d, norm1_g[1], w_kv1, tm=tm, tn=w_kv1.shape[1], out_dtype=BF16,
                 mod=(m1, ctx_row, 0, 1), name="odd_ctx_kv_proj")
    att = _swa_attention(z1_l, z1_c, swa_sinks[0], batch=batch, n_lat=n_lat, n_ctx=n_ctx)
    xl = _outproj(att, 0, att, 1, w_out1, xl, m1, lat_row(tm_o), 2, tm=tm_o)
    out = _mlp(xl, norm2_g[1], m1, lat_row(tm_f), w1_all, w2_all, 1, final_g=final_norm_g, tm=tm_f)
    return out.reshape(batch, n_lat, d)
```

```python
import functools
import math

import jax
import jax.numpy as jnp
from jax import lax
from jax.experimental import pallas as pl
from jax.experimental.pallas import tpu as pltpu

F32 = jnp.float32
BF16 = jnp.bfloat16

LANES = 128
MIB = 1 << 20

GRID_W = 64
N_MOD = 6
NORM_EPS = 1e-6
ROPE_BASE = 10000.0
NEG_INF = -1e30

MLA_HEADS = 8
MLA_LORA = 512
MLA_NOPE = 128
MLA_ROPE = 64
MLA_V = 128
MLA_QK_PAD = 2 * LANES
MLA_SCALE = 1.0 / math.sqrt(MLA_NOPE + MLA_ROPE)
MLA_Q_SUB = 512

GMLP_GROUPS = 8
GMLP_DIM = 128
GMLP_CHUNK = 128
GMLP_WIDTH = GMLP_GROUPS * GMLP_DIM
GMLP_BATCH = 2

SWA_HEADS = 32
SWA_KV_HEADS = 4
SWA_GROUP = SWA_HEADS // SWA_KV_HEADS
SWA_HEAD_DIM = 64
SWA_WINDOW = 128
SWA_BLOCK = 128
SWA_SPAN = SWA_BLOCK + 2 * SWA_WINDOW
SWA_SCALE = 1.0 / math.sqrt(SWA_HEAD_DIM)
SWA_Q_COLS = SWA_HEADS * SWA_HEAD_DIM
SWA_KV_COLS = SWA_KV_HEADS * SWA_HEAD_DIM
SWA_UNIT = SWA_GROUP

MOD_ROWS = 16
ROW_CHUNK = 128
STATS_CHUNK = 256
NARROW_ROW = 512
ONES_ROWS = 16
LOG2E = math.log2(math.e)


def _params(semantics, vmem_mib):
    return pltpu.CompilerParams(dimension_semantics=semantics, vmem_limit_bytes=vmem_mib * MIB)


def _rms(x, g):
    return x * lax.rsqrt(jnp.mean(x * x, axis=-1, keepdims=True) + NORM_EPS) * g


def _for_row_chunks(n_rows, fn, rows=ROW_CHUNK):
    rows = min(rows, n_rows)

    def body(r, carry):
        fn(pl.ds(pl.multiple_of(r * rows, rows), rows))
        return carry

    lax.fori_loop(0, n_rows // rows, body, 0)


def _norm_modulate(x_ref, r_ref, store, g, scale=None, shift=None):
    n_rows = x_ref.shape[0]
    gain = g if scale is None else g * (1.0 + scale)

    if x_ref.shape[1] <= NARROW_ROW:
        def whole(sl):
            y = _rms(x_ref[sl, :], gain)
            store(sl, y if shift is None else y + shift)

        _for_row_chunks(n_rows, whole)
        return

    def stats(sl):
        x = x_ref[sl, :]
        r_ref[sl, :] = lax.rsqrt(jnp.mean(x * x, axis=-1, keepdims=True) + NORM_EPS)

    _for_row_chunks(n_rows, stats, rows=STATS_CHUNK)

    def apply(sl):
        y = x_ref[sl, :] * r_ref[sl, :] * gain
        store(sl, y if shift is None else y + shift)

    _for_row_chunks(n_rows, apply)


def _dot(a, b):
    return jnp.dot(a, b, preferred_element_type=F32)


def _dot_nt(a, b):
    return lax.dot_general(a, b, (((1,), (1,)), ((), ())), preferred_element_type=F32)


def _mod_kernel(c_ref, w_ref, b_ref, o_ref):
    c = c_ref[...]
    s = (c * jax.nn.sigmoid(c)).astype(BF16)
    o_ref[...] = _dot(s, w_ref[...].astype(BF16)) + b_ref[...]


def _modulation(cvec, w_mod, b_mod, *, tn=1024):
    depth, d, n = w_mod.shape
    return pl.pallas_call(
        _mod_kernel,
        out_shape=jax.ShapeDtypeStruct((depth, MOD_ROWS, n), F32),
        grid=(depth, n // tn),
        in_specs=[
            pl.BlockSpec((MOD_ROWS, d), lambda l, j: (0, 0)),
            pl.BlockSpec((None, d, tn), lambda l, j: (l, 0, j)),
            pl.BlockSpec((None, 1, tn), lambda l, j: (l, 0, j)),
        ],
        out_specs=pl.BlockSpec((None, MOD_ROWS, tn), lambda l, j: (l, 0, j)),
        compiler_params=_params(("parallel", "parallel"), 40),
        name="modulation",
    )(cvec, w_mod, b_mod.reshape(depth, 1, n))


def _proj_kernel(*refs, modulated, rope):
    it = iter(refs)
    x_ref, g_ref = next(it), next(it)
    sh_ref = sc_ref = None
    if modulated:
        sh_ref, sc_ref = next(it), next(it)
    w_ref = next(it)
    tabs = [next(it) for _ in range(2 if rope else 0)]
    o_ref, h_ref, r_ref = next(it), next(it), next(it)
    j = pl.program_id(1)

    @pl.when(j == 0)
    def _():
        def store(sl, y):
            h_ref[sl, :] = y.astype(BF16)

        if modulated:
            _norm_modulate(x_ref, r_ref, store, g_ref[...], sc_ref[...], sh_ref[...])
        else:
            _norm_modulate(x_ref, r_ref, store, g_ref[...])

    z = _dot(h_ref[...], w_ref[...])
    groups_per_tile = z.shape[1] // LANES

    if rope is None:
        o_ref[...] = z.astype(o_ref.dtype)
        return

    tile, groups = rope

    def rotated(t):
        return t * tabs[0][...] + pltpu.roll(t, LANES // 2, 1) * tabs[1][...]

    if tile is None:
        for gi in range(groups_per_tile):
            sl = slice(gi * LANES, (gi + 1) * LANES)
            t = z[:, sl]
            o_ref[:, sl] = (rotated(t) if gi in groups else t).astype(o_ref.dtype)
    else:
        o_ref[...] = z.astype(o_ref.dtype)

        @pl.when(j == tile)
        def _():
            for gi in groups:
                sl = slice(gi * LANES, (gi + 1) * LANES)
                o_ref[:, sl] = rotated(z[:, sl]).astype(o_ref.dtype)


def _proj(x, xcol, k, g, w, *, tm, tn, out_dtype, mod=None, rope=None, tabs=(), pos_blocks=1,
          vmem_mib=48, name="proj"):
    t = x.shape[0]
    n = w.shape[1]
    assert t % tm == 0 and n % tn == 0 and w.shape[0] == k
    in_specs = [
        pl.BlockSpec((tm, k), lambda i, j: (i, xcol)),
        pl.BlockSpec((1, k), lambda i, j: (0, 0)),
    ]
    args = [x, g.reshape(1, k)]
    if mod is not None:
        mods, row_of, k_shift, k_scale = mod
        in_specs += [
            pl.BlockSpec((None, 1, k), lambda i, j: (row_of(i), 0, k_shift)),
            pl.BlockSpec((None, 1, k), lambda i, j: (row_of(i), 0, k_scale)),
        ]
        args += [mods, mods]
    w_mode = {"pipeline_mode": pl.Buffered(1)} if n == tn else {}
    in_specs.append(pl.BlockSpec((k, tn), lambda i, j: (0, j), **w_mode))
    args.append(w)
    for tab in tabs:
        in_specs.append(pl.BlockSpec((tm, LANES), lambda i, j: (i % pos_blocks, 0)))
        args.append(tab)
    return pl.pallas_call(
        functools.partial(_proj_kernel, modulated=mod is not None, rope=rope),
        out_shape=jax.ShapeDtypeStruct((t, n), out_dtype),
        grid=(t // tm, n // tn),
        in_specs=in_specs,
        out_specs=pl.BlockSpec((tm, tn), lambda i, j: (i, j)),
        scratch_shapes=[pltpu.VMEM((tm, k), BF16), pltpu.VMEM((tm, 1), F32)],
        compiler_params=_params(("parallel", "arbitrary"), vmem_mib),
        name=name,
    )(*args)


def _even_in_kernel(x_ref, g_ref, sh_ref, sc_ref, w_ref, cos_ref, sin_ref, gq_ref, gkv_ref, wq_ref,
                    wkv_ref, wsp_ref, bsp_ref, kr_ref, q_ref, kv_ref, gm_ref, h_ref, r_ref, *, lora,
                    q_groups):
    def store(sl, y):
        h_ref[sl, :] = y.astype(BF16)

    _norm_modulate(x_ref, r_ref, store, g_ref[...], sc_ref[...], sh_ref[...])
    z = _dot(h_ref[...], w_ref[...])

    def rotated(t):
        return t * cos_ref[...] + pltpu.roll(t, LANES // 2, 1) * sin_ref[...]

    gm_lo, gm_hi = 2 * lora, 2 * lora + 2 * GMLP_WIDTH
    kr_ref[...] = rotated(z[:, gm_hi:])
    _gmlp_tiles(z[:, gm_lo:gm_lo + GMLP_WIDTH], z[:, gm_lo + GMLP_WIDTH:gm_hi], wsp_ref, bsp_ref, gm_ref)

    q = _dot(_rms(z[:, :lora], gq_ref[...]).astype(BF16), wq_ref[...])
    for gi in range(q.shape[1] // LANES):
        sl = slice(gi * LANES, (gi + 1) * LANES)
        q_ref[:, sl] = (rotated(q[:, sl]) if gi in q_groups else q[:, sl]).astype(q_ref.dtype)
    kv = _dot(_rms(z[:, lora:2 * lora], gkv_ref[...]).astype(BF16), wkv_ref[...])
    kv_ref[...] = kv.astype(kv_ref.dtype)


def _even_in(x, g, mods, row_of, w, tabs, pos_blocks, gq, gkv, wq, wkv, w_sp, b_sp, *, tm, lora,
             q_groups, vmem_mib=58):
    t, d = x.shape
    n = w.shape[1]
    assert n == 2 * lora + 2 * GMLP_WIDTH + LANES
    once = {"pipeline_mode": pl.Buffered(1)}
    row_blk = lambda i: (i, 0)
    fixed = lambda i: (0, 0)
    return pl.pallas_call(
        functools.partial(_even_in_kernel, lora=lora, q_groups=q_groups),
        out_shape=[jax.ShapeDtypeStruct((t, LANES), F32),
                   jax.ShapeDtypeStruct((t, wq.shape[1]), BF16),
                   jax.ShapeDtypeStruct((t, wkv.shape[1]), BF16),
                   jax.ShapeDtypeStruct((t, GMLP_WIDTH), BF16)],
        grid=(t // tm,),
        in_specs=[
            pl.BlockSpec((tm, d), row_blk),
            pl.BlockSpec((1, d), fixed),
            pl.BlockSpec((None, 1, d), lambda i: (row_of(i), 0, 0)),
            pl.BlockSpec((None, 1, d), lambda i: (row_of(i), 0, 1)),
            pl.BlockSpec((d, n), fixed, **once),
            pl.BlockSpec((tm, LANES), lambda i: (i % pos_blocks, 0)),
            pl.BlockSpec((tm, LANES), lambda i: (i % pos_blocks, 0)),
            pl.BlockSpec((1, lora), fixed),
            pl.BlockSpec((1, lora), fixed),
            pl.BlockSpec(wq.shape, fixed, **once),
            pl.BlockSpec(wkv.shape, fixed, **once),
            pl.BlockSpec(w_sp.shape, lambda i: (0, 0, 0)),
            pl.BlockSpec(b_sp.shape, lambda i: (0, 0, 0)),
        ],
        out_specs=[pl.BlockSpec((tm, LANES), row_blk),
                   pl.BlockSpec((tm, wq.shape[1]), row_blk),
                   pl.BlockSpec((tm, wkv.shape[1]), row_blk),
                   pl.BlockSpec((tm, GMLP_WIDTH), row_blk)],
        scratch_shapes=[pltpu.VMEM((tm, d), BF16), pltpu.VMEM((tm, 1), F32)],
        compiler_params=_params(("parallel",), vmem_mib),
        name="even_in_proj",
    )(x, g.reshape(1, d), mods, mods, w, tabs[0], tabs[1], gq.reshape(1, lora), gkv.reshape(1, lora),
      wq, wkv, w_sp, b_sp)


def _mla_kernel(*refs, n_lat, n_ctx, n_cast):
    n_in = 8
    q_ref, qc_ref, knl_ref, vl_ref, krl_ref, knc_ref, vc_ref, krc_ref = refs[:n_in]
    cast_in = refs[n_in:n_in + n_cast]
    o_ref, oc_ref = refs[n_in + n_cast:n_in + n_cast + 2]
    cast_out = refs[n_in + n_cast + 2:n_in + 2 * n_cast + 2]
    k_scr, vt_scr = refs[n_in + 2 * n_cast + 2:]
    for src, dst in zip(cast_in, cast_out):
        dst[...] = src[...].astype(dst.dtype)

    k_scr[0:n_lat, 0:LANES] = knl_ref[...]
    k_scr[0:n_lat, LANES:] = krl_ref[...].astype(BF16)
    vt_scr[0:MLA_V, 0:n_lat] = vl_ref[...].astype(F32).T.astype(BF16)
    k_scr[n_lat:, 0:LANES] = knc_ref[...]
    k_scr[n_lat:, LANES:] = krc_ref[...].astype(BF16)
    vt_scr[0:MLA_V, n_lat:] = vc_ref[...].astype(F32).T.astype(BF16)
    vt_scr[MLA_V:, :] = jnp.ones((ONES_ROWS, n_lat + n_ctx), BF16)

    def weights(s):
        return jnp.exp2(s - jnp.max(s, axis=0, keepdims=True)).astype(BF16)

    def normalised(o):
        return (o[0:MLA_V] / o[MLA_V:MLA_V + 1]).T

    subs = [slice(r, r + MLA_Q_SUB) for r in range(0, n_lat, MLA_Q_SUB)]

    def scores(i):
        return _dot_nt(k_scr[...], q_ref[subs[i], :])

    n = len(subs)
    s = [scores(i) if i < 2 else None for i in range(n)]
    p = weights(s[0])
    for i, rows in enumerate(subs):
        if i + 2 < n:
            s[i + 2] = scores(i + 2)
        o = _dot(vt_scr[...], p)
        if i + 1 < n:
            p = weights(s[i + 1])
        o_ref[rows, :] = normalised(o).astype(o_ref.dtype)

    p_c = weights(_dot_nt(k_scr[n_lat:, :], qc_ref[...]))
    oc_ref[...] = normalised(_dot(vt_scr[:, n_lat:], p_c)).astype(oc_ref.dtype)


def _mla_attention(q_l, q_c, kv_l, kv_c, kr_l, kr_c, *, batch, casts=()):
    n_lat, n_ctx = q_l.shape[0] // batch, q_c.shape[0] // batch
    assert n_lat % MLA_Q_SUB == 0
    head = lambda b, h: (b, h)
    k_of = lambda b, h: (b, 2 * h)
    v_of = lambda b, h: (b, 2 * h + 1)
    shared = lambda b, h: (b, 0)
    in_specs = [
        pl.BlockSpec((n_lat, MLA_QK_PAD), head),
        pl.BlockSpec((n_ctx, MLA_QK_PAD), head),
        pl.BlockSpec((n_lat, LANES), k_of),
        pl.BlockSpec((n_lat, LANES), v_of),
        pl.BlockSpec((n_lat, LANES), shared),
        pl.BlockSpec((n_ctx, LANES), k_of),
        pl.BlockSpec((n_ctx, LANES), v_of),
        pl.BlockSpec((n_ctx, LANES), shared),
    ]
    args = [q_l, q_c, kv_l, kv_l, kr_l, kv_c, kv_c, kr_c]
    out_shape = [jax.ShapeDtypeStruct((batch * n_lat, MLA_HEADS * MLA_V), BF16),
                 jax.ShapeDtypeStruct((batch * n_ctx, MLA_HEADS * MLA_V), BF16)]
    out_specs = [pl.BlockSpec((n_lat, MLA_V), head), pl.BlockSpec((n_ctx, MLA_V), head)]
    for w in casts:
        assert w.shape[0] == batch * MLA_HEADS
        slab = pl.BlockSpec((None,) + w.shape[1:], lambda b, h: (b * MLA_HEADS + h, 0, 0))
        in_specs.append(slab)
        args.append(w)
        out_specs.append(slab)
        out_shape.append(jax.ShapeDtypeStruct(w.shape, BF16))
    n_keys = n_lat + n_ctx
    outs = pl.pallas_call(
        functools.partial(_mla_kernel, n_lat=n_lat, n_ctx=n_ctx, n_cast=len(casts)),
        out_shape=out_shape,
        grid=(batch, MLA_HEADS),
        in_specs=in_specs,
        out_specs=out_specs,
        scratch_shapes=[pltpu.VMEM((n_keys, MLA_QK_PAD), BF16),
                        pltpu.VMEM((MLA_V + ONES_ROWS, n_keys), BF16)],
        compiler_params=_params(("parallel", "parallel"), 56),
        name="mla_attention",
    )(*args)
    return outs[0], outs[1], outs[2:]


def _gelu(x):
    c = math.sqrt(2.0 / math.pi)
    return x * (0.5 + 0.5 * jnp.tanh(x * (c + (c * 0.044715) * (x * x))))


def _gmlp_tiles(u, v, w_ref, b_ref, o_ref):
    for n0 in range(0, u.shape[0] // GMLP_CHUNK, GMLP_BATCH):
        tiles = [(slice(n * GMLP_CHUNK, (n + 1) * GMLP_CHUNK), slice(g * GMLP_DIM, (g + 1) * GMLP_DIM), g)
                 for n in range(n0, n0 + GMLP_BATCH) for g in range(GMLP_GROUPS)]
        x = [_gelu(v[rows, cols]) for rows, cols, _ in tiles]
        xc = [t - jnp.mean(t, axis=-1, keepdims=True) for t in x]
        var = [jnp.mean(t * t, axis=-1, keepdims=True) for t in xc]
        xn = [(t * lax.rsqrt(s + NORM_EPS)).astype(BF16) for t, s in zip(xc, var)]
        for (rows, cols, g), t in zip(tiles, xn):
            mixed = _dot(w_ref[g], t) + b_ref[g]
            o_ref[rows, cols] = (_gelu(u[rows, cols]) * mixed).astype(o_ref.dtype)


def _swa_kernel(sink_ref, q_ref, kl_ref, vl_ref, kc_ref, vc_ref, o_ref, *, tq, n_lat):
    start = pl.program_id(1) * tq
    span0 = pl.multiple_of(jnp.clip(start - SWA_WINDOW, 0, n_lat - SWA_SPAN), SWA_WINDOW)
    k_span = kl_ref[pl.ds(span0, SWA_SPAN), :]
    k_ctx = kc_ref[...]
    vt_span = vl_ref[pl.ds(span0, SWA_SPAN), :].astype(F32).T.astype(BF16)
    vt_ctx = vc_ref[...].astype(F32).T.astype(BF16)
    ones_span = jnp.ones((ONES_ROWS, vt_span.shape[1]), BF16)
    ones_ctx = jnp.ones((ONES_ROWS, vt_ctx.shape[1]), BF16)

    k_pos = span0 + lax.broadcasted_iota(jnp.int32, (SWA_SPAN, tq), 0)
    q_pos = start + lax.broadcasted_iota(jnp.int32, (SWA_SPAN, tq), 1)
    bias = jnp.where(jnp.abs(k_pos - q_pos) <= SWA_WINDOW, 0.0, NEG_INF).astype(F32)
    bias = jnp.concatenate([bias] * SWA_UNIT, axis=1)

    lane = lax.broadcasted_iota(jnp.int32, (tq, LANES), 1)
    first = (lane % SWA_HEAD_DIM) < (SWA_HEAD_DIM // 2)
    keep = [jnp.where(first, 1.0, 0.0).astype(BF16), jnp.where(first, 0.0, 1.0).astype(BF16)]

    units = [range(h0, h0 + SWA_UNIT) for h0 in range(0, SWA_HEADS, SWA_UNIT)]

    def scores(heads):
        g = heads[0] // SWA_GROUP
        k_cols = slice(g * LANES, (g + 1) * LANES)
        q = jnp.concatenate(
            [q_ref[:, (h // 2) * LANES:(h // 2 + 1) * LANES] * keep[h % 2] for h in heads], axis=0)
        return _dot_nt(k_span[:, k_cols], q) + bias, _dot_nt(k_ctx[:, k_cols], q)

    s_next = scores(units[0])
    for u, heads in enumerate(units):
        g = heads[0] // SWA_GROUP
        kv_cols = slice(g * SWA_HEAD_DIM, (g + 1) * SWA_HEAD_DIM)
        s_l, s_c = s_next
        if u + 1 < len(units):
            s_next = scores(units[u + 1])
        sink = jnp.concatenate([jnp.full((1, tq), sink_ref[h] * LOG2E, F32) for h in heads], axis=1)
        m = jnp.maximum(jnp.maximum(jnp.max(s_l, axis=0, keepdims=True),
                                    jnp.max(s_c, axis=0, keepdims=True)), sink)
        p_l = jnp.exp2(s_l - m).astype(BF16)
        p_c = jnp.exp2(s_c - m).astype(BF16)
        vt_l = jnp.concatenate([vt_span[kv_cols, :], ones_span], axis=0)
        vt_c = jnp.concatenate([vt_ctx[kv_cols, :], ones_ctx], axis=0)
        o = _dot(vt_l, p_l) + _dot(vt_c, p_c)
        dim = SWA_HEAD_DIM
        o = o[0:dim] / (o[dim:dim + 1] + jnp.exp2(sink - m))
        for pair in range(SWA_UNIT // 2):
            cols = [slice((2 * pair + r) * tq, (2 * pair + r + 1) * tq) for r in range(2)]
            both = jnp.concatenate([o[:, cols[0]], o[:, cols[1]]], axis=0)
            group = heads[0] // 2 + pair
            o_ref[:, group * LANES:(group + 1) * LANES] = both.T.astype(o_ref.dtype)


def _swa_attention(z_l, z_c, sinks, *, batch, n_lat, n_ctx, tq=SWA_BLOCK):
    nq_blocks = n_lat // tq
    k_cols = 2 * SWA_KV_COLS
    return pl.pallas_call(
        functools.partial(_swa_kernel, tq=tq, n_lat=n_lat),
        out_shape=jax.ShapeDtypeStruct((batch * n_lat, SWA_Q_COLS), BF16),
        grid=(batch, nq_blocks),
        in_specs=[
            pl.BlockSpec(memory_space=pltpu.SMEM),
            pl.BlockSpec((tq, SWA_Q_COLS), lambda b, i: (b * nq_blocks + i, 0)),
            pl.BlockSpec((n_lat, k_cols), lambda b, i: (b, SWA_Q_COLS // k_cols)),
            pl.BlockSpec((n_lat, SWA_KV_COLS), lambda b, i: (b, (SWA_Q_COLS + k_cols) // SWA_KV_COLS)),
            pl.BlockSpec((n_ctx, k_cols), lambda b, i: (b, 0)),
            pl.BlockSpec((n_ctx, SWA_KV_COLS), lambda b, i: (b, k_cols // SWA_KV_COLS)),
        ],
        out_specs=pl.BlockSpec((tq, SWA_Q_COLS), lambda b, i: (b * nq_blocks + i, 0)),
        compiler_params=_params(("parallel", "arbitrary"), 48),
        name="swa_attention",
    )(sinks, z_l, z_l, z_l, z_c, z_c)


def _outproj_kernel(a1_ref, a2_ref, w1_ref, w2_ref, x_ref, gate_ref, o_ref):
    y = _dot(a1_ref[...], w1_ref[...]) + _dot(a2_ref[...], w2_ref[...])
    o_ref[...] = x_ref[...] + gate_ref[...] * y


def _outproj(a1, a1_col, a2, a2_col, w, x, mods, row_of, k_gate, *, tm=512):
    t, d = x.shape
    half = w.shape[0] // 2
    return pl.pallas_call(
        _outproj_kernel,
        out_shape=jax.ShapeDtypeStruct((t, d), F32),
        grid=(t // tm,),
        in_specs=[
            pl.BlockSpec((tm, half), lambda i: (i, a1_col)),
            pl.BlockSpec((tm, half), lambda i: (i, a2_col)),
            pl.BlockSpec((half, d), lambda i: (0, 0)),
            pl.BlockSpec((half, d), lambda i: (1, 0)),
            pl.BlockSpec((tm, d), lambda i: (i, 0)),
            pl.BlockSpec((None, 1, d), lambda i: (row_of(i), 0, k_gate)),
        ],
        out_specs=pl.BlockSpec((tm, d), lambda i: (i, 0)),
        compiler_params=_params(("parallel",), 48),
        name="outproj",
    )(a1, a2, w, w, x, mods)


def _mlp_kernel(*refs, final):
    if final:
        x_ref, g_ref, sh_ref, sc_ref, gate_ref, w1_ref, w2_ref, fg_ref, o_ref, h_ref, r_ref = refs
    else:
        x_ref, g_ref, sh_ref, sc_ref, gate_ref, w1_ref, w2_ref, o_ref, h_ref, r_ref = refs
    k = pl.program_id(1)

    def ff_chunk():
        a = jnp.square(jnp.maximum(_dot(h_ref[...], w1_ref[...]), 0.0)).astype(BF16)
        return _dot(a, w2_ref[...])

    @pl.when(k == 0)
    def _():
        def store(sl, y):
            h_ref[sl, :] = y.astype(BF16)

        _norm_modulate(x_ref, r_ref, store, g_ref[...], sc_ref[...], sh_ref[...])
        o_ref[...] = ff_chunk()

    @pl.when(k > 0)
    def _():
        o_ref[...] += ff_chunk()

    @pl.when(k == pl.num_programs(1) - 1)
    def _():
        def rows(sl):
            out = x_ref[sl, :] + gate_ref[...] * o_ref[sl, :]
            if final:
                out = _rms(out, fg_ref[...])
            o_ref[sl, :] = out

        _for_row_chunks(x_ref.shape[0], rows)


def _mlp(x, g, mods, row_of, w1, w2, layer, *, final_g=None, tm=512, tf=1024, vmem_mib=56):
    t, d = x.shape
    n_k = w1.shape[2] // tf
    in_specs = [
        pl.BlockSpec((tm, d), lambda i, k: (i, 0)),
        pl.BlockSpec((1, d), lambda i, k: (0, 0)),
        pl.BlockSpec((None, 1, d), lambda i, k: (row_of(i), 0, 3)),
        pl.BlockSpec((None, 1, d), lambda i, k: (row_of(i), 0, 4)),
        pl.BlockSpec((None, 1, d), lambda i, k: (row_of(i), 0, 5)),
        pl.BlockSpec((None, d, tf), lambda i, k: (layer, 0, k)),
        pl.BlockSpec((None, tf, d), lambda i, k: (layer, k, 0)),
    ]
    args = [x, g.reshape(1, d), mods, mods, mods, w1, w2]
    if final_g is not None:
        in_specs.append(pl.BlockSpec((1, d), lambda i, k: (0, 0)))
        args.append(final_g.reshape(1, d))
    return pl.pallas_call(
        functools.partial(_mlp_kernel, final=final_g is not None),
        out_shape=jax.ShapeDtypeStruct((t, d), F32),
        grid=(t // tm, n_k),
        in_specs=in_specs,
        out_specs=pl.BlockSpec((tm, d), lambda i, k: (i, 0)),
        scratch_shapes=[pltpu.VMEM((tm, d), BF16), pltpu.VMEM((tm, 1), F32)],
        compiler_params=_params(("parallel", "arbitrary"), vmem_mib),
        name="mlp",
    )(*args)


def _axial_angles(n_tokens, rot_dim):
    t = jnp.arange(n_tokens)
    row = (t // GRID_W).astype(F32)
    col = (t % GRID_W).astype(F32)
    n_freq = rot_dim // 4
    inv_freq = ROPE_BASE ** (-jnp.arange(n_freq, dtype=F32) / n_freq)
    ang = jnp.concatenate([row[:, None] * inv_freq, col[:, None] * inv_freq], axis=-1)
    return jnp.cos(ang), jnp.sin(ang)


def _swap_halves(w):
    half = w.shape[-1] // 2
    return jnp.concatenate([w[..., half:], w[..., :half]], axis=-1)


def kernel(x, c, ctx, c_ctx, norm1_g, w_mod, b_mod, norm2_g, w_ff1, w_ff2, even_w_in, mla_q_norm_g,
           mla_w_uq, mla_kv_norm_g, mla_w_ukv, gmlp_w_sp, gmlp_b_sp, even_w_out, odd_w_in, swa_sinks,
           odd_w_out, final_norm_g):
    batch, n_lat, d = x.shape
    n_ctx = ctx.shape[1]
    assert batch + 1 <= MOD_ROWS
    xl = x.reshape(batch * n_lat, d)
    xc = ctx.reshape(batch * n_ctx, d)

    tm = 1024

    def lat_row(block_rows):
        per_sample = n_lat // block_rows
        return lambda i: i // per_sample

    ctx_row = lambda i: batch

    cvec = jnp.concatenate([c, c_ctx[None, :], jnp.zeros((MOD_ROWS - batch - 1, d), F32)], axis=0)
    mods = _modulation(cvec, w_mod, b_mod)
    mods = mods.reshape(mods.shape[0], MOD_ROWS, 1, N_MOD * d)

    cos_m, sin_m = _axial_angles(n_lat, MLA_ROPE)
    zeros64 = jnp.zeros((n_lat, LANES // 2), F32)
    fold_cos = jnp.concatenate([cos_m, cos_m, zeros64], axis=-1)
    fold_sin = jnp.concatenate([-sin_m, sin_m, zeros64], axis=-1)
    keep = jnp.concatenate([jnp.ones((tm, LANES // 2), F32), jnp.zeros((tm, LANES // 2), F32)], axis=-1)
    drop = jnp.zeros((tm, LANES), F32)

    w_in = even_w_in[0]
    cq_w, ckv_w = w_in[:, :MLA_LORA], w_in[:, MLA_LORA:2 * MLA_LORA]
    kr_w = w_in[:, 2 * MLA_LORA:2 * MLA_LORA + MLA_ROPE]
    gm_w = w_in[:, 2 * MLA_LORA + MLA_ROPE:]
    w_in0 = jnp.concatenate([cq_w, ckv_w, gm_w, kr_w, _swap_halves(kr_w)], axis=-1).astype(BF16)

    wq = mla_w_uq[0].reshape(MLA_LORA, MLA_HEADS, MLA_NOPE + MLA_ROPE)
    wq_rope = wq[..., MLA_NOPE:]
    w_uq = jnp.concatenate([wq[..., :MLA_NOPE], wq_rope, _swap_halves(wq_rope)], axis=-1)
    w_uq = (w_uq * (MLA_SCALE * LOG2E)).reshape(MLA_LORA, MLA_HEADS * MLA_QK_PAD).astype(BF16)
    q_groups = tuple(range(1, 2 * MLA_HEADS, 2))
    w_ukv = mla_w_ukv[0].astype(BF16)
    w_sp = gmlp_w_sp[0].astype(BF16)
    b_sp = gmlp_b_sp[0][:, :, None]
    tm_i = 512

    def layer0_tokens(xs, tabs, latent):
        row_of = lat_row(tm_i) if latent else ctx_row
        pos_blocks = n_lat // tm_i if latent else 1
        return _even_in(xs, norm1_g[0], mods[0], row_of, w_in0, tabs, pos_blocks, mla_q_norm_g[0],
                        mla_kv_norm_g[0], w_uq, w_ukv, w_sp, b_sp, tm=tm_i, lora=MLA_LORA,
                        q_groups=q_groups)

    kr_l, q_l, kv_l, gm_l = layer0_tokens(xl, (fold_cos, fold_sin), True)
    kr_c, q_c, kv_c, gm_c = layer0_tokens(xc, (keep, drop), False)
    cast_steps = batch * MLA_HEADS
    late = (w_ff1, w_ff2, even_w_out, odd_w_out)
    slabs = [w.reshape(cast_steps, -1, w.shape[-1]) for w in late]
    att_l, att_c, late_bf16 = _mla_attention(q_l, q_c, kv_l, kv_c, kr_l, kr_c, batch=batch, casts=slabs)
    w1_all, w2_all, w_out0, w_out1 = [b.reshape(w.shape) for b, w in zip(late_bf16, late)]
    w_out0, w_out1 = w_out0[0], w_out1[0]

    tm_o = 512
    xl = _outproj(att_l, 0, gm_l, 0, w_out0, xl, mods[0], lat_row(tm_o), 2, tm=tm_o)
    xc = _outproj(att_c, 0, gm_c, 0, w_out0, xc, mods[0], ctx_row, 2, tm=tm_o)
    tm_f = 512
    xl = _mlp(xl, norm2_g[0], mods[0], lat_row(tm_f), w1_all, w2_all, 0, tm=tm_f)
    xc = _mlp(xc, norm2_g[0], mods[0], ctx_row, w1_all, w2_all, 0, tm=tm_f)

    cos_s, sin_s = _axial_angles(n_lat, SWA_HEAD_DIM)
    pair_cos = jnp.concatenate([cos_s] * 4, axis=-1)
    pair_sin = jnp.concatenate([-sin_s, -sin_s, sin_s, sin_s], axis=-1)
    half = SWA_HEAD_DIM // 2
    w_in = odd_w_in[0]
    wq = (w_in[:, :SWA_Q_COLS] * (SWA_SCALE * LOG2E)).reshape(d, SWA_HEADS // 2, 2, 2, half)
    wq = wq.transpose(0, 1, 3, 2, 4).reshape(d, SWA_Q_COLS)
    wk = w_in[:, SWA_Q_COLS:SWA_Q_COLS + SWA_KV_COLS].reshape(d, SWA_KV_HEADS, 2, 1, half)
    wk = jnp.broadcast_to(wk, (d, SWA_KV_HEADS, 2, 2, half)).reshape(d, 2 * SWA_KV_COLS)
    wv = w_in[:, SWA_Q_COLS + SWA_KV_COLS:]
    w_in1 = jnp.concatenate([wq, wk, wv], axis=-1).astype(BF16)
    w_kv1 = jnp.concatenate([wk, wv], axis=-1).astype(BF16)
    rot_groups = (SWA_Q_COLS + 2 * SWA_KV_COLS) // LANES
    m1 = mods[1]
    z1_l = _proj(xl, 0, d, norm1_g[1], w_in1, tm=tm_i, tn=w_in1.shape[1], out_dtype=BF16,
                 mod=(m1, lat_row(tm_i), 0, 1), rope=(None, tuple(range(rot_groups))),
                 tabs=(pair_cos, pair_sin), pos_blocks=n_lat // tm_i, vmem_mib=56, name="odd_in_proj")
    z1_c = _proj(xc, 0, d, norm1_g[1], w_kv1, tm=tm, tn=w_kv1.shape[1], out_dtype=BF16,
                 mod=(m1, ctx_row, 0, 1), name="odd_ctx_kv_proj")
    att = _swa_attention(z1_l, z1_c, swa_sinks[0], batch=batch, n_lat=n_lat, n_ctx=n_ctx)
    xl = _outproj(att, 0, att, 1, w_out1, xl, m1, lat_row(tm_o), 2, tm=tm_o)
    out = _mlp(xl, norm2_g[1], m1, lat_row(tm_f), w1_all, w2_all, 1, final_g=final_norm_g, tm=tm_f)
    return out.reshape(batch, n_lat, d)
```

```python
import functools
import math

import jax
import jax.numpy as jnp
from jax import lax
from jax.experimental import pallas as pl
from jax.experimental.pallas import tpu as pltpu

F32 = jnp.float32
BF16 = jnp.bfloat16

LANES = 128
MIB = 1 << 20

GRID_W = 64
N_MOD = 6
NORM_EPS = 1e-6
ROPE_BASE = 10000.0
NEG_INF = -1e30

MLA_HEADS = 8
MLA_LORA = 512
MLA_NOPE = 128
MLA_ROPE = 64
MLA_V = 128
MLA_QK_PAD = 2 * LANES
MLA_SCALE = 1.0 / math.sqrt(MLA_NOPE + MLA_ROPE)
MLA_Q_SUB = 512

GMLP_GROUPS = 8
GMLP_DIM = 128
GMLP_CHUNK = 128
GMLP_WIDTH = GMLP_GROUPS * GMLP_DIM
GMLP_BATCH = 2

SWA_HEADS = 32
SWA_KV_HEADS = 4
SWA_GROUP = SWA_HEADS // SWA_KV_HEADS
SWA_HEAD_DIM = 64
SWA_WINDOW = 128
SWA_BLOCK = 128
SWA_SPAN = SWA_BLOCK + 2 * SWA_WINDOW
SWA_SCALE = 1.0 / math.sqrt(SWA_HEAD_DIM)
SWA_Q_COLS = SWA_HEADS * SWA_HEAD_DIM
SWA_KV_COLS = SWA_KV_HEADS * SWA_HEAD_DIM
SWA_UNIT = SWA_GROUP

MOD_ROWS = 16
ROW_CHUNK = 128
STATS_CHUNK = 256
NARROW_ROW = 512
AHEAD_ROWS = 64
ONES_ROWS = 16
LOG2E = math.log2(math.e)


def _params(semantics, vmem_mib):
    return pltpu.CompilerParams(dimension_semantics=semantics, vmem_limit_bytes=vmem_mib * MIB)


def _rms(x, g):
    return x * lax.rsqrt(jnp.mean(x * x, axis=-1, keepdims=True) + NORM_EPS) * g


def _for_row_chunks(n_rows, fn, rows=ROW_CHUNK):
    rows = min(rows, n_rows)

    def body(r, carry):
        fn(pl.ds(pl.multiple_of(r * rows, rows), rows))
        return carry

    lax.fori_loop(0, n_rows // rows, body, 0)


def _norm_modulate(x_ref, r_ref, store, g, scale=None, shift=None):
    n_rows = x_ref.shape[0]
    gain = g if scale is None else g * (1.0 + scale)

    if x_ref.shape[1] <= NARROW_ROW:
        def whole(sl):
            y = _rms(x_ref[sl, :], gain)
            store(sl, y if shift is None else y + shift)

        _for_row_chunks(n_rows, whole)
        return

    def stats(sl):
        x = x_ref[sl, :]
        r_ref[sl, :] = lax.rsqrt(jnp.mean(x * x, axis=-1, keepdims=True) + NORM_EPS)

    _for_row_chunks(n_rows, stats, rows=STATS_CHUNK)

    def apply(sl):
        y = x_ref[sl, :] * r_ref[sl, :] * gain
        store(sl, y if shift is None else y + shift)

    _for_row_chunks(n_rows, apply)


def _first_block_norm(x_ref, r_ref, h_ref, g, scale, shift):
    @pl.when(pl.program_id(0) == 0)
    def _():
        def store(sl, y):
            h_ref[0, sl, :] = y.astype(BF16)

        _norm_modulate(x_ref, r_ref, store, g, scale, shift)


def _next_block_norm(xn_ref, h_ref, slot, g, scale, shift):
    gain = g * (1.0 + scale)
    for r0 in range(0, xn_ref.shape[0], AHEAD_ROWS):
        rows = slice(r0, r0 + AHEAD_ROWS)
        x = xn_ref[rows, :]
        r = lax.rsqrt(jnp.mean(x * x, axis=-1, keepdims=True) + NORM_EPS)
        h_ref[slot, rows, :] = (x * r * gain + shift).astype(BF16)


def _dot(a, b):
    return jnp.dot(a, b, preferred_element_type=F32)


def _dot_nt(a, b):
    return lax.dot_general(a, b, (((1,), (1,)), ((), ())), preferred_element_type=F32)


def _mod_kernel(c_ref, w_ref, b_ref, o_ref):
    c = c_ref[...]
    s = (c * jax.nn.sigmoid(c)).astype(BF16)
    o_ref[...] = _dot(s, w_ref[...].astype(BF16)) + b_ref[...]


def _modulation(cvec, w_mod, b_mod, *, tn=1024):
    depth, d, n = w_mod.shape
    return pl.pallas_call(
        _mod_kernel,
        out_shape=jax.ShapeDtypeStruct((depth, MOD_ROWS, n), F32),
        grid=(depth, n // tn),
        in_specs=[
            pl.BlockSpec((MOD_ROWS, d), lambda l, j: (0, 0)),
            pl.BlockSpec((None, d, tn), lambda l, j: (l, 0, j)),
            pl.BlockSpec((None, 1, tn), lambda l, j: (l, 0, j)),
        ],
        out_specs=pl.BlockSpec((None, MOD_ROWS, tn), lambda l, j: (l, 0, j)),
        compiler_params=_params(("parallel", "parallel"), 40),
        name="modulation",
    )(cvec, w_mod, b_mod.reshape(depth, 1, n))


def _proj_kernel(*refs, rot_groups):
    if rot_groups:
        (x_ref, xn_ref, g_ref, sh_ref, sc_ref, shn_ref, scn_ref, w_ref, cos_ref, sin_ref, o_ref, h_ref,
         r_ref) = refs
    else:
        x_ref, xn_ref, g_ref, sh_ref, sc_ref, shn_ref, scn_ref, w_ref, o_ref, h_ref, r_ref = refs
    cur = pl.program_id(0) % 2
    _first_block_norm(x_ref, r_ref, h_ref, g_ref[...], sc_ref[...], sh_ref[...])
    z = _dot(h_ref[cur], w_ref[...])
    _next_block_norm(xn_ref, h_ref, 1 - cur, g_ref[...], scn_ref[...], shn_ref[...])

    for gi in range(z.shape[1] // LANES):
        sl = slice(gi * LANES, (gi + 1) * LANES)
        t = z[:, sl]
        if gi < rot_groups:
            t = t * cos_ref[...] + pltpu.roll(t, LANES // 2, 1) * sin_ref[...]
        o_ref[:, sl] = t.astype(o_ref.dtype)


def _proj(x, g, mods, row_of, w, *, tm, rot_groups=0, tabs=(), pos_blocks=1, vmem_mib=48,
          name="proj"):
    t, k = x.shape
    n = w.shape[1]
    n_i = t // tm
    nxt = lambda i: jnp.minimum(i + 1, n_i - 1)
    in_specs = [
        pl.BlockSpec((tm, k), lambda i: (i, 0)),
        pl.BlockSpec((tm, k), lambda i: (nxt(i), 0)),
        pl.BlockSpec((1, k), lambda i: (0, 0)),
        pl.BlockSpec((None, 1, k), lambda i: (row_of(i), 0, 0)),
        pl.BlockSpec((None, 1, k), lambda i: (row_of(i), 0, 1)),
        pl.BlockSpec((None, 1, k), lambda i: (row_of(nxt(i)), 0, 0)),
        pl.BlockSpec((None, 1, k), lambda i: (row_of(nxt(i)), 0, 1)),
        pl.BlockSpec((k, n), lambda i: (0, 0), pipeline_mode=pl.Buffered(1)),
    ]
    in_specs += [pl.BlockSpec((tm, LANES), lambda i: (i % pos_blocks, 0)) for _ in tabs]
    return pl.pallas_call(
        functools.partial(_proj_kernel, rot_groups=rot_groups),
        out_shape=jax.ShapeDtypeStruct((t, n), BF16),
        grid=(n_i,),
        in_specs=in_specs,
        out_specs=pl.BlockSpec((tm, n), lambda i: (i, 0)),
        scratch_shapes=[pltpu.VMEM((2, tm, k), BF16), pltpu.VMEM((tm, 1), F32)],
        compiler_params=_params(("arbitrary",), vmem_mib),
        name=name,
    )(x, x, g.reshape(1, k), mods, mods, mods, mods, w, *tabs)


def _even_in_kernel(x_ref, xn_ref, g_ref, sh_ref, sc_ref, shn_ref, scn_ref, w_ref, cos_ref, sin_ref,
                    gq_ref, gkv_ref, wq_ref, wkv_ref, wsp_ref, bsp_ref, kr_ref, q_ref, kv_ref, gm_ref,
                    h_ref, r_ref, *, lora, q_groups):
    cur = pl.program_id(0) % 2
    _first_block_norm(x_ref, r_ref, h_ref, g_ref[...], sc_ref[...], sh_ref[...])
    z = _dot(h_ref[cur], w_ref[...])
    _next_block_norm(xn_ref, h_ref, 1 - cur, g_ref[...], scn_ref[...], shn_ref[...])

    def rotated(t):
        return t * cos_ref[...] + pltpu.roll(t, LANES // 2, 1) * sin_ref[...]

    gm_lo, gm_hi = 2 * lora, 2 * lora + 2 * GMLP_WIDTH
    kr_ref[...] = rotated(z[:, gm_hi:])
    _gmlp_tiles(z[:, gm_lo:gm_lo + GMLP_WIDTH], z[:, gm_lo + GMLP_WIDTH:gm_hi], wsp_ref, bsp_ref, gm_ref)

    q = _dot(_rms(z[:, :lora], gq_ref[...]).astype(BF16), wq_ref[...])
    for gi in range(q.shape[1] // LANES):
        sl = slice(gi * LANES, (gi + 1) * LANES)
        q_ref[:, sl] = (rotated(q[:, sl]) if gi in q_groups else q[:, sl]).astype(q_ref.dtype)
    kv = _dot(_rms(z[:, lora:2 * lora], gkv_ref[...]).astype(BF16), wkv_ref[...])
    kv_ref[...] = kv.astype(kv_ref.dtype)


def _even_in(x, g, mods, row_of, w, tabs, pos_blocks, gq, gkv, wq, wkv, w_sp, b_sp, *, tm, lora,
             q_groups, vmem_mib=58):
    t, d = x.shape
    n = w.shape[1]
    assert n == 2 * lora + 2 * GMLP_WIDTH + LANES
    once = {"pipeline_mode": pl.Buffered(1)}
    row_blk = lambda i: (i, 0)
    fixed = lambda i: (0, 0)
    n_i = t // tm
    nxt = lambda i: jnp.minimum(i + 1, n_i - 1)
    return pl.pallas_call(
        functools.partial(_even_in_kernel, lora=lora, q_groups=q_groups),
        out_shape=[jax.ShapeDtypeStruct((t, LANES), F32),
                   jax.ShapeDtypeStruct((t, wq.shape[1]), BF16),
                   jax.ShapeDtypeStruct((t, wkv.shape[1]), BF16),
                   jax.ShapeDtypeStruct((t, GMLP_WIDTH), BF16)],
        grid=(n_i,),
        in_specs=[
            pl.BlockSpec((tm, d), row_blk),
            pl.BlockSpec((tm, d), lambda i: (nxt(i), 0)),
            pl.BlockSpec((1, d), fixed),
            pl.BlockSpec((None, 1, d), lambda i: (row_of(i), 0, 0)),
            pl.BlockSpec((None, 1, d), lambda i: (row_of(i), 0, 1)),
            pl.BlockSpec((None, 1, d), lambda i: (row_of(nxt(i)), 0, 0)),
            pl.BlockSpec((None, 1, d), lambda i: (row_of(nxt(i)), 0, 1)),
            pl.BlockSpec((d, n), fixed, **once),
            pl.BlockSpec((tm, LANES), lambda i: (i % pos_blocks, 0)),
            pl.BlockSpec((tm, LANES), lambda i: (i % pos_blocks, 0)),
            pl.BlockSpec((1, lora), fixed),
            pl.BlockSpec((1, lora), fixed),
            pl.BlockSpec(wq.shape, fixed, **once),
            pl.BlockSpec(wkv.shape, fixed, **once),
            pl.BlockSpec(w_sp.shape, lambda i: (0, 0, 0)),
            pl.BlockSpec(b_sp.shape, lambda i: (0, 0, 0)),
        ],
        out_specs=[pl.BlockSpec((tm, LANES), row_blk),
                   pl.BlockSpec((tm, wq.shape[1]), row_blk),
                   pl.BlockSpec((tm, wkv.shape[1]), row_blk),
                   pl.BlockSpec((tm, GMLP_WIDTH), row_blk)],
        scratch_shapes=[pltpu.VMEM((2, tm, d), BF16), pltpu.VMEM((tm, 1), F32)],
        compiler_params=_params(("arbitrary",), vmem_mib),
        name="even_in_proj",
    )(x, x, g.reshape(1, d), mods, mods, mods, mods, w, tabs[0], tabs[1], gq.reshape(1, lora),
      gkv.reshape(1, lora), wq, wkv, w_sp, b_sp)


def _mla_kernel(*refs, n_lat, n_ctx, n_cast):
    n_in = 8
    q_ref, qc_ref, knl_ref, vl_ref, krl_ref, knc_ref, vc_ref, krc_ref = refs[:n_in]
    cast_in = refs[n_in:n_in + n_cast]
    o_ref, oc_ref = refs[n_in + n_cast:n_in + n_cast + 2]
    cast_out = refs[n_in + n_cast + 2:n_in + 2 * n_cast + 2]
    k_scr, vt_scr = refs[n_in + 2 * n_cast + 2:]
    for src, dst in zip(cast_in, cast_out):
        dst[...] = src[...].astype(dst.dtype)

    k_scr[0:n_lat, 0:LANES] = knl_ref[...]
    k_scr[0:n_lat, LANES:] = krl_ref[...].astype(BF16)
    vt_scr[0:MLA_V, 0:n_lat] = vl_ref[...].astype(F32).T.astype(BF16)
    k_scr[n_lat:, 0:LANES] = knc_ref[...]
    k_scr[n_lat:, LANES:] = krc_ref[...].astype(BF16)
    vt_scr[0:MLA_V, n_lat:] = vc_ref[...].astype(F32).T.astype(BF16)
    vt_scr[MLA_V:, :] = jnp.ones((ONES_ROWS, n_lat + n_ctx), BF16)

    def weights(s):
        return jnp.exp2(s - jnp.max(s, axis=0, keepdims=True)).astype(BF16)

    def normalised(o):
        return (o[0:MLA_V] / o[MLA_V:MLA_V + 1]).T

    subs = [slice(r, r + MLA_Q_SUB) for r in range(0, n_lat, MLA_Q_SUB)]

    def scores(i):
        return _dot_nt(k_scr[...], q_ref[subs[i], :])

    n = len(subs)
    s = [scores(i) if i < 2 else None for i in range(n)]
    p = weights(s[0])
    for i, rows in enumerate(subs):
        if i + 2 < n:
            s[i + 2] = scores(i + 2)
        o = _dot(vt_scr[...], p)
        if i + 1 < n:
            p = weights(s[i + 1])
        o_ref[rows, :] = normalised(o).astype(o_ref.dtype)

    p_c = weights(_dot_nt(k_scr[n_lat:, :], qc_ref[...]))
    oc_ref[...] = normalised(_dot(vt_scr[:, n_lat:], p_c)).astype(oc_ref.dtype)


def _mla_attention(q_l, q_c, kv_l, kv_c, kr_l, kr_c, *, batch, casts=()):
    n_lat, n_ctx = q_l.shape[0] // batch, q_c.shape[0] // batch
    assert n_lat % MLA_Q_SUB == 0
    head = lambda b, h: (b, h)
    k_of = lambda b, h: (b, 2 * h)
    v_of = lambda b, h: (b, 2 * h + 1)
    shared = lambda b, h: (b, 0)
    in_specs = [
        pl.BlockSpec((n_lat, MLA_QK_PAD), head),
        pl.BlockSpec((n_ctx, MLA_QK_PAD), head),
        pl.BlockSpec((n_lat, LANES), k_of),
        pl.BlockSpec((n_lat, LANES), v_of),
        pl.BlockSpec((n_lat, LANES), shared),
        pl.BlockSpec((n_ctx, LANES), k_of),
        pl.BlockSpec((n_ctx, LANES), v_of),
        pl.BlockSpec((n_ctx, LANES), shared),
    ]
    args = [q_l, q_c, kv_l, kv_l, kr_l, kv_c, kv_c, kr_c]
    out_shape = [jax.ShapeDtypeStruct((batch * n_lat, MLA_HEADS * MLA_V), BF16),
                 jax.ShapeDtypeStruct((batch * n_ctx, MLA_HEADS * MLA_V), BF16)]
    out_specs = [pl.BlockSpec((n_lat, MLA_V), head), pl.BlockSpec((n_ctx, MLA_V), head)]
    for w in casts:
        assert w.shape[0] == batch * MLA_HEADS
        slab = pl.BlockSpec((None,) + w.shape[1:], lambda b, h: (b * MLA_HEADS + h, 0, 0))
        in_specs.append(slab)
        args.append(w)
        out_specs.append(slab)
        out_shape.append(jax.ShapeDtypeStruct(w.shape, BF16))
    n_keys = n_lat + n_ctx
    outs = pl.pallas_call(
        functools.partial(_mla_kernel, n_lat=n_lat, n_ctx=n_ctx, n_cast=len(casts)),
        out_shape=out_shape,
        grid=(batch, MLA_HEADS),
        in_specs=in_specs,
        out_specs=out_specs,
        scratch_shapes=[pltpu.VMEM((n_keys, MLA_QK_PAD), BF16),
                        pltpu.VMEM((MLA_V + ONES_ROWS, n_keys), BF16)],
        compiler_params=_params(("parallel", "parallel"), 56),
        name="mla_attention",
    )(*args)
    return outs[0], outs[1], outs[2:]


def _gelu(x):
    c = math.sqrt(2.0 / math.pi)
    return x * (0.5 + 0.5 * jnp.tanh(x * (c + (c * 0.044715) * (x * x))))


def _gmlp_tiles(u, v, w_ref, b_ref, o_ref):
    for n0 in range(0, u.shape[0] // GMLP_CHUNK, GMLP_BATCH):
        tiles = [(slice(n * GMLP_CHUNK, (n + 1) * GMLP_CHUNK), slice(g * GMLP_DIM, (g + 1) * GMLP_DIM), g)
                 for n in range(n0, n0 + GMLP_BATCH) for g in range(GMLP_GROUPS)]
        x = [_gelu(v[rows, cols]) for rows, cols, _ in tiles]
        xc = [t - jnp.mean(t, axis=-1, keepdims=True) for t in x]
        var = [jnp.mean(t * t, axis=-1, keepdims=True) for t in xc]
        xn = [(t * lax.rsqrt(s + NORM_EPS)).astype(BF16) for t, s in zip(xc, var)]
        for (rows, cols, g), t in zip(tiles, xn):
            mixed = _dot(w_ref[g], t) + b_ref[g]
            o_ref[rows, cols] = (_gelu(u[rows, cols]) * mixed).astype(o_ref.dtype)


def _swa_kernel(sink_ref, q_ref, kl_ref, vl_ref, kc_ref, vc_ref, o_ref, *, tq, n_lat):
    start = pl.program_id(1) * tq
    span0 = pl.multiple_of(jnp.clip(start - SWA_WINDOW, 0, n_lat - SWA_SPAN), SWA_WINDOW)
    k_span = kl_ref[pl.ds(span0, SWA_SPAN), :]
    k_ctx = kc_ref[...]
    vt_span = vl_ref[pl.ds(span0, SWA_SPAN), :].astype(F32).T.astype(BF16)
    vt_ctx = vc_ref[...].astype(F32).T.astype(BF16)
    ones_span = jnp.ones((ONES_ROWS, vt_span.shape[1]), BF16)
    ones_ctx = jnp.ones((ONES_ROWS, vt_ctx.shape[1]), BF16)

    k_pos = span0 + lax.broadcasted_iota(jnp.int32, (SWA_SPAN, tq), 0)
    q_pos = start + lax.broadcasted_iota(jnp.int32, (SWA_SPAN, tq), 1)
    bias = jnp.where(jnp.abs(k_pos - q_pos) <= SWA_WINDOW, 0.0, NEG_INF).astype(F32)
    bias = jnp.concatenate([bias] * SWA_UNIT, axis=1)

    lane = lax.broadcasted_iota(jnp.int32, (tq, LANES), 1)
    first = (lane % SWA_HEAD_DIM) < (SWA_HEAD_DIM // 2)
    keep = [jnp.where(first, 1.0, 0.0).astype(BF16), jnp.where(first, 0.0, 1.0).astype(BF16)]

    units = [range(h0, h0 + SWA_UNIT) for h0 in range(0, SWA_HEADS, SWA_UNIT)]

    def scores(heads):
        g = heads[0] // SWA_GROUP
        k_cols = slice(g * LANES, (g + 1) * LANES)
        q = jnp.concatenate(
            [q_ref[:, (h // 2) * LANES:(h // 2 + 1) * LANES] * keep[h % 2] for h in heads], axis=0)
        return _dot_nt(k_span[:, k_cols], q) + bias, _dot_nt(k_ctx[:, k_cols], q)

    s_next = scores(units[0])
    for u, heads in enumerate(units):
        g = heads[0] // SWA_GROUP
        kv_cols = slice(g * SWA_HEAD_DIM, (g + 1) * SWA_HEAD_DIM)
        s_l, s_c = s_next
        if u + 1 < len(units):
            s_next = scores(units[u + 1])
        sink = jnp.concatenate([jnp.full((1, tq), sink_ref[h] * LOG2E, F32) for h in heads], axis=1)
        m = jnp.maximum(jnp.maximum(jnp.max(s_l, axis=0, keepdims=True),
                                    jnp.max(s_c, axis=0, keepdims=True)), sink)
        p_l = jnp.exp2(s_l - m).astype(BF16)
        p_c = jnp.exp2(s_c - m).astype(BF16)
        vt_l = jnp.concatenate([vt_span[kv_cols, :], ones_span], axis=0)
        vt_c = jnp.concatenate([vt_ctx[kv_cols, :], ones_ctx], axis=0)
        o = _dot(vt_l, p_l) + _dot(vt_c, p_c)
        dim = SWA_HEAD_DIM
        o = o[0:dim] / (o[dim:dim + 1] + jnp.exp2(sink - m))
        for pair in range(SWA_UNIT // 2):
            cols = [slice((2 * pair + r) * tq, (2 * pair + r + 1) * tq) for r in range(2)]
            both = jnp.concatenate([o[:, cols[0]], o[:, cols[1]]], axis=0)
            group = heads[0] // 2 + pair
            o_ref[:, group * LANES:(group + 1) * LANES] = both.T.astype(o_ref.dtype)


def _swa_attention(z_l, z_c, sinks, *, batch, n_lat, n_ctx, tq=SWA_BLOCK):
    nq_blocks = n_lat // tq
    k_cols = 2 * SWA_KV_COLS
    return pl.pallas_call(
        functools.partial(_swa_kernel, tq=tq, n_lat=n_lat),
        out_shape=jax.ShapeDtypeStruct((batch * n_lat, SWA_Q_COLS), BF16),
        grid=(batch, nq_blocks),
        in_specs=[
            pl.BlockSpec(memory_space=pltpu.SMEM),
            pl.BlockSpec((tq, SWA_Q_COLS), lambda b, i: (b * nq_blocks + i, 0)),
            pl.BlockSpec((n_lat, k_cols), lambda b, i: (b, SWA_Q_COLS // k_cols)),
            pl.BlockSpec((n_lat, SWA_KV_COLS), lambda b, i: (b, (SWA_Q_COLS + k_cols) // SWA_KV_COLS)),
            pl.BlockSpec((n_ctx, k_cols), lambda b, i: (b, 0)),
            pl.BlockSpec((n_ctx, SWA_KV_COLS), lambda b, i: (b, k_cols // SWA_KV_COLS)),
        ],
        out_specs=pl.BlockSpec((tq, SWA_Q_COLS), lambda b, i: (b * nq_blocks + i, 0)),
        compiler_params=_params(("parallel", "arbitrary"), 48),
        name="swa_attention",
    )(sinks, z_l, z_l, z_l, z_c, z_c)


def _outproj_kernel(a1_ref, a2_ref, w1_ref, w2_ref, x_ref, gate_ref, o_ref):
    y = _dot(a1_ref[...], w1_ref[...]) + _dot(a2_ref[...], w2_ref[...])
    o_ref[...] = x_ref[...] + gate_ref[...] * y


def _outproj(a1, a1_col, a2, a2_col, w, x, mods, row_of, k_gate, *, tm=512):
    t, d = x.shape
    half = w.shape[0] // 2
    return pl.pallas_call(
        _outproj_kernel,
        out_shape=jax.ShapeDtypeStruct((t, d), F32),
        grid=(t // tm,),
        in_specs=[
            pl.BlockSpec((tm, half), lambda i: (i, a1_col)),
            pl.BlockSpec((tm, half), lambda i: (i, a2_col)),
            pl.BlockSpec((half, d), lambda i: (0, 0)),
            pl.BlockSpec((half, d), lambda i: (1, 0)),
            pl.BlockSpec((tm, d), lambda i: (i, 0)),
            pl.BlockSpec((None, 1, d), lambda i: (row_of(i), 0, k_gate)),
        ],
        out_specs=pl.BlockSpec((tm, d), lambda i: (i, 0)),
        compiler_params=_params(("parallel",), 48),
        name="outproj",
    )(a1, a2, w, w, x, mods)


def _mlp_kernel(*refs, final):
    if final:
        x_ref, g_ref, sh_ref, sc_ref, gate_ref, w1_ref, w2_ref, fg_ref, o_ref, h_ref, r_ref = refs
    else:
        x_ref, g_ref, sh_ref, sc_ref, gate_ref, w1_ref, w2_ref, o_ref, h_ref, r_ref = refs
    k = pl.program_id(1)

    def ff_chunk():
        a = jnp.square(jnp.maximum(_dot(h_ref[...], w1_ref[...]), 0.0)).astype(BF16)
        return _dot(a, w2_ref[...])

    @pl.when(k == 0)
    def _():
        def store(sl, y):
            h_ref[sl, :] = y.astype(BF16)

        _norm_modulate(x_ref, r_ref, store, g_ref[...], sc_ref[...], sh_ref[...])
        o_ref[...] = ff_chunk()

    @pl.when(k > 0)
    def _():
        o_ref[...] += ff_chunk()

    @pl.when(k == pl.num_programs(1) - 1)
    def _():
        def rows(sl):
            out = x_ref[sl, :] + gate_ref[...] * o_ref[sl, :]
            if final:
                out = _rms(out, fg_ref[...])
            o_ref[sl, :] = out

        _for_row_chunks(x_ref.shape[0], rows)


def _mlp(x, g, mods, row_of, w1, w2, layer, *, final_g=None, tm=512, tf=1024, vmem_mib=56):
    t, d = x.shape
    n_k = w1.shape[2] // tf
    in_specs = [
        pl.BlockSpec((tm, d), lambda i, k: (i, 0)),
        pl.BlockSpec((1, d), lambda i, k: (0, 0)),
        pl.BlockSpec((None, 1, d), lambda i, k: (row_of(i), 0, 3)),
        pl.BlockSpec((None, 1, d), lambda i, k: (row_of(i), 0, 4)),
        pl.BlockSpec((None, 1, d), lambda i, k: (row_of(i), 0, 5)),
        pl.BlockSpec((None, d, tf), lambda i, k: (layer, 0, k)),
        pl.BlockSpec((None, tf, d), lambda i, k: (layer, k, 0)),
    ]
    args = [x, g.reshape(1, d), mods, mods, mods, w1, w2]
    if final_g is not None:
        in_specs.append(pl.BlockSpec((1, d), lambda i, k: (0, 0)))
        args.append(final_g.reshape(1, d))
    return pl.pallas_call(
        functools.partial(_mlp_kernel, final=final_g is not None),
        out_shape=jax.ShapeDtypeStruct((t, d), F32),
        grid=(t // tm, n_k),
        in_specs=in_specs,
        out_specs=pl.BlockSpec((tm, d), lambda i, k: (i, 0)),
        scratch_shapes=[pltpu.VMEM((tm, d), BF16), pltpu.VMEM((tm, 1), F32)],
        compiler_params=_params(("parallel", "arbitrary"), vmem_mib),
        name="mlp",
    )(*args)


def _axial_angles(n_tokens, rot_dim):
    t = jnp.arange(n_tokens)
    row = (t // GRID_W).astype(F32)
    col = (t % GRID_W).astype(F32)
    n_freq = rot_dim // 4
    inv_freq = ROPE_BASE ** (-jnp.arange(n_freq, dtype=F32) / n_freq)
    ang = jnp.concatenate([row[:, None] * inv_freq, col[:, None] * inv_freq], axis=-1)
    return jnp.cos(ang), jnp.sin(ang)


def _swap_halves(w):
    half = w.shape[-1] // 2
    return jnp.concatenate([w[..., half:], w[..., :half]], axis=-1)


def kernel(x, c, ctx, c_ctx, norm1_g, w_mod, b_mod, norm2_g, w_ff1, w_ff2, even_w_in, mla_q_norm_g,
           mla_w_uq, mla_kv_norm_g, mla_w_ukv, gmlp_w_sp, gmlp_b_sp, even_w_out, odd_w_in, swa_sinks,
           odd_w_out, final_norm_g):
    batch, n_lat, d = x.shape
    n_ctx = ctx.shape[1]
    assert batch + 1 <= MOD_ROWS
    xl = x.reshape(batch * n_lat, d)
    xc = ctx.reshape(batch * n_ctx, d)

    tm = 1024

    def lat_row(block_rows):
        per_sample = n_lat // block_rows
        return lambda i: i // per_sample

    ctx_row = lambda i: batch

    cvec = jnp.concatenate([c, c_ctx[None, :], jnp.zeros((MOD_ROWS - batch - 1, d), F32)], axis=0)
    mods = _modulation(cvec, w_mod, b_mod)
    mods = mods.reshape(mods.shape[0], MOD_ROWS, 1, N_MOD * d)

    cos_m, sin_m = _axial_angles(n_lat, MLA_ROPE)
    zeros64 = jnp.zeros((n_lat, LANES // 2), F32)
    fold_cos = jnp.concatenate([cos_m, cos_m, zeros64], axis=-1)
    fold_sin = jnp.concatenate([-sin_m, sin_m, zeros64], axis=-1)
    keep = jnp.concatenate([jnp.ones((tm, LANES // 2), F32), jnp.zeros((tm, LANES // 2), F32)], axis=-1)
    drop = jnp.zeros((tm, LANES), F32)

    w_in = even_w_in[0]
    cq_w, ckv_w = w_in[:, :MLA_LORA], w_in[:, MLA_LORA:2 * MLA_LORA]
    kr_w = w_in[:, 2 * MLA_LORA:2 * MLA_LORA + MLA_ROPE]
    gm_w = w_in[:, 2 * MLA_LORA + MLA_ROPE:]
    w_in0 = jnp.concatenate([cq_w, ckv_w, gm_w, kr_w, _swap_halves(kr_w)], axis=-1).astype(BF16)

    wq = mla_w_uq[0].reshape(MLA_LORA, MLA_HEADS, MLA_NOPE + MLA_ROPE)
    wq_rope = wq[..., MLA_NOPE:]
    w_uq = jnp.concatenate([wq[..., :MLA_NOPE], wq_rope, _swap_halves(wq_rope)], axis=-1)
    w_uq = (w_uq * (MLA_SCALE * LOG2E)).reshape(MLA_LORA, MLA_HEADS * MLA_QK_PAD).astype(BF16)
    q_groups = tuple(range(1, 2 * MLA_HEADS, 2))
    w_ukv = mla_w_ukv[0].astype(BF16)
    w_sp = gmlp_w_sp[0].astype(BF16)
    b_sp = gmlp_b_sp[0][:, :, None]
    tm_i = 512

    def layer0_tokens(xs, tabs, latent):
        row_of = lat_row(tm_i) if latent else ctx_row
        pos_blocks = n_lat // tm_i if latent else 1
        return _even_in(xs, norm1_g[0], mods[0], row_of, w_in0, tabs, pos_blocks, mla_q_norm_g[0],
                        mla_kv_norm_g[0], w_uq, w_ukv, w_sp, b_sp, tm=tm_i, lora=MLA_LORA,
                        q_groups=q_groups)

    kr_l, q_l, kv_l, gm_l = layer0_tokens(xl, (fold_cos, fold_sin), True)
    kr_c, q_c, kv_c, gm_c = layer0_tokens(xc, (keep, drop), False)
    cast_steps = batch * MLA_HEADS
    late = (w_ff1, w_ff2, even_w_out, odd_w_out)
    slabs = [w.reshape(cast_steps, -1, w.shape[-1]) for w in late]
    att_l, att_c, late_bf16 = _mla_attention(q_l, q_c, kv_l, kv_c, kr_l, kr_c, batch=batch, casts=slabs)
    w1_all, w2_all, w_out0, w_out1 = [b.reshape(w.shape) for b, w in zip(late_bf16, late)]
    w_out0, w_out1 = w_out0[0], w_out1[0]

    tm_o = 512
    xl = _outproj(att_l, 0, gm_l, 0, w_out0, xl, mods[0], lat_row(tm_o), 2, tm=tm_o)
    xc = _outproj(att_c, 0, gm_c, 0, w_out0, xc, mods[0], ctx_row, 2, tm=tm_o)
    tm_f = 512
    xl = _mlp(xl, norm2_g[0], mods[0], lat_row(tm_f), w1_all, w2_all, 0, tm=tm_f)
    xc = _mlp(xc, norm2_g[0], mods[0], ctx_row, w1_all, w2_all, 0, tm=tm_f)

    cos_s, sin_s = _axial_angles(n_lat, SWA_HEAD_DIM)
    pair_cos = jnp.concatenate([cos_s] * 4, axis=-1)
    pair_sin = jnp.concatenate([-sin_s, -sin_s, sin_s, sin_s], axis=-1)
    half = SWA_HEAD_DIM // 2
    w_in = odd_w_in[0]
    wq = (w_in[:, :SWA_Q_COLS] * (SWA_SCALE * LOG2E)).reshape(d, SWA_HEADS // 2, 2, 2, half)
    wq = wq.transpose(0, 1, 3, 2, 4).reshape(d, SWA_Q_COLS)
    wk = w_in[:, SWA_Q_COLS:SWA_Q_COLS + SWA_KV_COLS].reshape(d, SWA_KV_HEADS, 2, 1, half)
    wk = jnp.broadcast_to(wk, (d, SWA_KV_HEADS, 2, 2, half)).reshape(d, 2 * SWA_KV_COLS)
    wv = w_in[:, SWA_Q_COLS + SWA_KV_COLS:]
    w_in1 = jnp.concatenate([wq, wk, wv], axis=-1).astype(BF16)
    w_kv1 = jnp.concatenate([wk, wv], axis=-1).astype(BF16)
    rot_groups = (SWA_Q_COLS + 2 * SWA_KV_COLS) // LANES
    m1 = mods[1]
    z1_l = _proj(xl, norm1_g[1], m1, lat_row(tm_i), w_in1, tm=tm_i, rot_groups=rot_groups,
                 tabs=(pair_cos, pair_sin), pos_blocks=n_lat // tm_i, vmem_mib=56, name="odd_in_proj")
    z1_c = _proj(xc, norm1_g[1], m1, ctx_row, w_kv1, tm=tm, name="odd_ctx_kv_proj")
    att = _swa_attention(z1_l, z1_c, swa_sinks[0], batch=batch, n_lat=n_lat, n_ctx=n_ctx)
    xl = _outproj(att, 0, att, 1, w_out1, xl, m1, lat_row(tm_o), 2, tm=tm_o)
    out = _mlp(xl, norm2_g[1], m1, lat_row(tm_f), w1_all, w2_all, 1, final_g=final_norm_g, tm=tm_f)
    return out.reshape(batch, n_lat, d)
```

```python
import functools
import math

import jax
import jax.numpy as jnp
from jax import lax
from jax.experimental import pallas as pl
from jax.experimental.pallas import tpu as pltpu

F32 = jnp.float32
BF16 = jnp.bfloat16

LANES = 128
MIB = 1 << 20

GRID_W = 64
N_MOD = 6
NORM_EPS = 1e-6
ROPE_BASE = 10000.0
NEG_INF = -1e30

MLA_HEADS = 8
MLA_LORA = 512
MLA_NOPE = 128
MLA_ROPE = 64
MLA_V = 128
MLA_QK_PAD = 2 * LANES
MLA_SCALE = 1.0 / math.sqrt(MLA_NOPE + MLA_ROPE)
MLA_Q_SUB = 512

GMLP_GROUPS = 8
GMLP_DIM = 128
GMLP_CHUNK = 128
GMLP_WIDTH = GMLP_GROUPS * GMLP_DIM
GMLP_BATCH = 2

SWA_HEADS = 32
SWA_KV_HEADS = 4
SWA_GROUP = SWA_HEADS // SWA_KV_HEADS
SWA_HEAD_DIM = 64
SWA_WINDOW = 128
SWA_BLOCK = 128
SWA_SPAN = SWA_BLOCK + 2 * SWA_WINDOW
SWA_SCALE = 1.0 / math.sqrt(SWA_HEAD_DIM)
SWA_Q_COLS = SWA_HEADS * SWA_HEAD_DIM
SWA_KV_COLS = SWA_KV_HEADS * SWA_HEAD_DIM
SWA_UNIT = SWA_GROUP

MOD_ROWS = 16
ROW_CHUNK = 128
STATS_CHUNK = 256
NARROW_ROW = 512
AHEAD_ROWS = 64
ONES_ROWS = 16
LOG2E = math.log2(math.e)


def _params(semantics, vmem_mib):
    return pltpu.CompilerParams(dimension_semantics=semantics, vmem_limit_bytes=vmem_mib * MIB)


def _rms(x, g):
    return x * lax.rsqrt(jnp.mean(x * x, axis=-1, keepdims=True) + NORM_EPS) * g


def _for_row_chunks(n_rows, fn, rows=ROW_CHUNK):
    rows = min(rows, n_rows)

    def body(r, carry):
        fn(pl.ds(pl.multiple_of(r * rows, rows), rows))
        return carry

    lax.fori_loop(0, n_rows // rows, body, 0)


def _norm_modulate(x_ref, r_ref, store, g, scale=None, shift=None):
    n_rows = x_ref.shape[0]
    gain = g if scale is None else g * (1.0 + scale)

    if x_ref.shape[1] <= NARROW_ROW:
        def whole(sl):
            y = _rms(x_ref[sl, :], gain)
            store(sl, y if shift is None else y + shift)

        _for_row_chunks(n_rows, whole)
        return

    def stats(sl):
        x = x_ref[sl, :]
        r_ref[sl, :] = lax.rsqrt(jnp.mean(x * x, axis=-1, keepdims=True) + NORM_EPS)

    _for_row_chunks(n_rows, stats, rows=STATS_CHUNK)

    def apply(sl):
        y = x_ref[sl, :] * r_ref[sl, :] * gain
        store(sl, y if shift is None else y + shift)

    _for_row_chunks(n_rows, apply)


def _first_block_norm(x_ref, r_ref, h_ref, g, scale, shift, first_step=None):
    @pl.when(pl.program_id(0) == 0 if first_step is None else first_step)
    def _():
        def store(sl, y):
            h_ref[0, sl, :] = y.astype(BF16)

        _norm_modulate(x_ref, r_ref, store, g, scale, shift)


def _next_block_norm(xn_ref, h_ref, slot, g, scale, shift):
    gain = g * (1.0 + scale)
    for r0 in range(0, xn_ref.shape[0], AHEAD_ROWS):
        rows = slice(r0, r0 + AHEAD_ROWS)
        x = xn_ref[rows, :]
        r = lax.rsqrt(jnp.mean(x * x, axis=-1, keepdims=True) + NORM_EPS)
        h_ref[slot, rows, :] = (x * r * gain + shift).astype(BF16)


def _dot(a, b):
    return jnp.dot(a, b, preferred_element_type=F32)


def _dot_nt(a, b):
    return lax.dot_general(a, b, (((1,), (1,)), ((), ())), preferred_element_type=F32)


def _mod_kernel(c_ref, w_ref, b_ref, o_ref):
    c = c_ref[...]
    s = (c * jax.nn.sigmoid(c)).astype(BF16)
    o_ref[...] = _dot(s, w_ref[...].astype(BF16)) + b_ref[...]


def _modulation(cvec, w_mod, b_mod, *, tn=1024):
    depth, d, n = w_mod.shape
    return pl.pallas_call(
        _mod_kernel,
        out_shape=jax.ShapeDtypeStruct((depth, MOD_ROWS, n), F32),
        grid=(depth, n // tn),
        in_specs=[
            pl.BlockSpec((MOD_ROWS, d), lambda l, j: (0, 0)),
            pl.BlockSpec((None, d, tn), lambda l, j: (l, 0, j)),
            pl.BlockSpec((None, 1, tn), lambda l, j: (l, 0, j)),
        ],
        out_specs=pl.BlockSpec((None, MOD_ROWS, tn), lambda l, j: (l, 0, j)),
        compiler_params=_params(("parallel", "parallel"), 40),
        name="modulation",
    )(cvec, w_mod, b_mod.reshape(depth, 1, n))


def _proj_kernel(*refs, rot_groups):
    if rot_groups:
        (x_ref, xn_ref, g_ref, sh_ref, sc_ref, shn_ref, scn_ref, w_ref, cos_ref, sin_ref, o_ref, h_ref,
         r_ref) = refs
    else:
        x_ref, xn_ref, g_ref, sh_ref, sc_ref, shn_ref, scn_ref, w_ref, o_ref, h_ref, r_ref = refs
    cur = pl.program_id(0) % 2
    _first_block_norm(x_ref, r_ref, h_ref, g_ref[...], sc_ref[...], sh_ref[...])
    z = _dot(h_ref[cur], w_ref[...])
    _next_block_norm(xn_ref, h_ref, 1 - cur, g_ref[...], scn_ref[...], shn_ref[...])

    for gi in range(z.shape[1] // LANES):
        sl = slice(gi * LANES, (gi + 1) * LANES)
        t = z[:, sl]
        if gi < rot_groups:
            t = t * cos_ref[...] + pltpu.roll(t, LANES // 2, 1) * sin_ref[...]
        o_ref[:, sl] = t.astype(o_ref.dtype)


def _proj(x, g, mods, row_of, w, *, tm, rot_groups=0, tabs=(), pos_blocks=1, vmem_mib=48,
          name="proj"):
    t, k = x.shape
    n = w.shape[1]
    n_i = t // tm
    nxt = lambda i: jnp.minimum(i + 1, n_i - 1)
    in_specs = [
        pl.BlockSpec((tm, k), lambda i: (i, 0)),
        pl.BlockSpec((tm, k), lambda i: (nxt(i), 0)),
        pl.BlockSpec((1, k), lambda i: (0, 0)),
        pl.BlockSpec((None, 1, k), lambda i: (row_of(i), 0, 0)),
        pl.BlockSpec((None, 1, k), lambda i: (row_of(i), 0, 1)),
        pl.BlockSpec((None, 1, k), lambda i: (row_of(nxt(i)), 0, 0)),
        pl.BlockSpec((None, 1, k), lambda i: (row_of(nxt(i)), 0, 1)),
        pl.BlockSpec((k, n), lambda i: (0, 0), pipeline_mode=pl.Buffered(1)),
    ]
    in_specs += [pl.BlockSpec((tm, LANES), lambda i: (i % pos_blocks, 0)) for _ in tabs]
    return pl.pallas_call(
        functools.partial(_proj_kernel, rot_groups=rot_groups),
        out_shape=jax.ShapeDtypeStruct((t, n), BF16),
        grid=(n_i,),
        in_specs=in_specs,
        out_specs=pl.BlockSpec((tm, n), lambda i: (i, 0)),
        scratch_shapes=[pltpu.VMEM((2, tm, k), BF16), pltpu.VMEM((tm, 1), F32)],
        compiler_params=_params(("arbitrary",), vmem_mib),
        name=name,
    )(x, x, g.reshape(1, k), mods, mods, mods, mods, w, *tabs)


def _even_in_kernel(x_ref, xn_ref, g_ref, sh_ref, sc_ref, shn_ref, scn_ref, w_ref, cos_ref, sin_ref,
                    gq_ref, gkv_ref, wq_ref, wkv_ref, wsp_ref, bsp_ref, kr_ref, q_ref, kv_ref, gm_ref,
                    h_ref, r_ref, *, lora, q_groups):
    cur = pl.program_id(0) % 2
    _first_block_norm(x_ref, r_ref, h_ref, g_ref[...], sc_ref[...], sh_ref[...])
    z = _dot(h_ref[cur], w_ref[...])
    _next_block_norm(xn_ref, h_ref, 1 - cur, g_ref[...], scn_ref[...], shn_ref[...])

    def rotated(t):
        return t * cos_ref[...] + pltpu.roll(t, LANES // 2, 1) * sin_ref[...]

    gm_lo, gm_hi = 2 * lora, 2 * lora + 2 * GMLP_WIDTH
    kr_ref[...] = rotated(z[:, gm_hi:])
    _gmlp_tiles(z[:, gm_lo:gm_lo + GMLP_WIDTH], z[:, gm_lo + GMLP_WIDTH:gm_hi], wsp_ref, bsp_ref, gm_ref)

    q = _dot(_rms(z[:, :lora], gq_ref[...]).astype(BF16), wq_ref[...])
    for gi in range(q.shape[1] // LANES):
        sl = slice(gi * LANES, (gi + 1) * LANES)
        q_ref[:, sl] = (rotated(q[:, sl]) if gi in q_groups else q[:, sl]).astype(q_ref.dtype)
    kv = _dot(_rms(z[:, lora:2 * lora], gkv_ref[...]).astype(BF16), wkv_ref[...])
    kv_ref[...] = kv.astype(kv_ref.dtype)


def _even_in(x, g, mods, row_of, w, tabs, pos_blocks, gq, gkv, wq, wkv, w_sp, b_sp, *, tm, lora,
             q_groups, vmem_mib=58):
    t, d = x.shape
    n = w.shape[1]
    assert n == 2 * lora + 2 * GMLP_WIDTH + LANES
    once = {"pipeline_mode": pl.Buffered(1)}
    row_blk = lambda i: (i, 0)
    fixed = lambda i: (0, 0)
    n_i = t // tm
    nxt = lambda i: jnp.minimum(i + 1, n_i - 1)
    return pl.pallas_call(
        functools.partial(_even_in_kernel, lora=lora, q_groups=q_groups),
        out_shape=[jax.ShapeDtypeStruct((t, LANES), F32),
                   jax.ShapeDtypeStruct((t, wq.shape[1]), BF16),
                   jax.ShapeDtypeStruct((t, wkv.shape[1]), BF16),
                   jax.ShapeDtypeStruct((t, GMLP_WIDTH), BF16)],
        grid=(n_i,),
        in_specs=[
            pl.BlockSpec((tm, d), row_blk),
            pl.BlockSpec((tm, d), lambda i: (nxt(i), 0)),
            pl.BlockSpec((1, d), fixed),
            pl.BlockSpec((None, 1, d), lambda i: (row_of(i), 0, 0)),
            pl.BlockSpec((None, 1, d), lambda i: (row_of(i), 0, 1)),
            pl.BlockSpec((None, 1, d), lambda i: (row_of(nxt(i)), 0, 0)),
            pl.BlockSpec((None, 1, d), lambda i: (row_of(nxt(i)), 0, 1)),
            pl.BlockSpec((d, n), fixed, **once),
            pl.BlockSpec((tm, LANES), lambda i: (i % pos_blocks, 0)),
            pl.BlockSpec((tm, LANES), lambda i: (i % pos_blocks, 0)),
            pl.BlockSpec((1, lora), fixed),
            pl.BlockSpec((1, lora), fixed),
            pl.BlockSpec(wq.shape, fixed, **once),
            pl.BlockSpec(wkv.shape, fixed, **once),
            pl.BlockSpec(w_sp.shape, lambda i: (0, 0, 0)),
            pl.BlockSpec(b_sp.shape, lambda i: (0, 0, 0)),
        ],
        out_specs=[pl.BlockSpec((tm, LANES), row_blk),
                   pl.BlockSpec((tm, wq.shape[1]), row_blk),
                   pl.BlockSpec((tm, wkv.shape[1]), row_blk),
                   pl.BlockSpec((tm, GMLP_WIDTH), row_blk)],
        scratch_shapes=[pltpu.VMEM((2, tm, d), BF16), pltpu.VMEM((tm, 1), F32)],
        compiler_params=_params(("arbitrary",), vmem_mib),
        name="even_in_proj",
    )(x, x, g.reshape(1, d), mods, mods, mods, mods, w, tabs[0], tabs[1], gq.reshape(1, lora),
      gkv.reshape(1, lora), wq, wkv, w_sp, b_sp)


def _mla_kernel(*refs, n_lat, n_ctx, n_cast):
    n_in = 8
    q_ref, qc_ref, knl_ref, vl_ref, krl_ref, knc_ref, vc_ref, krc_ref = refs[:n_in]
    cast_in = refs[n_in:n_in + n_cast]
    o_ref, oc_ref = refs[n_in + n_cast:n_in + n_cast + 2]
    cast_out = refs[n_in + n_cast + 2:n_in + 2 * n_cast + 2]
    k_scr, vt_scr = refs[n_in + 2 * n_cast + 2:]
    for src, dst in zip(cast_in, cast_out):
        dst[...] = src[...].astype(dst.dtype)

    k_scr[0:n_lat, 0:LANES] = knl_ref[...]
    k_scr[0:n_lat, LANES:] = krl_ref[...].astype(BF16)
    vt_scr[0:MLA_V, 0:n_lat] = vl_ref[...].astype(F32).T.astype(BF16)
    k_scr[n_lat:, 0:LANES] = knc_ref[...]
    k_scr[n_lat:, LANES:] = krc_ref[...].astype(BF16)
    vt_scr[0:MLA_V, n_lat:] = vc_ref[...].astype(F32).T.astype(BF16)
    vt_scr[MLA_V:, :] = jnp.ones((ONES_ROWS, n_lat + n_ctx), BF16)

    def weights(s):
        return jnp.exp2(s - jnp.max(s, axis=0, keepdims=True)).astype(BF16)

    def normalised(o):
        return (o[0:MLA_V] / o[MLA_V:MLA_V + 1]).T

    subs = [slice(r, r + MLA_Q_SUB) for r in range(0, n_lat, MLA_Q_SUB)]

    def scores(i):
        return _dot_nt(k_scr[...], q_ref[subs[i], :])

    n = len(subs)
    s = [scores(i) if i < 2 else None for i in range(n)]
    p = weights(s[0])
    for i, rows in enumerate(subs):
        if i + 2 < n:
            s[i + 2] = scores(i + 2)
        o = _dot(vt_scr[...], p)
        if i + 1 < n:
            p = weights(s[i + 1])
        o_ref[rows, :] = normalised(o).astype(o_ref.dtype)

    p_c = weights(_dot_nt(k_scr[n_lat:, :], qc_ref[...]))
    oc_ref[...] = normalised(_dot(vt_scr[:, n_lat:], p_c)).astype(oc_ref.dtype)


def _mla_attention(q_l, q_c, kv_l, kv_c, kr_l, kr_c, *, batch, casts=()):
    n_lat, n_ctx = q_l.shape[0] // batch, q_c.shape[0] // batch
    assert n_lat % MLA_Q_SUB == 0
    head = lambda b, h: (b, h)
    k_of = lambda b, h: (b, 2 * h)
    v_of = lambda b, h: (b, 2 * h + 1)
    shared = lambda b, h: (b, 0)
    in_specs = [
        pl.BlockSpec((n_lat, MLA_QK_PAD), head),
        pl.BlockSpec((n_ctx, MLA_QK_PAD), head),
        pl.BlockSpec((n_lat, LANES), k_of),
        pl.BlockSpec((n_lat, LANES), v_of),
        pl.BlockSpec((n_lat, LANES), shared),
        pl.BlockSpec((n_ctx, LANES), k_of),
        pl.BlockSpec((n_ctx, LANES), v_of),
        pl.BlockSpec((n_ctx, LANES), shared),
    ]
    args = [q_l, q_c, kv_l, kv_l, kr_l, kv_c, kv_c, kr_c]
    out_shape = [jax.ShapeDtypeStruct((batch * n_lat, MLA_HEADS * MLA_V), BF16),
                 jax.ShapeDtypeStruct((batch * n_ctx, MLA_HEADS * MLA_V), BF16)]
    out_specs = [pl.BlockSpec((n_lat, MLA_V), head), pl.BlockSpec((n_ctx, MLA_V), head)]
    for w in casts:
        assert w.shape[0] == batch * MLA_HEADS
        slab = pl.BlockSpec((None,) + w.shape[1:], lambda b, h: (b * MLA_HEADS + h, 0, 0))
        in_specs.append(slab)
        args.append(w)
        out_specs.append(slab)
        out_shape.append(jax.ShapeDtypeStruct(w.shape, BF16))
    n_keys = n_lat + n_ctx
    outs = pl.pallas_call(
        functools.partial(_mla_kernel, n_lat=n_lat, n_ctx=n_ctx, n_cast=len(casts)),
        out_shape=out_shape,
        grid=(batch, MLA_HEADS),
        in_specs=in_specs,
        out_specs=out_specs,
        scratch_shapes=[pltpu.VMEM((n_keys, MLA_QK_PAD), BF16),
                        pltpu.VMEM((MLA_V + ONES_ROWS, n_keys), BF16)],
        compiler_params=_params(("parallel", "parallel"), 56),
        name="mla_attention",
    )(*args)
    return outs[0], outs[1], outs[2:]


def _gelu(x):
    c = math.sqrt(2.0 / math.pi)
    return x * (0.5 + 0.5 * jnp.tanh(x * (c + (c * 0.044715) * (x * x))))


def _gmlp_tiles(u, v, w_ref, b_ref, o_ref):
    for n0 in range(0, u.shape[0] // GMLP_CHUNK, GMLP_BATCH):
        tiles = [(slice(n * GMLP_CHUNK, (n + 1) * GMLP_CHUNK), slice(g * GMLP_DIM, (g + 1) * GMLP_DIM), g)
                 for n in range(n0, n0 + GMLP_BATCH) for g in range(GMLP_GROUPS)]
        x = [_gelu(v[rows, cols]) for rows, cols, _ in tiles]
        xc = [t - jnp.mean(t, axis=-1, keepdims=True) for t in x]
        var = [jnp.mean(t * t, axis=-1, keepdims=True) for t in xc]
        xn = [(t * lax.rsqrt(s + NORM_EPS)).astype(BF16) for t, s in zip(xc, var)]
        for (rows, cols, g), t in zip(tiles, xn):
            mixed = _dot(w_ref[g], t) + b_ref[g]
            o_ref[rows, cols] = (_gelu(u[rows, cols]) * mixed).astype(o_ref.dtype)


def _swa_kernel(sink_ref, q_ref, kl_ref, vl_ref, kc_ref, vc_ref, o_ref, *, tq, n_lat):
    start = pl.program_id(1) * tq
    span0 = pl.multiple_of(jnp.clip(start - SWA_WINDOW, 0, n_lat - SWA_SPAN), SWA_WINDOW)
    k_span = kl_ref[pl.ds(span0, SWA_SPAN), :]
    k_ctx = kc_ref[...]
    vt_span = vl_ref[pl.ds(span0, SWA_SPAN), :].astype(F32).T.astype(BF16)
    vt_ctx = vc_ref[...].astype(F32).T.astype(BF16)
    ones_span = jnp.ones((ONES_ROWS, vt_span.shape[1]), BF16)
    ones_ctx = jnp.ones((ONES_ROWS, vt_ctx.shape[1]), BF16)

    k_pos = span0 + lax.broadcasted_iota(jnp.int32, (SWA_SPAN, tq), 0)
    q_pos = start + lax.broadcasted_iota(jnp.int32, (SWA_SPAN, tq), 1)
    bias = jnp.where(jnp.abs(k_pos - q_pos) <= SWA_WINDOW, 0.0, NEG_INF).astype(F32)
    bias = jnp.concatenate([bias] * SWA_UNIT, axis=1)

    lane = lax.broadcasted_iota(jnp.int32, (tq, LANES), 1)
    first = (lane % SWA_HEAD_DIM) < (SWA_HEAD_DIM // 2)
    keep = [jnp.where(first, 1.0, 0.0).astype(BF16), jnp.where(first, 0.0, 1.0).astype(BF16)]

    units = [range(h0, h0 + SWA_UNIT) for h0 in range(0, SWA_HEADS, SWA_UNIT)]

    def scores(heads):
        g = heads[0] // SWA_GROUP
        k_cols = slice(g * LANES, (g + 1) * LANES)
        q = jnp.concatenate(
            [q_ref[:, (h // 2) * LANES:(h // 2 + 1) * LANES] * keep[h % 2] for h in heads], axis=0)
        return _dot_nt(k_span[:, k_cols], q) + bias, _dot_nt(k_ctx[:, k_cols], q)

    s_next = scores(units[0])
    for u, heads in enumerate(units):
        g = heads[0] // SWA_GROUP
        kv_cols = slice(g * SWA_HEAD_DIM, (g + 1) * SWA_HEAD_DIM)
        s_l, s_c = s_next
        if u + 1 < len(units):
            s_next = scores(units[u + 1])
        sink = jnp.concatenate([jnp.full((1, tq), sink_ref[h] * LOG2E, F32) for h in heads], axis=1)
        m = jnp.maximum(jnp.maximum(jnp.max(s_l, axis=0, keepdims=True),
                                    jnp.max(s_c, axis=0, keepdims=True)), sink)
        p_l = jnp.exp2(s_l - m).astype(BF16)
        p_c = jnp.exp2(s_c - m).astype(BF16)
        vt_l = jnp.concatenate([vt_span[kv_cols, :], ones_span], axis=0)
        vt_c = jnp.concatenate([vt_ctx[kv_cols, :], ones_ctx], axis=0)
        o = _dot(vt_l, p_l) + _dot(vt_c, p_c)
        dim = SWA_HEAD_DIM
        o = o[0:dim] / (o[dim:dim + 1] + jnp.exp2(sink - m))
        for pair in range(SWA_UNIT // 2):
            cols = [slice((2 * pair + r) * tq, (2 * pair + r + 1) * tq) for r in range(2)]
            both = jnp.concatenate([o[:, cols[0]], o[:, cols[1]]], axis=0)
            group = heads[0] // 2 + pair
            o_ref[:, group * LANES:(group + 1) * LANES] = both.T.astype(o_ref.dtype)


def _swa_attention(z_l, z_c, sinks, *, batch, n_lat, n_ctx, tq=SWA_BLOCK):
    nq_blocks = n_lat // tq
    k_cols = 2 * SWA_KV_COLS
    return pl.pallas_call(
        functools.partial(_swa_kernel, tq=tq, n_lat=n_lat),
        out_shape=jax.ShapeDtypeStruct((batch * n_lat, SWA_Q_COLS), BF16),
        grid=(batch, nq_blocks),
        in_specs=[
            pl.BlockSpec(memory_space=pltpu.SMEM),
            pl.BlockSpec((tq, SWA_Q_COLS), lambda b, i: (b * nq_blocks + i, 0)),
            pl.BlockSpec((n_lat, k_cols), lambda b, i: (b, SWA_Q_COLS // k_cols)),
            pl.BlockSpec((n_lat, SWA_KV_COLS), lambda b, i: (b, (SWA_Q_COLS + k_cols) // SWA_KV_COLS)),
            pl.BlockSpec((n_ctx, k_cols), lambda b, i: (b, 0)),
            pl.BlockSpec((n_ctx, SWA_KV_COLS), lambda b, i: (b, k_cols // SWA_KV_COLS)),
        ],
        out_specs=pl.BlockSpec((tq, SWA_Q_COLS), lambda b, i: (b * nq_blocks + i, 0)),
        compiler_params=_params(("parallel", "arbitrary"), 48),
        name="swa_attention",
    )(sinks, z_l, z_l, z_l, z_c, z_c)


def _outproj_kernel(a1_ref, a2_ref, w1_ref, w2_ref, x_ref, gate_ref, o_ref):
    y = _dot(a1_ref[...], w1_ref[...]) + _dot(a2_ref[...], w2_ref[...])
    o_ref[...] = x_ref[...] + gate_ref[...] * y


def _outproj(a1, a1_col, a2, a2_col, w, x, mods, row_of, k_gate, *, tm=512):
    t, d = x.shape
    half = w.shape[0] // 2
    return pl.pallas_call(
        _outproj_kernel,
        out_shape=jax.ShapeDtypeStruct((t, d), F32),
        grid=(t // tm,),
        in_specs=[
            pl.BlockSpec((tm, half), lambda i: (i, a1_col)),
            pl.BlockSpec((tm, half), lambda i: (i, a2_col)),
            pl.BlockSpec((half, d), lambda i: (0, 0)),
            pl.BlockSpec((half, d), lambda i: (1, 0)),
            pl.BlockSpec((tm, d), lambda i: (i, 0)),
            pl.BlockSpec((None, 1, d), lambda i: (row_of(i), 0, k_gate)),
        ],
        out_specs=pl.BlockSpec((tm, d), lambda i: (i, 0)),
        compiler_params=_params(("parallel",), 48),
        name="outproj",
    )(a1, a2, w, w, x, mods)


def _mlp_kernel(*refs, final, n_k):
    fg_ref = None
    if final:
        (x_ref, xn_ref, g_ref, sh_ref, sc_ref, shn_ref, scn_ref, gate_ref, w1_ref, w2_ref, fg_ref,
         o_ref, h_ref, r_ref) = refs
    else:
        (x_ref, xn_ref, g_ref, sh_ref, sc_ref, shn_ref, scn_ref, gate_ref, w1_ref, w2_ref,
         o_ref, h_ref, r_ref) = refs
    i, k = pl.program_id(0), pl.program_id(1)
    cur = i % 2

    def ff_chunk():
        a = jnp.square(jnp.maximum(_dot(h_ref[cur], w1_ref[...]), 0.0)).astype(BF16)
        return _dot(a, w2_ref[...])

    _first_block_norm(x_ref, r_ref, h_ref, g_ref[...], sc_ref[...], sh_ref[...], (i == 0) & (k == 0))

    @pl.when(k == 0)
    def _():
        o_ref[...] = ff_chunk()

    @pl.when((k > 0) & (k < n_k - 1))
    def _():
        o_ref[...] += ff_chunk()

    @pl.when(k == n_k - 1)
    def _():
        o_ref[...] += ff_chunk()
        _next_block_norm(xn_ref, h_ref, 1 - cur, g_ref[...], scn_ref[...], shn_ref[...])

        def rows(sl):
            out = x_ref[sl, :] + gate_ref[...] * o_ref[sl, :]
            if final:
                out = _rms(out, fg_ref[...])
            o_ref[sl, :] = out

        _for_row_chunks(x_ref.shape[0], rows)


def _mlp(x, g, mods, row_of, w1, w2, layer, *, final_g=None, tm=512, tf=1024, vmem_mib=56):
    t, d = x.shape
    n_i, n_k = t // tm, w1.shape[2] // tf
    assert n_k >= 2
    nxt = lambda i: jnp.minimum(i + 1, n_i - 1)
    in_specs = [
        pl.BlockSpec((tm, d), lambda i, k: (i, 0)),
        pl.BlockSpec((tm, d), lambda i, k: (nxt(i), 0)),
        pl.BlockSpec((1, d), lambda i, k: (0, 0)),
        pl.BlockSpec((None, 1, d), lambda i, k: (row_of(i), 0, 3)),
        pl.BlockSpec((None, 1, d), lambda i, k: (row_of(i), 0, 4)),
        pl.BlockSpec((None, 1, d), lambda i, k: (row_of(nxt(i)), 0, 3)),
        pl.BlockSpec((None, 1, d), lambda i, k: (row_of(nxt(i)), 0, 4)),
        pl.BlockSpec((None, 1, d), lambda i, k: (row_of(i), 0, 5)),
        pl.BlockSpec((None, d, tf), lambda i, k: (layer, 0, k)),
        pl.BlockSpec((None, tf, d), lambda i, k: (layer, k, 0)),
    ]
    args = [x, x, g.reshape(1, d), mods, mods, mods, mods, mods, w1, w2]
    if final_g is not None:
        in_specs.append(pl.BlockSpec((1, d), lambda i, k: (0, 0)))
        args.append(final_g.reshape(1, d))
    return pl.pallas_call(
        functools.partial(_mlp_kernel, final=final_g is not None, n_k=n_k),
        out_shape=jax.ShapeDtypeStruct((t, d), F32),
        grid=(n_i, n_k),
        in_specs=in_specs,
        out_specs=pl.BlockSpec((tm, d), lambda i, k: (i, 0)),
        scratch_shapes=[pltpu.VMEM((2, tm, d), BF16), pltpu.VMEM((tm, 1), F32)],
        compiler_params=_params(("arbitrary", "arbitrary"), vmem_mib),
        name="mlp",
    )(*args)


def _axial_angles(n_tokens, rot_dim):
    t = jnp.arange(n_tokens)
    row = (t // GRID_W).astype(F32)
    col = (t % GRID_W).astype(F32)
    n_freq = rot_dim // 4
    inv_freq = ROPE_BASE ** (-jnp.arange(n_freq, dtype=F32) / n_freq)
    ang = jnp.concatenate([row[:, None] * inv_freq, col[:, None] * inv_freq], axis=-1)
    return jnp.cos(ang), jnp.sin(ang)


def _swap_halves(w):
    half = w.shape[-1] // 2
    return jnp.concatenate([w[..., half:], w[..., :half]], axis=-1)


def kernel(x, c, ctx, c_ctx, norm1_g, w_mod, b_mod, norm2_g, w_ff1, w_ff2, even_w_in, mla_q_norm_g,
           mla_w_uq, mla_kv_norm_g, mla_w_ukv, gmlp_w_sp, gmlp_b_sp, even_w_out, odd_w_in, swa_sinks,
           odd_w_out, final_norm_g):
    batch, n_lat, d = x.shape
    n_ctx = ctx.shape[1]
    assert batch + 1 <= MOD_ROWS
    xl = x.reshape(batch * n_lat, d)
    xc = ctx.reshape(batch * n_ctx, d)

    tm = 1024

    def lat_row(block_rows):
        per_sample = n_lat // block_rows
        return lambda i: i // per_sample

    ctx_row = lambda i: batch

    cvec = jnp.concatenate([c, c_ctx[None, :], jnp.zeros((MOD_ROWS - batch - 1, d), F32)], axis=0)
    mods = _modulation(cvec, w_mod, b_mod)
    mods = mods.reshape(mods.shape[0], MOD_ROWS, 1, N_MOD * d)

    cos_m, sin_m = _axial_angles(n_lat, MLA_ROPE)
    zeros64 = jnp.zeros((n_lat, LANES // 2), F32)
    fold_cos = jnp.concatenate([cos_m, cos_m, zeros64], axis=-1)
    fold_sin = jnp.concatenate([-sin_m, sin_m, zeros64], axis=-1)
    keep = jnp.concatenate([jnp.ones((tm, LANES // 2), F32), jnp.zeros((tm, LANES // 2), F32)], axis=-1)
    drop = jnp.zeros((tm, LANES), F32)

    w_in = even_w_in[0]
    cq_w, ckv_w = w_in[:, :MLA_LORA], w_in[:, MLA_LORA:2 * MLA_LORA]
    kr_w = w_in[:, 2 * MLA_LORA:2 * MLA_LORA + MLA_ROPE]
    gm_w = w_in[:, 2 * MLA_LORA + MLA_ROPE:]
    w_in0 = jnp.concatenate([cq_w, ckv_w, gm_w, kr_w, _swap_halves(kr_w)], axis=-1).astype(BF16)

    wq = mla_w_uq[0].reshape(MLA_LORA, MLA_HEADS, MLA_NOPE + MLA_ROPE)
    wq_rope = wq[..., MLA_NOPE:]
    w_uq = jnp.concatenate([wq[..., :MLA_NOPE], wq_rope, _swap_halves(wq_rope)], axis=-1)
    w_uq = (w_uq * (MLA_SCALE * LOG2E)).reshape(MLA_LORA, MLA_HEADS * MLA_QK_PAD).astype(BF16)
    q_groups = tuple(range(1, 2 * MLA_HEADS, 2))
    w_ukv = mla_w_ukv[0].astype(BF16)
    w_sp = gmlp_w_sp[0].astype(BF16)
    b_sp = gmlp_b_sp[0][:, :, None]
    tm_i = 512

    def layer0_tokens(xs, tabs, latent):
        row_of = lat_row(tm_i) if latent else ctx_row
        pos_blocks = n_lat // tm_i if latent else 1
        return _even_in(xs, norm1_g[0], mods[0], row_of, w_in0, tabs, pos_blocks, mla_q_norm_g[0],
                        mla_kv_norm_g[0], w_uq, w_ukv, w_sp, b_sp, tm=tm_i, lora=MLA_LORA,
                        q_groups=q_groups)

    kr_l, q_l, kv_l, gm_l = layer0_tokens(xl, (fold_cos, fold_sin), True)
    kr_c, q_c, kv_c, gm_c = layer0_tokens(xc, (keep, drop), False)
    cast_steps = batch * MLA_HEADS
    late = (w_ff1, w_ff2, even_w_out, odd_w_out)
    slabs = [w.reshape(cast_steps, -1, w.shape[-1]) for w in late]
    att_l, att_c, late_bf16 = _mla_attention(q_l, q_c, kv_l, kv_c, kr_l, kr_c, batch=batch, casts=slabs)
    w1_all, w2_all, w_out0, w_out1 = [b.reshape(w.shape) for b, w in zip(late_bf16, late)]
    w_out0, w_out1 = w_out0[0], w_out1[0]

    tm_o = 512
    xl = _outproj(att_l, 0, gm_l, 0, w_out0, xl, mods[0], lat_row(tm_o), 2, tm=tm_o)
    xc = _outproj(att_c, 0, gm_c, 0, w_out0, xc, mods[0], ctx_row, 2, tm=tm_o)
    tm_f = 512
    xl = _mlp(xl, norm2_g[0], mods[0], lat_row(tm_f), w1_all, w2_all, 0, tm=tm_f)
    xc = _mlp(xc, norm2_g[0], mods[0], ctx_row, w1_all, w2_all, 0, tm=tm_f)

    cos_s, sin_s = _axial_angles(n_lat, SWA_HEAD_DIM)
    pair_cos = jnp.concatenate([cos_s] * 4, axis=-1)
    pair_sin = jnp.concatenate([-sin_s, -sin_s, sin_s, sin_s], axis=-1)
    half = SWA_HEAD_DIM // 2
    w_in = odd_w_in[0]
    wq = (w_in[:, :SWA_Q_COLS] * (SWA_SCALE * LOG2E)).reshape(d, SWA_HEADS // 2, 2, 2, half)
    wq = wq.transpose(0, 1, 3, 2, 4).reshape(d, SWA_Q_COLS)
    wk = w_in[:, SWA_Q_COLS:SWA_Q_COLS + SWA_KV_COLS].reshape(d, SWA_KV_HEADS, 2, 1, half)
    wk = jnp.broadcast_to(wk, (d, SWA_KV_HEADS, 2, 2, half)).reshape(d, 2 * SWA_KV_COLS)
    wv = w_in[:, SWA_Q_COLS + SWA_KV_COLS:]
    w_in1 = jnp.concatenate([wq, wk, wv], axis=-1).astype(BF16)
    w_kv1 = jnp.concatenate([wk, wv], axis=-1).astype(BF16)
    rot_groups = (SWA_Q_COLS + 2 * SWA_KV_COLS) // LANES
    m1 = mods[1]
    z1_l = _proj(xl, norm1_g[1], m1, lat_row(tm_i), w_in1, tm=tm_i, rot_groups=rot_groups,
                 tabs=(pair_cos, pair_sin), pos_blocks=n_lat // tm_i, vmem_mib=56, name="odd_in_proj")
    z1_c = _proj(xc, norm1_g[1], m1, ctx_row, w_kv1, tm=tm, name="odd_ctx_kv_proj")
    att = _swa_attention(z1_l, z1_c, swa_sinks[0], batch=batch, n_lat=n_lat, n_ctx=n_ctx)
    xl = _outproj(att, 0, att, 1, w_out1, xl, m1, lat_row(tm_o), 2, tm=tm_o)
    out = _mlp(xl, norm2_g[1], m1, lat_row(tm_f), w1_all, w2_all, 1, final_g=final_norm_g, tm=tm_f)
    return out.reshape(batch, n_lat, d)
```

```python
import functools
import math

import jax
import jax.numpy as jnp
import numpy as np
from jax import lax
from jax.experimental import pallas as pl
from jax.experimental.pallas import tpu as pltpu

F32 = jnp.float32
BF16 = jnp.bfloat16

LANES = 128
MIB = 1 << 20

GRID_W = 64
N_MOD = 6
NORM_EPS = 1e-6
ROPE_BASE = 10000.0
NEG_INF = -1e30

MLA_HEADS = 8
MLA_LORA = 512
MLA_NOPE = 128
MLA_ROPE = 64
MLA_V = 128
MLA_QK_PAD = 2 * LANES
MLA_SCALE = 1.0 / math.sqrt(MLA_NOPE + MLA_ROPE)
MLA_Q_SUB = 512

GMLP_GROUPS = 8
GMLP_DIM = 128
GMLP_CHUNK = 128
GMLP_WIDTH = GMLP_GROUPS * GMLP_DIM
GMLP_BATCH = 2

SWA_HEADS = 32
SWA_KV_HEADS = 4
SWA_GROUP = SWA_HEADS // SWA_KV_HEADS
SWA_HEAD_DIM = 64
SWA_WINDOW = 128
SWA_BLOCK = 128
SWA_SPAN = SWA_BLOCK + 2 * SWA_WINDOW
SWA_SCALE = 1.0 / math.sqrt(SWA_HEAD_DIM)
SWA_Q_COLS = SWA_HEADS * SWA_HEAD_DIM
SWA_KV_COLS = SWA_KV_HEADS * SWA_HEAD_DIM
SWA_UNIT = SWA_GROUP

MOD_ROWS = 16
ROW_CHUNK = 128
STATS_CHUNK = 256
NARROW_ROW = 512
AHEAD_ROWS = 64
ONES_ROWS = 16
LOG2E = math.log2(math.e)


def _params(semantics, vmem_mib):
    return pltpu.CompilerParams(dimension_semantics=semantics, vmem_limit_bytes=vmem_mib * MIB)


def _rms(x, g):
    return x * lax.rsqrt(jnp.mean(x * x, axis=-1, keepdims=True) + NORM_EPS) * g


def _for_row_chunks(n_rows, fn, rows=ROW_CHUNK):
    rows = min(rows, n_rows)

    def body(r, carry):
        fn(pl.ds(pl.multiple_of(r * rows, rows), rows))
        return carry

    lax.fori_loop(0, n_rows // rows, body, 0)


def _norm_modulate(x_ref, r_ref, store, g, scale=None, shift=None):
    n_rows = x_ref.shape[0]
    gain = g if scale is None else g * (1.0 + scale)

    if x_ref.shape[1] <= NARROW_ROW:
        def whole(sl):
            y = _rms(x_ref[sl, :], gain)
            store(sl, y if shift is None else y + shift)

        _for_row_chunks(n_rows, whole)
        return

    def stats(sl):
        x = x_ref[sl, :]
        r_ref[sl, :] = lax.rsqrt(jnp.mean(x * x, axis=-1, keepdims=True) + NORM_EPS)

    _for_row_chunks(n_rows, stats, rows=STATS_CHUNK)

    def apply(sl):
        y = x_ref[sl, :] * r_ref[sl, :] * gain
        store(sl, y if shift is None else y + shift)

    _for_row_chunks(n_rows, apply)


def _first_block_norm(x_ref, r_ref, h_ref, g, scale, shift):
    @pl.when(pl.program_id(0) == 0)
    def _():
        def store(sl, y):
            h_ref[0, sl, :] = y.astype(BF16)

        _norm_modulate(x_ref, r_ref, store, g, scale, shift)


def _next_block_norm(xn_ref, h_ref, slot, g, scale, shift):
    gain = g * (1.0 + scale)
    for r0 in range(0, xn_ref.shape[0], AHEAD_ROWS):
        rows = slice(r0, r0 + AHEAD_ROWS)
        x = xn_ref[rows, :]
        r = lax.rsqrt(jnp.mean(x * x, axis=-1, keepdims=True) + NORM_EPS)
        h_ref[slot, rows, :] = (x * r * gain + shift).astype(BF16)


def _dot(a, b):
    return jnp.dot(a, b, preferred_element_type=F32)


def _dot_nt(a, b):
    return lax.dot_general(a, b, (((1,), (1,)), ((), ())), preferred_element_type=F32)


def _mod_kernel(c_ref, w_ref, b_ref, o_ref):
    c = c_ref[...]
    s = (c * jax.nn.sigmoid(c)).astype(BF16)
    o_ref[...] = _dot(s, w_ref[...].astype(BF16)) + b_ref[...]


def _modulation(cvec, w_mod, b_mod, *, tn=1024):
    depth, d, n = w_mod.shape
    return pl.pallas_call(
        _mod_kernel,
        out_shape=jax.ShapeDtypeStruct((depth, MOD_ROWS, n), F32),
        grid=(depth, n // tn),
        in_specs=[
            pl.BlockSpec((MOD_ROWS, d), lambda l, j: (0, 0)),
            pl.BlockSpec((None, d, tn), lambda l, j: (l, 0, j)),
            pl.BlockSpec((None, 1, tn), lambda l, j: (l, 0, j)),
        ],
        out_specs=pl.BlockSpec((None, MOD_ROWS, tn), lambda l, j: (l, 0, j)),
        compiler_params=_params(("parallel", "parallel"), 40),
        name="modulation",
    )(cvec, w_mod, b_mod.reshape(depth, 1, n))


def _proj_kernel(*refs, rot_groups):
    if rot_groups:
        (x_ref, xn_ref, g_ref, sh_ref, sc_ref, shn_ref, scn_ref, w_ref, cos_ref, sin_ref, o_ref, h_ref,
         r_ref) = refs
    else:
        x_ref, xn_ref, g_ref, sh_ref, sc_ref, shn_ref, scn_ref, w_ref, o_ref, h_ref, r_ref = refs
    cur = pl.program_id(0) % 2
    _first_block_norm(x_ref, r_ref, h_ref, g_ref[...], sc_ref[...], sh_ref[...])
    z = _dot(h_ref[cur], w_ref[...])
    _next_block_norm(xn_ref, h_ref, 1 - cur, g_ref[...], scn_ref[...], shn_ref[...])

    for gi in range(z.shape[1] // LANES):
        sl = slice(gi * LANES, (gi + 1) * LANES)
        t = z[:, sl]
        if gi < rot_groups:
            t = t * cos_ref[...] + pltpu.roll(t, LANES // 2, 1) * sin_ref[...]
        o_ref[:, sl] = t.astype(o_ref.dtype)


def _proj(x, g, mods, row_of, w, *, tm, rot_groups=0, tabs=(), pos_blocks=1, vmem_mib=48,
          name="proj"):
    t, k = x.shape
    n = w.shape[1]
    n_i = t // tm
    nxt = lambda i: jnp.minimum(i + 1, n_i - 1)
    in_specs = [
        pl.BlockSpec((tm, k), lambda i: (i, 0)),
        pl.BlockSpec((tm, k), lambda i: (nxt(i), 0)),
        pl.BlockSpec((1, k), lambda i: (0, 0)),
        pl.BlockSpec((None, 1, k), lambda i: (row_of(i), 0, 0)),
        pl.BlockSpec((None, 1, k), lambda i: (row_of(i), 0, 1)),
        pl.BlockSpec((None, 1, k), lambda i: (row_of(nxt(i)), 0, 0)),
        pl.BlockSpec((None, 1, k), lambda i: (row_of(nxt(i)), 0, 1)),
        pl.BlockSpec((k, n), lambda i: (0, 0), pipeline_mode=pl.Buffered(1)),
    ]
    in_specs += [pl.BlockSpec((tm, LANES), lambda i: (i % pos_blocks, 0)) for _ in tabs]
    return pl.pallas_call(
        functools.partial(_proj_kernel, rot_groups=rot_groups),
        out_shape=jax.ShapeDtypeStruct((t, n), BF16),
        grid=(n_i,),
        in_specs=in_specs,
        out_specs=pl.BlockSpec((tm, n), lambda i: (i, 0)),
        scratch_shapes=[pltpu.VMEM((2, tm, k), BF16), pltpu.VMEM((tm, 1), F32)],
        compiler_params=_params(("arbitrary",), vmem_mib),
        name=name,
    )(x, x, g.reshape(1, k), mods, mods, mods, mods, w, *tabs)


def _even_in_kernel(x_ref, xn_ref, g_ref, sh_ref, sc_ref, shn_ref, scn_ref, w_ref, cos_ref, sin_ref,
                    gq_ref, gkv_ref, wq_ref, wkv_ref, wsp_ref, bsp_ref, kr_ref, q_ref, kv_ref, gm_ref,
                    h_ref, r_ref, *, lora, q_groups):
    cur = pl.program_id(0) % 2
    _first_block_norm(x_ref, r_ref, h_ref, g_ref[...], sc_ref[...], sh_ref[...])
    z = _dot(h_ref[cur], w_ref[...])
    _next_block_norm(xn_ref, h_ref, 1 - cur, g_ref[...], scn_ref[...], shn_ref[...])

    def rotated(t):
        return t * cos_ref[...] + pltpu.roll(t, LANES // 2, 1) * sin_ref[...]

    gm_lo, gm_hi = 2 * lora, 2 * lora + 2 * GMLP_WIDTH
    kr_ref[...] = rotated(z[:, gm_hi:])
    _gmlp_tiles(z[:, gm_lo:gm_lo + GMLP_WIDTH], z[:, gm_lo + GMLP_WIDTH:gm_hi], wsp_ref, bsp_ref, gm_ref)

    q = _dot(_rms(z[:, :lora], gq_ref[...]).astype(BF16), wq_ref[...])
    for gi in range(q.shape[1] // LANES):
        sl = slice(gi * LANES, (gi + 1) * LANES)
        q_ref[:, sl] = (rotated(q[:, sl]) if gi in q_groups else q[:, sl]).astype(q_ref.dtype)
    kv = _dot(_rms(z[:, lora:2 * lora], gkv_ref[...]).astype(BF16), wkv_ref[...])
    kv_ref[...] = kv.astype(kv_ref.dtype)


def _even_in(x, g, mods, row_of, w, tabs, pos_blocks, gq, gkv, wq, wkv, w_sp, b_sp, *, tm, lora,
             q_groups, vmem_mib=58):
    t, d = x.shape
    n = w.shape[1]
    assert n == 2 * lora + 2 * GMLP_WIDTH + LANES
    once = {"pipeline_mode": pl.Buffered(1)}
    row_blk = lambda i: (i, 0)
    fixed = lambda i: (0, 0)
    n_i = t // tm
    nxt = lambda i: jnp.minimum(i + 1, n_i - 1)
    return pl.pallas_call(
        functools.partial(_even_in_kernel, lora=lora, q_groups=q_groups),
        out_shape=[jax.ShapeDtypeStruct((t, LANES), F32),
                   jax.ShapeDtypeStruct((t, wq.shape[1]), BF16),
                   jax.ShapeDtypeStruct((t, wkv.shape[1]), BF16),
                   jax.ShapeDtypeStruct((t, GMLP_WIDTH), BF16)],
        grid=(n_i,),
        in_specs=[
            pl.BlockSpec((tm, d), row_blk),
            pl.BlockSpec((tm, d), lambda i: (nxt(i), 0)),
            pl.BlockSpec((1, d), fixed),
            pl.BlockSpec((None, 1, d), lambda i: (row_of(i), 0, 0)),
            pl.BlockSpec((None, 1, d), lambda i: (row_of(i), 0, 1)),
            pl.BlockSpec((None, 1, d), lambda i: (row_of(nxt(i)), 0, 0)),
            pl.BlockSpec((None, 1, d), lambda i: (row_of(nxt(i)), 0, 1)),
            pl.BlockSpec((d, n), fixed, **once),
            pl.BlockSpec((tm, LANES), lambda i: (i % pos_blocks, 0)),
            pl.BlockSpec((tm, LANES), lambda i: (i % pos_blocks, 0)),
            pl.BlockSpec((1, lora), fixed),
            pl.BlockSpec((1, lora), fixed),
            pl.BlockSpec(wq.shape, fixed, **once),
            pl.BlockSpec(wkv.shape, fixed, **once),
            pl.BlockSpec(w_sp.shape, lambda i: (0, 0, 0)),
            pl.BlockSpec(b_sp.shape, lambda i: (0, 0, 0)),
        ],
        out_specs=[pl.BlockSpec((tm, LANES), row_blk),
                   pl.BlockSpec((tm, wq.shape[1]), row_blk),
                   pl.BlockSpec((tm, wkv.shape[1]), row_blk),
                   pl.BlockSpec((tm, GMLP_WIDTH), row_blk)],
        scratch_shapes=[pltpu.VMEM((2, tm, d), BF16), pltpu.VMEM((tm, 1), F32)],
        compiler_params=_params(("arbitrary",), vmem_mib),
        name="even_in_proj",
    )(x, x, g.reshape(1, d), mods, mods, mods, mods, w, tabs[0], tabs[1], gq.reshape(1, lora),
      gkv.reshape(1, lora), wq, wkv, w_sp, b_sp)


def _mla_kernel(*refs, n_lat, n_ctx, n_cast):
    n_in = 8
    q_ref, qc_ref, knl_ref, vl_ref, krl_ref, knc_ref, vc_ref, krc_ref = refs[:n_in]
    cast_in = refs[n_in:n_in + n_cast]
    o_ref, oc_ref = refs[n_in + n_cast:n_in + n_cast + 2]
    cast_out = refs[n_in + n_cast + 2:n_in + 2 * n_cast + 2]
    k_scr, vt_scr = refs[n_in + 2 * n_cast + 2:]
    for src, dst in zip(cast_in, cast_out):
        dst[...] = src[...].astype(dst.dtype)

    k_scr[0:n_lat, 0:LANES] = knl_ref[...]
    k_scr[0:n_lat, LANES:] = krl_ref[...].astype(BF16)
    vt_scr[0:MLA_V, 0:n_lat] = vl_ref[...].astype(F32).T.astype(BF16)
    k_scr[n_lat:, 0:LANES] = knc_ref[...]
    k_scr[n_lat:, LANES:] = krc_ref[...].astype(BF16)
    vt_scr[0:MLA_V, n_lat:] = vc_ref[...].astype(F32).T.astype(BF16)
    vt_scr[MLA_V:, :] = jnp.ones((ONES_ROWS, n_lat + n_ctx), BF16)

    def weights(s):
        return jnp.exp2(s - jnp.max(s, axis=0, keepdims=True)).astype(BF16)

    def normalised(o):
        return (o[0:MLA_V] / o[MLA_V:MLA_V + 1]).T

    subs = [slice(r, r + MLA_Q_SUB) for r in range(0, n_lat, MLA_Q_SUB)]

    def scores(i):
        return _dot_nt(k_scr[...], q_ref[subs[i], :])

    n = len(subs)
    s = [scores(i) if i < 2 else None for i in range(n)]
    p = weights(s[0])
    for i, rows in enumerate(subs):
        if i + 2 < n:
            s[i + 2] = scores(i + 2)
        o = _dot(vt_scr[...], p)
        if i + 1 < n:
            p = weights(s[i + 1])
        o_ref[rows, :] = normalised(o).astype(o_ref.dtype)

    p_c = weights(_dot_nt(k_scr[n_lat:, :], qc_ref[...]))
    oc_ref[...] = normalised(_dot(vt_scr[:, n_lat:], p_c)).astype(oc_ref.dtype)


def _mla_attention(q_l, q_c, kv_l, kv_c, kr_l, kr_c, *, batch, casts=()):
    n_lat, n_ctx = q_l.shape[0] // batch, q_c.shape[0] // batch
    assert n_lat % MLA_Q_SUB == 0
    head = lambda b, h: (b, h)
    k_of = lambda b, h: (b, 2 * h)
    v_of = lambda b, h: (b, 2 * h + 1)
    shared = lambda b, h: (b, 0)
    in_specs = [
        pl.BlockSpec((n_lat, MLA_QK_PAD), head),
        pl.BlockSpec((n_ctx, MLA_QK_PAD), head),
        pl.BlockSpec((n_lat, LANES), k_of),
        pl.BlockSpec((n_lat, LANES), v_of),
        pl.BlockSpec((n_lat, LANES), shared),
        pl.BlockSpec((n_ctx, LANES), k_of),
        pl.BlockSpec((n_ctx, LANES), v_of),
        pl.BlockSpec((n_ctx, LANES), shared),
    ]
    args = [q_l, q_c, kv_l, kv_l, kr_l, kv_c, kv_c, kr_c]
    out_shape = [jax.ShapeDtypeStruct((batch * n_lat, MLA_HEADS * MLA_V), BF16),
                 jax.ShapeDtypeStruct((batch * n_ctx, MLA_HEADS * MLA_V), BF16)]
    out_specs = [pl.BlockSpec((n_lat, MLA_V), head), pl.BlockSpec((n_ctx, MLA_V), head)]
    for w in casts:
        assert w.shape[0] == batch * MLA_HEADS
        slab = pl.BlockSpec((None,) + w.shape[1:], lambda b, h: (b * MLA_HEADS + h, 0, 0))
        in_specs.append(slab)
        args.append(w)
        out_specs.append(slab)
        out_shape.append(jax.ShapeDtypeStruct(w.shape, BF16))
    n_keys = n_lat + n_ctx
    outs = pl.pallas_call(
        functools.partial(_mla_kernel, n_lat=n_lat, n_ctx=n_ctx, n_cast=len(casts)),
        out_shape=out_shape,
        grid=(batch, MLA_HEADS),
        in_specs=in_specs,
        out_specs=out_specs,
        scratch_shapes=[pltpu.VMEM((n_keys, MLA_QK_PAD), BF16),
                        pltpu.VMEM((MLA_V + ONES_ROWS, n_keys), BF16)],
        compiler_params=_params(("parallel", "parallel"), 56),
        name="mla_attention",
    )(*args)
    return outs[0], outs[1], outs[2:]


def _gelu(x):
    c = math.sqrt(2.0 / math.pi)
    return x * (0.5 + 0.5 * jnp.tanh(x * (c + (c * 0.044715) * (x * x))))


def _gmlp_tiles(u, v, w_ref, b_ref, o_ref):
    for n0 in range(0, u.shape[0] // GMLP_CHUNK, GMLP_BATCH):
        tiles = [(slice(n * GMLP_CHUNK, (n + 1) * GMLP_CHUNK), slice(g * GMLP_DIM, (g + 1) * GMLP_DIM), g)
                 for n in range(n0, n0 + GMLP_BATCH) for g in range(GMLP_GROUPS)]
        x = [_gelu(v[rows, cols]) for rows, cols, _ in tiles]
        xc = [t - jnp.mean(t, axis=-1, keepdims=True) for t in x]
        var = [jnp.mean(t * t, axis=-1, keepdims=True) for t in xc]
        xn = [(t * lax.rsqrt(s + NORM_EPS)).astype(BF16) for t, s in zip(xc, var)]
        for (rows, cols, g), t in zip(tiles, xn):
            mixed = _dot(w_ref[g], t) + b_ref[g]
            o_ref[rows, cols] = (_gelu(u[rows, cols]) * mixed).astype(o_ref.dtype)


def _swa_kernel(sink_ref, q_ref, kl_ref, vl_ref, kc_ref, vc_ref, o_ref, *, tq, n_lat):
    start = pl.program_id(1) * tq
    span0 = pl.multiple_of(jnp.clip(start - SWA_WINDOW, 0, n_lat - SWA_SPAN), SWA_WINDOW)
    k_span = kl_ref[pl.ds(span0, SWA_SPAN), :]
    k_ctx = kc_ref[...]
    vt_span = vl_ref[pl.ds(span0, SWA_SPAN), :].astype(F32).T.astype(BF16)
    vt_ctx = vc_ref[...].astype(F32).T.astype(BF16)
    ones_span = jnp.ones((ONES_ROWS, vt_span.shape[1]), BF16)
    ones_ctx = jnp.ones((ONES_ROWS, vt_ctx.shape[1]), BF16)

    k_pos = span0 + lax.broadcasted_iota(jnp.int32, (SWA_SPAN, tq), 0)
    q_pos = start + lax.broadcasted_iota(jnp.int32, (SWA_SPAN, tq), 1)
    bias = jnp.where(jnp.abs(k_pos - q_pos) <= SWA_WINDOW, 0.0, NEG_INF).astype(F32)
    bias = jnp.concatenate([bias] * SWA_UNIT, axis=1)

    lane = lax.broadcasted_iota(jnp.int32, (tq, LANES), 1)
    first = (lane % SWA_HEAD_DIM) < (SWA_HEAD_DIM // 2)
    keep = [jnp.where(first, 1.0, 0.0).astype(BF16), jnp.where(first, 0.0, 1.0).astype(BF16)]

    units = [range(h0, h0 + SWA_UNIT) for h0 in range(0, SWA_HEADS, SWA_UNIT)]

    def scores(heads):
        g = heads[0] // SWA_GROUP
        k_cols = slice(g * LANES, (g + 1) * LANES)
        q = jnp.concatenate(
            [q_ref[:, (h // 2) * LANES:(h // 2 + 1) * LANES] * keep[h % 2] for h in heads], axis=0)
        return _dot_nt(k_span[:, k_cols], q) + bias, _dot_nt(k_ctx[:, k_cols], q)

    s_next = scores(units[0])
    for u, heads in enumerate(units):
        g = heads[0] // SWA_GROUP
        kv_cols = slice(g * SWA_HEAD_DIM, (g + 1) * SWA_HEAD_DIM)
        s_l, s_c = s_next
        if u + 1 < len(units):
            s_next = scores(units[u + 1])
        sink = jnp.concatenate([jnp.full((1, tq), sink_ref[h] * LOG2E, F32) for h in heads], axis=1)
        m = jnp.maximum(jnp.maximum(jnp.max(s_l, axis=0, keepdims=True),
                                    jnp.max(s_c, axis=0, keepdims=True)), sink)
        p_l = jnp.exp2(s_l - m).astype(BF16)
        p_c = jnp.exp2(s_c - m).astype(BF16)
        vt_l = jnp.concatenate([vt_span[kv_cols, :], ones_span], axis=0)
        vt_c = jnp.concatenate([vt_ctx[kv_cols, :], ones_ctx], axis=0)
        o = _dot(vt_l, p_l) + _dot(vt_c, p_c)
        dim = SWA_HEAD_DIM
        o = o[0:dim] / (o[dim:dim + 1] + jnp.exp2(sink - m))
        for pair in range(SWA_UNIT // 2):
            cols = [slice((2 * pair + r) * tq, (2 * pair + r + 1) * tq) for r in range(2)]
            both = jnp.concatenate([o[:, cols[0]], o[:, cols[1]]], axis=0)
            group = heads[0] // 2 + pair
            o_ref[:, group * LANES:(group + 1) * LANES] = both.T.astype(o_ref.dtype)


def _swa_attention(z_l, z_c, sinks, *, batch, n_lat, n_ctx, tq=SWA_BLOCK):
    nq_blocks = n_lat // tq
    k_cols = 2 * SWA_KV_COLS
    return pl.pallas_call(
        functools.partial(_swa_kernel, tq=tq, n_lat=n_lat),
        out_shape=jax.ShapeDtypeStruct((batch * n_lat, SWA_Q_COLS), BF16),
        grid=(batch, nq_blocks),
        in_specs=[
            pl.BlockSpec(memory_space=pltpu.SMEM),
            pl.BlockSpec((tq, SWA_Q_COLS), lambda b, i: (b * nq_blocks + i, 0)),
            pl.BlockSpec((n_lat, k_cols), lambda b, i: (b, SWA_Q_COLS // k_cols)),
            pl.BlockSpec((n_lat, SWA_KV_COLS), lambda b, i: (b, (SWA_Q_COLS + k_cols) // SWA_KV_COLS)),
            pl.BlockSpec((n_ctx, k_cols), lambda b, i: (b, 0)),
            pl.BlockSpec((n_ctx, SWA_KV_COLS), lambda b, i: (b, k_cols // SWA_KV_COLS)),
        ],
        out_specs=pl.BlockSpec((tq, SWA_Q_COLS), lambda b, i: (b * nq_blocks + i, 0)),
        compiler_params=_params(("parallel", "arbitrary"), 48),
        name="swa_attention",
    )(sinks, z_l, z_l, z_l, z_c, z_c)


def _outproj_kernel(a1_ref, a2_ref, w1_ref, w2_ref, x_ref, gate_ref, o_ref):
    y = _dot(a1_ref[...], w1_ref[...]) + _dot(a2_ref[...], w2_ref[...])
    o_ref[...] = x_ref[...] + gate_ref[...] * y


def _outproj(a1, a1_col, a2, a2_col, w, x, mods, row_of, k_gate, *, tm=512):
    t, d = x.shape
    half = w.shape[0] // 2
    return pl.pallas_call(
        _outproj_kernel,
        out_shape=jax.ShapeDtypeStruct((t, d), F32),
        grid=(t // tm,),
        in_specs=[
            pl.BlockSpec((tm, half), lambda i: (i, a1_col)),
            pl.BlockSpec((tm, half), lambda i: (i, a2_col)),
            pl.BlockSpec((half, d), lambda i: (0, 0)),
            pl.BlockSpec((half, d), lambda i: (1, 0)),
            pl.BlockSpec((tm, d), lambda i: (i, 0)),
            pl.BlockSpec((None, 1, d), lambda i: (row_of(i), 0, k_gate)),
        ],
        out_specs=pl.BlockSpec((tm, d), lambda i: (i, 0)),
        compiler_params=_params(("parallel",), 48),
        name="outproj",
    )(a1, a2, w, w, x, mods)


def _mlp_kernel(*refs, final):
    if final:
        x_ref, g_ref, sh_ref, sc_ref, gate_ref, w1_ref, w2_ref, fg_ref, o_ref, h_ref, r_ref = refs
    else:
        x_ref, g_ref, sh_ref, sc_ref, gate_ref, w1_ref, w2_ref, o_ref, h_ref, r_ref = refs
    k = pl.program_id(1)

    def ff_chunk():
        a = jnp.square(jnp.maximum(_dot(h_ref[...], w1_ref[...]), 0.0)).astype(BF16)
        return _dot(a, w2_ref[...])

    @pl.when(k == 0)
    def _():
        def store(sl, y):
            h_ref[sl, :] = y.astype(BF16)

        _norm_modulate(x_ref, r_ref, store, g_ref[...], sc_ref[...], sh_ref[...])
        o_ref[...] = ff_chunk()

    @pl.when(k > 0)
    def _():
        o_ref[...] += ff_chunk()

    @pl.when(k == pl.num_programs(1) - 1)
    def _():
        def rows(sl):
            out = x_ref[sl, :] + gate_ref[...] * o_ref[sl, :]
            if final:
                out = _rms(out, fg_ref[...])
            o_ref[sl, :] = out

        _for_row_chunks(x_ref.shape[0], rows)


def _mlp(x, g, mods, row_of, w1, w2, layer, *, final_g=None, tm=512, tf=1024, vmem_mib=56):
    t, d = x.shape
    n_k = w1.shape[2] // tf
    in_specs = [
        pl.BlockSpec((tm, d), lambda i, k: (i, 0)),
        pl.BlockSpec((1, d), lambda i, k: (0, 0)),
        pl.BlockSpec((None, 1, d), lambda i, k: (row_of(i), 0, 3)),
        pl.BlockSpec((None, 1, d), lambda i, k: (row_of(i), 0, 4)),
        pl.BlockSpec((None, 1, d), lambda i, k: (row_of(i), 0, 5)),
        pl.BlockSpec((None, d, tf), lambda i, k: (layer, 0, k)),
        pl.BlockSpec((None, tf, d), lambda i, k: (layer, k, 0)),
    ]
    args = [x, g.reshape(1, d), mods, mods, mods, w1, w2]
    if final_g is not None:
        in_specs.append(pl.BlockSpec((1, d), lambda i, k: (0, 0)))
        args.append(final_g.reshape(1, d))
    return pl.pallas_call(
        functools.partial(_mlp_kernel, final=final_g is not None),
        out_shape=jax.ShapeDtypeStruct((t, d), F32),
        grid=(t // tm, n_k),
        in_specs=in_specs,
        out_specs=pl.BlockSpec((tm, d), lambda i, k: (i, 0)),
        scratch_shapes=[pltpu.VMEM((tm, d), BF16), pltpu.VMEM((tm, 1), F32)],
        compiler_params=_params(("parallel", "arbitrary"), vmem_mib),
        name="mlp",
    )(*args)


def _axial_angles(n_tokens, rot_dim):
    t = np.arange(n_tokens)
    row = (t // GRID_W).astype(np.float32)
    col = (t % GRID_W).astype(np.float32)
    n_freq = rot_dim // 4
    inv_freq = np.float32(ROPE_BASE) ** (-np.arange(n_freq, dtype=np.float32) / np.float32(n_freq))
    ang = np.concatenate([row[:, None] * inv_freq, col[:, None] * inv_freq], axis=-1)
    return np.cos(ang).astype(np.float32), np.sin(ang).astype(np.float32)


def _swap_halves(w):
    half = w.shape[-1] // 2
    return jnp.concatenate([w[..., half:], w[..., :half]], axis=-1)


def kernel(x, c, ctx, c_ctx, norm1_g, w_mod, b_mod, norm2_g, w_ff1, w_ff2, even_w_in, mla_q_norm_g,
           mla_w_uq, mla_kv_norm_g, mla_w_ukv, gmlp_w_sp, gmlp_b_sp, even_w_out, odd_w_in, swa_sinks,
           odd_w_out, final_norm_g):
    batch, n_lat, d = x.shape
    n_ctx = ctx.shape[1]
    assert batch + 1 <= MOD_ROWS
    xl = x.reshape(batch * n_lat, d)
    xc = ctx.reshape(batch * n_ctx, d)

    tm = 1024

    def lat_row(block_rows):
        per_sample = n_lat // block_rows
        return lambda i: i // per_sample

    ctx_row = lambda i: batch

    cvec = jnp.concatenate([c, c_ctx[None, :], jnp.zeros((MOD_ROWS - batch - 1, d), F32)], axis=0)
    mods = _modulation(cvec, w_mod, b_mod)
    mods = mods.reshape(mods.shape[0], MOD_ROWS, 1, N_MOD * d)

    cos_m, sin_m = _axial_angles(n_lat, MLA_ROPE)
    zeros64 = np.zeros((n_lat, LANES // 2), np.float32)
    fold_cos = jnp.asarray(np.concatenate([cos_m, cos_m, zeros64], axis=-1))
    fold_sin = jnp.asarray(np.concatenate([-sin_m, sin_m, zeros64], axis=-1))
    keep = jnp.concatenate([jnp.ones((tm, LANES // 2), F32), jnp.zeros((tm, LANES // 2), F32)], axis=-1)
    drop = jnp.zeros((tm, LANES), F32)

    w_in = even_w_in[0]
    cq_w, ckv_w = w_in[:, :MLA_LORA], w_in[:, MLA_LORA:2 * MLA_LORA]
    kr_w = w_in[:, 2 * MLA_LORA:2 * MLA_LORA + MLA_ROPE]
    gm_w = w_in[:, 2 * MLA_LORA + MLA_ROPE:]
    w_in0 = jnp.concatenate([cq_w, ckv_w, gm_w, kr_w, _swap_halves(kr_w)], axis=-1).astype(BF16)

    wq = mla_w_uq[0].reshape(MLA_LORA, MLA_HEADS, MLA_NOPE + MLA_ROPE)
    wq_rope = wq[..., MLA_NOPE:]
    w_uq = jnp.concatenate([wq[..., :MLA_NOPE], wq_rope, _swap_halves(wq_rope)], axis=-1)
    w_uq = (w_uq * (MLA_SCALE * LOG2E)).reshape(MLA_LORA, MLA_HEADS * MLA_QK_PAD).astype(BF16)
    q_groups = tuple(range(1, 2 * MLA_HEADS, 2))
    w_ukv = mla_w_ukv[0].astype(BF16)
    w_sp = gmlp_w_sp[0].astype(BF16)
    b_sp = gmlp_b_sp[0][:, :, None]
    tm_i = 512

    def layer0_tokens(xs, tabs, latent):
        row_of = lat_row(tm_i) if latent else ctx_row
        pos_blocks = n_lat // tm_i if latent else 1
        return _even_in(xs, norm1_g[0], mods[0], row_of, w_in0, tabs, pos_blocks, mla_q_norm_g[0],
                        mla_kv_norm_g[0], w_uq, w_ukv, w_sp, b_sp, tm=tm_i, lora=MLA_LORA,
                        q_groups=q_groups)

    kr_l, q_l, kv_l, gm_l = layer0_tokens(xl, (fold_cos, fold_sin), True)
    kr_c, q_c, kv_c, gm_c = layer0_tokens(xc, (keep, drop), False)
    cast_steps = batch * MLA_HEADS
    late = (w_ff1, w_ff2, even_w_out, odd_w_out)
    slabs = [w.reshape(cast_steps, -1, w.shape[-1]) for w in late]
    att_l, att_c, late_bf16 = _mla_attention(q_l, q_c, kv_l, kv_c, kr_l, kr_c, batch=batch, casts=slabs)
    w1_all, w2_all, w_out0, w_out1 = [b.reshape(w.shape) for b, w in zip(late_bf16, late)]
    w_out0, w_out1 = w_out0[0], w_out1[0]

    tm_o = 512
    xl = _outproj(att_l, 0, gm_l, 0, w_out0, xl, mods[0], lat_row(tm_o), 2, tm=tm_o)
    xc = _outproj(att_c, 0, gm_c, 0, w_out0, xc, mods[0], ctx_row, 2, tm=tm_o)
    tm_f = 512
    xl = _mlp(xl, norm2_g[0], mods[0], lat_row(tm_f), w1_all, w2_all, 0, tm=tm_f)
    xc = _mlp(xc, norm2_g[0], mods[0], ctx_row, w1_all, w2_all, 0, tm=tm_f)

    cos_s, sin_s = _axial_angles(n_lat, SWA_HEAD_DIM)
    pair_cos = jnp.asarray(np.concatenate([cos_s] * 4, axis=-1))
    pair_sin = jnp.asarray(np.concatenate([-sin_s, -sin_s, sin_s, sin_s], axis=-1))
    half = SWA_HEAD_DIM // 2
    w_in = odd_w_in[0]
    wq = (w_in[:, :SWA_Q_COLS] * (SWA_SCALE * LOG2E)).reshape(d, SWA_HEADS // 2, 2, 2, half)
    wq = wq.transpose(0, 1, 3, 2, 4).reshape(d, SWA_Q_COLS)
    wk = w_in[:, SWA_Q_COLS:SWA_Q_COLS + SWA_KV_COLS].reshape(d, SWA_KV_HEADS, 2, 1, half)
    wk = jnp.broadcast_to(wk, (d, SWA_KV_HEADS, 2, 2, half)).reshape(d, 2 * SWA_KV_COLS)
    wv = w_in[:, SWA_Q_COLS + SWA_KV_COLS:]
    w_in1 = jnp.concatenate([wq, wk, wv], axis=-1).astype(BF16)
    w_kv1 = jnp.concatenate([wk, wv], axis=-1).astype(BF16)
    rot_groups = (SWA_Q_COLS + 2 * SWA_KV_COLS) // LANES
    m1 = mods[1]
    z1_l = _proj(xl, norm1_g[1], m1, lat_row(tm_i), w_in1, tm=tm_i, rot_groups=rot_groups,
                 tabs=(pair_cos, pair_sin), pos_blocks=n_lat // tm_i, vmem_mib=56, name="odd_in_proj")
    z1_c = _proj(xc, norm1_g[1], m1, ctx_row, w_kv1, tm=tm, name="odd_ctx_kv_proj")
    att = _swa_attention(z1_l, z1_c, swa_sinks[0], batch=batch, n_lat=n_lat, n_ctx=n_ctx)
    xl = _outproj(att, 0, att, 1, w_out1, xl, m1, lat_row(tm_o), 2, tm=tm_o)
    out = _mlp(xl, norm2_g[1], m1, lat_row(tm_f), w1_all, w2_all, 1, final_g=final_norm_g, tm=tm_f)
    return out.reshape(batch, n_lat, d)
```

```python
import functools
import math

import jax
import jax.numpy as jnp
import numpy as np
from jax import lax
from jax.experimental import pallas as pl
from jax.experimental.pallas import tpu as pltpu

F32 = jnp.float32
BF16 = jnp.bfloat16

LANES = 128
MIB = 1 << 20

GRID_W = 64
N_MOD = 6
NORM_EPS = 1e-6
ROPE_BASE = 10000.0
NEG_INF = -1e30

MLA_HEADS = 8
MLA_LORA = 512
MLA_NOPE = 128
MLA_ROPE = 64
MLA_V = 128
MLA_QK_PAD = 2 * LANES
MLA_SCALE = 1.0 / math.sqrt(MLA_NOPE + MLA_ROPE)
MLA_Q_SUB = 512

GMLP_GROUPS = 8
GMLP_DIM = 128
GMLP_CHUNK = 128
GMLP_WIDTH = GMLP_GROUPS * GMLP_DIM
GMLP_BATCH = 2

SWA_HEADS = 32
SWA_KV_HEADS = 4
SWA_GROUP = SWA_HEADS // SWA_KV_HEADS
SWA_HEAD_DIM = 64
SWA_WINDOW = 128
SWA_BLOCK = 128
SWA_SPAN = SWA_BLOCK + 2 * SWA_WINDOW
SWA_SCALE = 1.0 / math.sqrt(SWA_HEAD_DIM)
SWA_Q_COLS = SWA_HEADS * SWA_HEAD_DIM
SWA_KV_COLS = SWA_KV_HEADS * SWA_HEAD_DIM
SWA_UNIT = SWA_GROUP

MOD_ROWS = 16
ROW_CHUNK = 128
STATS_CHUNK = 256
NARROW_ROW = 512
AHEAD_ROWS = 64
WEIGHT_SLOTS = 3
ONES_ROWS = 16
LOG2E = math.log2(math.e)


def _params(semantics, vmem_mib):
    return pltpu.CompilerParams(dimension_semantics=semantics, vmem_limit_bytes=vmem_mib * MIB)


def _rms(x, g):
    return x * lax.rsqrt(jnp.mean(x * x, axis=-1, keepdims=True) + NORM_EPS) * g


def _for_row_chunks(n_rows, fn, rows=ROW_CHUNK):
    rows = min(rows, n_rows)

    def body(r, carry):
        fn(pl.ds(pl.multiple_of(r * rows, rows), rows))
        return carry

    lax.fori_loop(0, n_rows // rows, body, 0)


def _norm_modulate(x_ref, r_ref, store, g, scale=None, shift=None):
    n_rows = x_ref.shape[0]
    gain = g if scale is None else g * (1.0 + scale)

    if x_ref.shape[1] <= NARROW_ROW:
        def whole(sl):
            y = _rms(x_ref[sl, :], gain)
            store(sl, y if shift is None else y + shift)

        _for_row_chunks(n_rows, whole)
        return

    def stats(sl):
        x = x_ref[sl, :]
        r_ref[sl, :] = lax.rsqrt(jnp.mean(x * x, axis=-1, keepdims=True) + NORM_EPS)

    _for_row_chunks(n_rows, stats, rows=STATS_CHUNK)

    def apply(sl):
        y = x_ref[sl, :] * r_ref[sl, :] * gain
        store(sl, y if shift is None else y + shift)

    _for_row_chunks(n_rows, apply)


def _first_block_norm(x_ref, r_ref, h_ref, g, scale, shift):
    @pl.when(pl.program_id(0) == 0)
    def _():
        def store(sl, y):
            h_ref[0, sl, :] = y.astype(BF16)

        _norm_modulate(x_ref, r_ref, store, g, scale, shift)


def _next_block_norm(xn_ref, h_ref, slot, g, scale, shift):
    gain = g * (1.0 + scale)
    for r0 in range(0, xn_ref.shape[0], AHEAD_ROWS):
        rows = slice(r0, r0 + AHEAD_ROWS)
        x = xn_ref[rows, :]
        r = lax.rsqrt(jnp.mean(x * x, axis=-1, keepdims=True) + NORM_EPS)
        h_ref[slot, rows, :] = (x * r * gain + shift).astype(BF16)


def _dot(a, b):
    return jnp.dot(a, b, preferred_element_type=F32)


def _dot_nt(a, b):
    return lax.dot_general(a, b, (((1,), (1,)), ((), ())), preferred_element_type=F32)


def _mod_kernel(c_ref, w_ref, b_ref, o_ref):
    c = c_ref[...]
    s = (c * jax.nn.sigmoid(c)).astype(BF16)
    o_ref[...] = _dot(s, w_ref[...].astype(BF16)) + b_ref[...]


def _modulation(cvec, w_mod, b_mod, *, tn=1024):
    depth, d, n = w_mod.shape
    return pl.pallas_call(
        _mod_kernel,
        out_shape=jax.ShapeDtypeStruct((depth, MOD_ROWS, n), F32),
        grid=(depth, n // tn),
        in_specs=[
            pl.BlockSpec((MOD_ROWS, d), lambda l, j: (0, 0)),
            pl.BlockSpec((None, d, tn), lambda l, j: (l, 0, j)),
            pl.BlockSpec((None, 1, tn), lambda l, j: (l, 0, j)),
        ],
        out_specs=pl.BlockSpec((None, MOD_ROWS, tn), lambda l, j: (l, 0, j)),
        compiler_params=_params(("parallel", "parallel"), 40),
        name="modulation",
    )(cvec, w_mod, b_mod.reshape(depth, 1, n))


def _proj_kernel(*refs, rot_groups):
    if rot_groups:
        (x_ref, xn_ref, g_ref, sh_ref, sc_ref, shn_ref, scn_ref, w_ref, cos_ref, sin_ref, o_ref, h_ref,
         r_ref) = refs
    else:
        x_ref, xn_ref, g_ref, sh_ref, sc_ref, shn_ref, scn_ref, w_ref, o_ref, h_ref, r_ref = refs
    cur = pl.program_id(0) % 2
    _first_block_norm(x_ref, r_ref, h_ref, g_ref[...], sc_ref[...], sh_ref[...])
    z = _dot(h_ref[cur], w_ref[...])
    _next_block_norm(xn_ref, h_ref, 1 - cur, g_ref[...], scn_ref[...], shn_ref[...])

    for gi in range(z.shape[1] // LANES):
        sl = slice(gi * LANES, (gi + 1) * LANES)
        t = z[:, sl]
        if gi < rot_groups:
            t = t * cos_ref[...] + pltpu.roll(t, LANES // 2, 1) * sin_ref[...]
        o_ref[:, sl] = t.astype(o_ref.dtype)


def _proj(x, g, mods, row_of, w, *, tm, rot_groups=0, tabs=(), pos_blocks=1, vmem_mib=48,
          name="proj"):
    t, k = x.shape
    n = w.shape[1]
    n_i = t // tm
    nxt = lambda i: jnp.minimum(i + 1, n_i - 1)
    in_specs = [
        pl.BlockSpec((tm, k), lambda i: (i, 0)),
        pl.BlockSpec((tm, k), lambda i: (nxt(i), 0)),
        pl.BlockSpec((1, k), lambda i: (0, 0)),
        pl.BlockSpec((None, 1, k), lambda i: (row_of(i), 0, 0)),
        pl.BlockSpec((None, 1, k), lambda i: (row_of(i), 0, 1)),
        pl.BlockSpec((None, 1, k), lambda i: (row_of(nxt(i)), 0, 0)),
        pl.BlockSpec((None, 1, k), lambda i: (row_of(nxt(i)), 0, 1)),
        pl.BlockSpec((k, n), lambda i: (0, 0), pipeline_mode=pl.Buffered(1)),
    ]
    in_specs += [pl.BlockSpec((tm, LANES), lambda i: (i % pos_blocks, 0)) for _ in tabs]
    return pl.pallas_call(
        functools.partial(_proj_kernel, rot_groups=rot_groups),
        out_shape=jax.ShapeDtypeStruct((t, n), BF16),
        grid=(n_i,),
        in_specs=in_specs,
        out_specs=pl.BlockSpec((tm, n), lambda i: (i, 0)),
        scratch_shapes=[pltpu.VMEM((2, tm, k), BF16), pltpu.VMEM((tm, 1), F32)],
        compiler_params=_params(("arbitrary",), vmem_mib),
        name=name,
    )(x, x, g.reshape(1, k), mods, mods, mods, mods, w, *tabs)


def _even_in_kernel(x_ref, xn_ref, g_ref, sh_ref, sc_ref, shn_ref, scn_ref, w_ref, cos_ref, sin_ref,
                    gq_ref, gkv_ref, wq_ref, wkv_ref, wsp_ref, bsp_ref, kr_ref, q_ref, kv_ref, gm_ref,
                    h_ref, r_ref, *, lora, q_groups):
    cur = pl.program_id(0) % 2
    _first_block_norm(x_ref, r_ref, h_ref, g_ref[...], sc_ref[...], sh_ref[...])
    z = _dot(h_ref[cur], w_ref[...])
    _next_block_norm(xn_ref, h_ref, 1 - cur, g_ref[...], scn_ref[...], shn_ref[...])

    def rotated(t):
        return t * cos_ref[...] + pltpu.roll(t, LANES // 2, 1) * sin_ref[...]

    gm_lo, gm_hi = 2 * lora, 2 * lora + 2 * GMLP_WIDTH
    kr_ref[...] = rotated(z[:, gm_hi:])
    _gmlp_tiles(z[:, gm_lo:gm_lo + GMLP_WIDTH], z[:, gm_lo + GMLP_WIDTH:gm_hi], wsp_ref, bsp_ref, gm_ref)

    q = _dot(_rms(z[:, :lora], gq_ref[...]).astype(BF16), wq_ref[...])
    for gi in range(q.shape[1] // LANES):
        sl = slice(gi * LANES, (gi + 1) * LANES)
        q_ref[:, sl] = (rotated(q[:, sl]) if gi in q_groups else q[:, sl]).astype(q_ref.dtype)
    kv = _dot(_rms(z[:, lora:2 * lora], gkv_ref[...]).astype(BF16), wkv_ref[...])
    kv_ref[...] = kv.astype(kv_ref.dtype)


def _even_in(x, g, mods, row_of, w, tabs, pos_blocks, gq, gkv, wq, wkv, w_sp, b_sp, *, tm, lora,
             q_groups, vmem_mib=58):
    t, d = x.shape
    n = w.shape[1]
    assert n == 2 * lora + 2 * GMLP_WIDTH + LANES
    once = {"pipeline_mode": pl.Buffered(1)}
    row_blk = lambda i: (i, 0)
    fixed = lambda i: (0, 0)
    n_i = t // tm
    nxt = lambda i: jnp.minimum(i + 1, n_i - 1)
    return pl.pallas_call(
        functools.partial(_even_in_kernel, lora=lora, q_groups=q_groups),
        out_shape=[jax.ShapeDtypeStruct((t, LANES), F32),
                   jax.ShapeDtypeStruct((t, wq.shape[1]), BF16),
                   jax.ShapeDtypeStruct((t, wkv.shape[1]), BF16),
                   jax.ShapeDtypeStruct((t, GMLP_WIDTH), BF16)],
        grid=(n_i,),
        in_specs=[
            pl.BlockSpec((tm, d), row_blk),
            pl.BlockSpec((tm, d), lambda i: (nxt(i), 0)),
            pl.BlockSpec((1, d), fixed),
            pl.BlockSpec((None, 1, d), lambda i: (row_of(i), 0, 0)),
            pl.BlockSpec((None, 1, d), lambda i: (row_of(i), 0, 1)),
            pl.BlockSpec((None, 1, d), lambda i: (row_of(nxt(i)), 0, 0)),
            pl.BlockSpec((None, 1, d), lambda i: (row_of(nxt(i)), 0, 1)),
            pl.BlockSpec((d, n), fixed, **once),
            pl.BlockSpec((tm, LANES), lambda i: (i % pos_blocks, 0)),
            pl.BlockSpec((tm, LANES), lambda i: (i % pos_blocks, 0)),
            pl.BlockSpec((1, lora), fixed),
            pl.BlockSpec((1, lora), fixed),
            pl.BlockSpec(wq.shape, fixed, **once),
            pl.BlockSpec(wkv.shape, fixed, **once),
            pl.BlockSpec(w_sp.shape, lambda i: (0, 0, 0)),
            pl.BlockSpec(b_sp.shape, lambda i: (0, 0, 0)),
        ],
        out_specs=[pl.BlockSpec((tm, LANES), row_blk),
                   pl.BlockSpec((tm, wq.shape[1]), row_blk),
                   pl.BlockSpec((tm, wkv.shape[1]), row_blk),
                   pl.BlockSpec((tm, GMLP_WIDTH), row_blk)],
        scratch_shapes=[pltpu.VMEM((2, tm, d), BF16), pltpu.VMEM((tm, 1), F32)],
        compiler_params=_params(("arbitrary",), vmem_mib),
        name="even_in_proj",
    )(x, x, g.reshape(1, d), mods, mods, mods, mods, w, tabs[0], tabs[1], gq.reshape(1, lora),
      gkv.reshape(1, lora), wq, wkv, w_sp, b_sp)


def _mla_kernel(*refs, n_lat, n_ctx, n_cast):
    n_in = 8
    q_ref, qc_ref, knl_ref, vl_ref, krl_ref, knc_ref, vc_ref, krc_ref = refs[:n_in]
    cast_in = refs[n_in:n_in + n_cast]
    o_ref, oc_ref = refs[n_in + n_cast:n_in + n_cast + 2]
    cast_out = refs[n_in + n_cast + 2:n_in + 2 * n_cast + 2]
    k_scr, vt_scr = refs[n_in + 2 * n_cast + 2:]
    for src, dst in zip(cast_in, cast_out):
        dst[...] = src[...].astype(dst.dtype)

    k_scr[0:n_lat, 0:LANES] = knl_ref[...]
    k_scr[0:n_lat, LANES:] = krl_ref[...].astype(BF16)
    vt_scr[0:MLA_V, 0:n_lat] = vl_ref[...].astype(F32).T.astype(BF16)
    k_scr[n_lat:, 0:LANES] = knc_ref[...]
    k_scr[n_lat:, LANES:] = krc_ref[...].astype(BF16)
    vt_scr[0:MLA_V, n_lat:] = vc_ref[...].astype(F32).T.astype(BF16)
    vt_scr[MLA_V:, :] = jnp.ones((ONES_ROWS, n_lat + n_ctx), BF16)

    def weights(s):
        return jnp.exp2(s - jnp.max(s, axis=0, keepdims=True)).astype(BF16)

    def normalised(o):
        return (o[0:MLA_V] / o[MLA_V:MLA_V + 1]).T

    subs = [slice(r, r + MLA_Q_SUB) for r in range(0, n_lat, MLA_Q_SUB)]

    def scores(i):
        return _dot_nt(k_scr[...], q_ref[subs[i], :])

    n = len(subs)
    s = [scores(i) if i < 2 else None for i in range(n)]
    p = weights(s[0])
    for i, rows in enumerate(subs):
        if i + 2 < n:
            s[i + 2] = scores(i + 2)
        o = _dot(vt_scr[...], p)
        if i + 1 < n:
            p = weights(s[i + 1])
        o_ref[rows, :] = normalised(o).astype(o_ref.dtype)

    p_c = weights(_dot_nt(k_scr[n_lat:, :], qc_ref[...]))
    oc_ref[...] = normalised(_dot(vt_scr[:, n_lat:], p_c)).astype(oc_ref.dtype)


def _mla_attention(q_l, q_c, kv_l, kv_c, kr_l, kr_c, *, batch, casts=()):
    n_lat, n_ctx = q_l.shape[0] // batch, q_c.shape[0] // batch
    assert n_lat % MLA_Q_SUB == 0
    head = lambda b, h: (b, h)
    k_of = lambda b, h: (b, 2 * h)
    v_of = lambda b, h: (b, 2 * h + 1)
    shared = lambda b, h: (b, 0)
    in_specs = [
        pl.BlockSpec((n_lat, MLA_QK_PAD), head),
        pl.BlockSpec((n_ctx, MLA_QK_PAD), head),
        pl.BlockSpec((n_lat, LANES), k_of),
        pl.BlockSpec((n_lat, LANES), v_of),
        pl.BlockSpec((n_lat, LANES), shared),
        pl.BlockSpec((n_ctx, LANES), k_of),
        pl.BlockSpec((n_ctx, LANES), v_of),
        pl.BlockSpec((n_ctx, LANES), shared),
    ]
    args = [q_l, q_c, kv_l, kv_l, kr_l, kv_c, kv_c, kr_c]
    out_shape = [jax.ShapeDtypeStruct((batch * n_lat, MLA_HEADS * MLA_V), BF16),
                 jax.ShapeDtypeStruct((batch * n_ctx, MLA_HEADS * MLA_V), BF16)]
    out_specs = [pl.BlockSpec((n_lat, MLA_V), head), pl.BlockSpec((n_ctx, MLA_V), head)]
    for w in casts:
        assert w.shape[0] == batch * MLA_HEADS
        slab = pl.BlockSpec((None,) + w.shape[1:], lambda b, h: (b * MLA_HEADS + h, 0, 0))
        in_specs.append(slab)
        args.append(w)
        out_specs.append(slab)
        out_shape.append(jax.ShapeDtypeStruct(w.shape, BF16))
    n_keys = n_lat + n_ctx
    outs = pl.pallas_call(
        functools.partial(_mla_kernel, n_lat=n_lat, n_ctx=n_ctx, n_cast=len(casts)),
        out_shape=out_shape,
        grid=(batch, MLA_HEADS),
        in_specs=in_specs,
        out_specs=out_specs,
        scratch_shapes=[pltpu.VMEM((n_keys, MLA_QK_PAD), BF16),
                        pltpu.VMEM((MLA_V + ONES_ROWS, n_keys), BF16)],
        compiler_params=_params(("parallel", "parallel"), 56),
        name="mla_attention",
    )(*args)
    return outs[0], outs[1], outs[2:]


def _gelu(x):
    c = math.sqrt(2.0 / math.pi)
    return x * (0.5 + 0.5 * jnp.tanh(x * (c + (c * 0.044715) * (x * x))))


def _gmlp_tiles(u, v, w_ref, b_ref, o_ref):
    for n0 in range(0, u.shape[0] // GMLP_CHUNK, GMLP_BATCH):
        tiles = [(slice(n * GMLP_CHUNK, (n + 1) * GMLP_CHUNK), slice(g * GMLP_DIM, (g + 1) * GMLP_DIM), g)
                 for n in range(n0, n0 + GMLP_BATCH) for g in range(GMLP_GROUPS)]
        x = [_gelu(v[rows, cols]) for rows, cols, _ in tiles]
        xc = [t - jnp.mean(t, axis=-1, keepdims=True) for t in x]
        var = [jnp.mean(t * t, axis=-1, keepdims=True) for t in xc]
        xn = [(t * lax.rsqrt(s + NORM_EPS)).astype(BF16) for t, s in zip(xc, var)]
        for (rows, cols, g), t in zip(tiles, xn):
            mixed = _dot(w_ref[g], t) + b_ref[g]
            o_ref[rows, cols] = (_gelu(u[rows, cols]) * mixed).astype(o_ref.dtype)


def _swa_kernel(sink_ref, q_ref, kl_ref, vl_ref, kc_ref, vc_ref, o_ref, *, tq, n_lat):
    start = pl.program_id(1) * tq
    span0 = pl.multiple_of(jnp.clip(start - SWA_WINDOW, 0, n_lat - SWA_SPAN), SWA_WINDOW)
    k_span = kl_ref[pl.ds(span0, SWA_SPAN), :]
    k_ctx = kc_ref[...]
    vt_span = vl_ref[pl.ds(span0, SWA_SPAN), :].astype(F32).T.astype(BF16)
    vt_ctx = vc_ref[...].astype(F32).T.astype(BF16)
    ones_span = jnp.ones((ONES_ROWS, vt_span.shape[1]), BF16)
    ones_ctx = jnp.ones((ONES_ROWS, vt_ctx.shape[1]), BF16)

    k_pos = span0 + lax.broadcasted_iota(jnp.int32, (SWA_SPAN, tq), 0)
    q_pos = start + lax.broadcasted_iota(jnp.int32, (SWA_SPAN, tq), 1)
    bias = jnp.where(jnp.abs(k_pos - q_pos) <= SWA_WINDOW, 0.0, NEG_INF).astype(F32)
    bias = jnp.concatenate([bias] * SWA_UNIT, axis=1)

    lane = lax.broadcasted_iota(jnp.int32, (tq, LANES), 1)
    first = (lane % SWA_HEAD_DIM) < (SWA_HEAD_DIM // 2)
    keep = [jnp.where(first, 1.0, 0.0).astype(BF16), jnp.where(first, 0.0, 1.0).astype(BF16)]

    units = [range(h0, h0 + SWA_UNIT) for h0 in range(0, SWA_HEADS, SWA_UNIT)]

    def scores(heads):
        g = heads[0] // SWA_GROUP
        k_cols = slice(g * LANES, (g + 1) * LANES)
        q = jnp.concatenate(
            [q_ref[:, (h // 2) * LANES:(h // 2 + 1) * LANES] * keep[h % 2] for h in heads], axis=0)
        return _dot_nt(k_span[:, k_cols], q) + bias, _dot_nt(k_ctx[:, k_cols], q)

    s_next = scores(units[0])
    for u, heads in enumerate(units):
        g = heads[0] // SWA_GROUP
        kv_cols = slice(g * SWA_HEAD_DIM, (g + 1) * SWA_HEAD_DIM)
        s_l, s_c = s_next
        if u + 1 < len(units):
            s_next = scores(units[u + 1])
        sink = jnp.concatenate([jnp.full((1, tq), sink_ref[h] * LOG2E, F32) for h in heads], axis=1)
        m = jnp.maximum(jnp.maximum(jnp.max(s_l, axis=0, keepdims=True),
                                    jnp.max(s_c, axis=0, keepdims=True)), sink)
        p_l = jnp.exp2(s_l - m).astype(BF16)
        p_c = jnp.exp2(s_c - m).astype(BF16)
        vt_l = jnp.concatenate([vt_span[kv_cols, :], ones_span], axis=0)
        vt_c = jnp.concatenate([vt_ctx[kv_cols, :], ones_ctx], axis=0)
        o = _dot(vt_l, p_l) + _dot(vt_c, p_c)
        dim = SWA_HEAD_DIM
        o = o[0:dim] / (o[dim:dim + 1] + jnp.exp2(sink - m))
        for pair in range(SWA_UNIT // 2):
            cols = [slice((2 * pair + r) * tq, (2 * pair + r + 1) * tq) for r in range(2)]
            both = jnp.concatenate([o[:, cols[0]], o[:, cols[1]]], axis=0)
            group = heads[0] // 2 + pair
            o_ref[:, group * LANES:(group + 1) * LANES] = both.T.astype(o_ref.dtype)


def _swa_attention(z_l, z_c, sinks, *, batch, n_lat, n_ctx, tq=SWA_BLOCK):
    nq_blocks = n_lat // tq
    k_cols = 2 * SWA_KV_COLS
    return pl.pallas_call(
        functools.partial(_swa_kernel, tq=tq, n_lat=n_lat),
        out_shape=jax.ShapeDtypeStruct((batch * n_lat, SWA_Q_COLS), BF16),
        grid=(batch, nq_blocks),
        in_specs=[
            pl.BlockSpec(memory_space=pltpu.SMEM),
            pl.BlockSpec((tq, SWA_Q_COLS), lambda b, i: (b * nq_blocks + i, 0)),
            pl.BlockSpec((n_lat, k_cols), lambda b, i: (b, SWA_Q_COLS // k_cols)),
            pl.BlockSpec((n_lat, SWA_KV_COLS), lambda b, i: (b, (SWA_Q_COLS + k_cols) // SWA_KV_COLS)),
            pl.BlockSpec((n_ctx, k_cols), lambda b, i: (b, 0)),
            pl.BlockSpec((n_ctx, SWA_KV_COLS), lambda b, i: (b, k_cols // SWA_KV_COLS)),
        ],
        out_specs=pl.BlockSpec((tq, SWA_Q_COLS), lambda b, i: (b * nq_blocks + i, 0)),
        compiler_params=_params(("parallel", "arbitrary"), 48),
        name="swa_attention",
    )(sinks, z_l, z_l, z_l, z_c, z_c)


def _outproj_kernel(a1_ref, a2_ref, w1_ref, w2_ref, x_ref, gate_ref, o_ref):
    y = _dot(a1_ref[...], w1_ref[...]) + _dot(a2_ref[...], w2_ref[...])
    o_ref[...] = x_ref[...] + gate_ref[...] * y


def _outproj(a1, a1_col, a2, a2_col, w, x, mods, row_of, k_gate, *, tm=512):
    t, d = x.shape
    half = w.shape[0] // 2
    return pl.pallas_call(
        _outproj_kernel,
        out_shape=jax.ShapeDtypeStruct((t, d), F32),
        grid=(t // tm,),
        in_specs=[
            pl.BlockSpec((tm, half), lambda i: (i, a1_col)),
            pl.BlockSpec((tm, half), lambda i: (i, a2_col)),
            pl.BlockSpec((half, d), lambda i: (0, 0)),
            pl.BlockSpec((half, d), lambda i: (1, 0)),
            pl.BlockSpec((tm, d), lambda i: (i, 0)),
            pl.BlockSpec((None, 1, d), lambda i: (row_of(i), 0, k_gate)),
        ],
        out_specs=pl.BlockSpec((tm, d), lambda i: (i, 0)),
        compiler_params=_params(("parallel",), 48),
        name="outproj",
    )(a1, a2, w, w, x, mods)


def _mlp_kernel(*refs, final, layer, n_i, n_k, tf):
    fg_ref = None
    if final:
        (x_ref, g_ref, sh_ref, sc_ref, gate_ref, w1_hbm, w2_hbm, fg_ref, o_ref, h_ref, r_ref,
         w1_buf, w2_buf, sem) = refs
    else:
        (x_ref, g_ref, sh_ref, sc_ref, gate_ref, w1_hbm, w2_hbm, o_ref, h_ref, r_ref,
         w1_buf, w2_buf, sem) = refs
    i, k = pl.program_id(0), pl.program_id(1)
    step, n_steps = i * n_k + k, n_i * n_k

    def tile_copies(kk, slot):
        cols = pl.ds(pl.multiple_of(kk * tf, tf), tf)
        return (pltpu.make_async_copy(w1_hbm.at[layer, :, cols], w1_buf.at[slot], sem.at[0, slot]),
                pltpu.make_async_copy(w2_hbm.at[layer, cols, :], w2_buf.at[slot], sem.at[1, slot]))

    def start(ahead):
        for copy in tile_copies((k + ahead) % n_k, (step + ahead) % WEIGHT_SLOTS):
            copy.start()

    @pl.when(step == 0)
    def _():
        start(0)
        start(1)

    @pl.when(step + 2 < n_steps)
    def _():
        start(2)

    slot = step % WEIGHT_SLOTS
    for copy in tile_copies(k, slot):
        copy.wait()

    def ff_chunk():
        a = jnp.square(jnp.maximum(_dot(h_ref[...], w1_buf[slot]), 0.0)).astype(BF16)
        return _dot(a, w2_buf[slot])

    @pl.when(k == 0)
    def _():
        def store(sl, y):
            h_ref[sl, :] = y.astype(BF16)

        _norm_modulate(x_ref, r_ref, store, g_ref[...], sc_ref[...], sh_ref[...])
        o_ref[...] = ff_chunk()

    @pl.when(k > 0)
    def _():
        o_ref[...] += ff_chunk()

    @pl.when(k == pl.num_programs(1) - 1)
    def _():
        def rows(sl):
            out = x_ref[sl, :] + gate_ref[...] * o_ref[sl, :]
            if final:
                out = _rms(out, fg_ref[...])
            o_ref[sl, :] = out

        _for_row_chunks(x_ref.shape[0], rows)


def _mlp(x, g, mods, row_of, w1, w2, layer, *, final_g=None, tm=512, tf=1024, vmem_mib=56):
    t, d = x.shape
    n_i, n_k = t // tm, w1.shape[2] // tf
    assert n_i * n_k >= 2
    in_specs = [
        pl.BlockSpec((tm, d), lambda i, k: (i, 0)),
        pl.BlockSpec((1, d), lambda i, k: (0, 0)),
        pl.BlockSpec((None, 1, d), lambda i, k: (row_of(i), 0, 3)),
        pl.BlockSpec((None, 1, d), lambda i, k: (row_of(i), 0, 4)),
        pl.BlockSpec((None, 1, d), lambda i, k: (row_of(i), 0, 5)),
        pl.BlockSpec(memory_space=pl.ANY),
        pl.BlockSpec(memory_space=pl.ANY),
    ]
    args = [x, g.reshape(1, d), mods, mods, mods, w1, w2]
    if final_g is not None:
        in_specs.append(pl.BlockSpec((1, d), lambda i, k: (0, 0)))
        args.append(final_g.reshape(1, d))
    return pl.pallas_call(
        functools.partial(_mlp_kernel, final=final_g is not None, layer=layer, n_i=n_i, n_k=n_k, tf=tf),
        out_shape=jax.ShapeDtypeStruct((t, d), F32),
        grid=(n_i, n_k),
        in_specs=in_specs,
        out_specs=pl.BlockSpec((tm, d), lambda i, k: (i, 0)),
        scratch_shapes=[pltpu.VMEM((tm, d), BF16), pltpu.VMEM((tm, 1), F32),
                        pltpu.VMEM((WEIGHT_SLOTS, d, tf), BF16), pltpu.VMEM((WEIGHT_SLOTS, tf, d), BF16),
                        pltpu.SemaphoreType.DMA((2, WEIGHT_SLOTS))],
        compiler_params=_params(("arbitrary", "arbitrary"), vmem_mib),
        name="mlp",
    )(*args)


def _axial_angles(n_tokens, rot_dim):
    t = np.arange(n_tokens)
    row = (t // GRID_W).astype(np.float32)
    col = (t % GRID_W).astype(np.float32)
    n_freq = rot_dim // 4
    inv_freq = np.float32(ROPE_BASE) ** (-np.arange(n_freq, dtype=np.float32) / np.float32(n_freq))
    ang = np.concatenate([row[:, None] * inv_freq, col[:, None] * inv_freq], axis=-1)
    return np.cos(ang).astype(np.float32), np.sin(ang).astype(np.float32)


def _swap_halves(w):
    half = w.shape[-1] // 2
    return jnp.concatenate([w[..., half:], w[..., :half]], axis=-1)


def kernel(x, c, ctx, c_ctx, norm1_g, w_mod, b_mod, norm2_g, w_ff1, w_ff2, even_w_in, mla_q_norm_g,
           mla_w_uq, mla_kv_norm_g, mla_w_ukv, gmlp_w_sp, gmlp_b_sp, even_w_out, odd_w_in, swa_sinks,
           odd_w_out, final_norm_g):
    batch, n_lat, d = x.shape
    n_ctx = ctx.shape[1]
    assert batch + 1 <= MOD_ROWS
    xl = x.reshape(batch * n_lat, d)
    xc = ctx.reshape(batch * n_ctx, d)

    tm = 1024

    def lat_row(block_rows):
        per_sample = n_lat // block_rows
        return lambda i: i // per_sample

    ctx_row = lambda i: batch

    cvec = jnp.concatenate([c, c_ctx[None, :], jnp.zeros((MOD_ROWS - batch - 1, d), F32)], axis=0)
    mods = _modulation(cvec, w_mod, b_mod)
    mods = mods.reshape(mods.shape[0], MOD_ROWS, 1, N_MOD * d)

    cos_m, sin_m = _axial_angles(n_lat, MLA_ROPE)
    zeros64 = np.zeros((n_lat, LANES // 2), np.float32)
    fold_cos = jnp.asarray(np.concatenate([cos_m, cos_m, zeros64], axis=-1))
    fold_sin = jnp.asarray(np.concatenate([-sin_m, sin_m, zeros64], axis=-1))
    keep = jnp.concatenate([jnp.ones((tm, LANES // 2), F32), jnp.zeros((tm, LANES // 2), F32)], axis=-1)
    drop = jnp.zeros((tm, LANES), F32)

    w_in = even_w_in[0]
    cq_w, ckv_w = w_in[:, :MLA_LORA], w_in[:, MLA_LORA:2 * MLA_LORA]
    kr_w = w_in[:, 2 * MLA_LORA:2 * MLA_LORA + MLA_ROPE]
    gm_w = w_in[:, 2 * MLA_LORA + MLA_ROPE:]
    w_in0 = jnp.concatenate([cq_w, ckv_w, gm_w, kr_w, _swap_halves(kr_w)], axis=-1).astype(BF16)

    wq = mla_w_uq[0].reshape(MLA_LORA, MLA_HEADS, MLA_NOPE + MLA_ROPE)
    wq_rope = wq[..., MLA_NOPE:]
    w_uq = jnp.concatenate([wq[..., :MLA_NOPE], wq_rope, _swap_halves(wq_rope)], axis=-1)
    w_uq = (w_uq * (MLA_SCALE * LOG2E)).reshape(MLA_LORA, MLA_HEADS * MLA_QK_PAD).astype(BF16)
    q_groups = tuple(range(1, 2 * MLA_HEADS, 2))
    w_ukv = mla_w_ukv[0].astype(BF16)
    w_sp = gmlp_w_sp[0].astype(BF16)
    b_sp = gmlp_b_sp[0][:, :, None]
    tm_i = 512

    def layer0_tokens(xs, tabs, latent):
        row_of = lat_row(tm_i) if latent else ctx_row
        pos_blocks = n_lat // tm_i if latent else 1
        return _even_in(xs, norm1_g[0], mods[0], row_of, w_in0, tabs, pos_blocks, mla_q_norm_g[0],
                        mla_kv_norm_g[0], w_uq, w_ukv, w_sp, b_sp, tm=tm_i, lora=MLA_LORA,
                        q_groups=q_groups)

    kr_l, q_l, kv_l, gm_l = layer0_tokens(xl, (fold_cos, fold_sin), True)
    kr_c, q_c, kv_c, gm_c = layer0_tokens(xc, (keep, drop), False)
    cast_steps = batch * MLA_HEADS
    late = (w_ff1, w_ff2, even_w_out, odd_w_out)
    slabs = [w.reshape(cast_steps, -1, w.shape[-1]) for w in late]
    att_l, att_c, late_bf16 = _mla_attention(q_l, q_c, kv_l, kv_c, kr_l, kr_c, batch=batch, casts=slabs)
    w1_all, w2_all, w_out0, w_out1 = [b.reshape(w.shape) for b, w in zip(late_bf16, late)]
    w_out0, w_out1 = w_out0[0], w_out1[0]

    tm_o = 512
    xl = _outproj(att_l, 0, gm_l, 0, w_out0, xl, mods[0], lat_row(tm_o), 2, tm=tm_o)
    xc = _outproj(att_c, 0, gm_c, 0, w_out0, xc, mods[0], ctx_row, 2, tm=tm_o)
    tm_f = 512
    xl = _mlp(xl, norm2_g[0], mods[0], lat_row(tm_f), w1_all, w2_all, 0, tm=tm_f)
    xc = _mlp(xc, norm2_g[0], mods[0], ctx_row, w1_all, w2_all, 0, tm=tm_f)

    cos_s, sin_s = _axial_angles(n_lat, SWA_HEAD_DIM)
    pair_cos = jnp.asarray(np.concatenate([cos_s] * 4, axis=-1))
    pair_sin = jnp.asarray(np.concatenate([-sin_s, -sin_s, sin_s, sin_s], axis=-1))
    half = SWA_HEAD_DIM // 2
    w_in = odd_w_in[0]
    wq = (w_in[:, :SWA_Q_COLS] * (SWA_SCALE * LOG2E)).reshape(d, SWA_HEADS // 2, 2, 2, half)
    wq = wq.transpose(0, 1, 3, 2, 4).reshape(d, SWA_Q_COLS)
    wk = w_in[:, SWA_Q_COLS:SWA_Q_COLS + SWA_KV_COLS].reshape(d, SWA_KV_HEADS, 2, 1, half)
    wk = jnp.broadcast_to(wk, (d, SWA_KV_HEADS, 2, 2, half)).reshape(d, 2 * SWA_KV_COLS)
    wv = w_in[:, SWA_Q_COLS + SWA_KV_COLS:]
    w_in1 = jnp.concatenate([wq, wk, wv], axis=-1).astype(BF16)
    w_kv1 = jnp.concatenate([wk, wv], axis=-1).astype(BF16)
    rot_groups = (SWA_Q_COLS + 2 * SWA_KV_COLS) // LANES
    m1 = mods[1]
    z1_l = _proj(xl, norm1_g[1], m1, lat_row(tm_i), w_in1, tm=tm_i, rot_groups=rot_groups,
                 tabs=(pair_cos, pair_sin), pos_blocks=n_lat // tm_i, vmem_mib=56, name="odd_in_proj")
    z1_c = _proj(xc, norm1_g[1], m1, ctx_row, w_kv1, tm=tm, name="odd_ctx_kv_proj")
    att = _swa_attention(z1_l, z1_c, swa_sinks[0], batch=batch, n_lat=n_lat, n_ctx=n_ctx)
    xl = _outproj(att, 0, att, 1, w_out1, xl, m1, lat_row(tm_o), 2, tm=tm_o)
    out = _mlp(xl, norm2_g[1], m1, lat_row(tm_f), w1_all, w2_all, 1, final_g=final_norm_g, tm=tm_f)
    return out.reshape(batch, n_lat, d)
```
